```python
import functools
import jax
import jax.numpy as jnp
from jax import lax
import numpy as np

D_MODEL = 1024
BATCH = 32
SEQ = 2048
DEPTH = 1
DEC_BATCH = 128
DEC_SEQ = 4
PAST_LEN = 8192
PAGE_SIZE = 128

GDN_HEADS = 8
GDN_HEAD_DIM = 128
GDN_DIM = GDN_HEADS * GDN_HEAD_DIM
GDN_QKV_DIM = 3 * GDN_DIM
CONV_WIDTH = 4
GDN_CHUNK = 64
NSA_HEADS = 8
NSA_KV_HEADS = 2
NSA_GROUP = NSA_HEADS // NSA_KV_HEADS
NSA_HEAD_DIM = 64
NSA_DIM = NSA_HEADS * NSA_HEAD_DIM
NSA_KV_DIM = NSA_KV_HEADS * NSA_HEAD_DIM
CMP_STRIDE = 16
CMP_LEN = 2 * CMP_STRIDE
SEL_BLOCK = 64
SEL_RATIO = SEL_BLOCK // CMP_STRIDE
N_SELECT = 16
WINDOW = 512
NSA_QBLOCK = 16
FORCE_BONUS = 100.0
IN_SPLITS = (GDN_QKV_DIM, GDN_HEADS, GDN_HEADS, GDN_DIM, NSA_DIM, 6 * NSA_KV_DIM, 3 * NSA_HEADS, NSA_DIM, 2 * D_MODEL)
IN_DIM = GDN_QKV_DIM + 2 * GDN_HEADS + GDN_DIM + 2 * NSA_DIM + 6 * NSA_KV_DIM + 3 * NSA_HEADS + 2 * D_MODEL
RMS_EPS = 1e-6
NEG_INF = -1e30

kernel_name = 'gdn_nsa_parallel_hybrid_step'


def rmsnorm(x, gain):
    xf = x.astype(jnp.float32)
    y = xf * lax.rsqrt(jnp.mean(xf * xf, axis=-1, keepdims=True) + RMS_EPS)
    return (y * gain.astype(jnp.float32)).astype(x.dtype)


def l2norm(x):
    xf = x.astype(jnp.float32)
    return xf * lax.rsqrt(jnp.sum(xf * xf, axis=-1, keepdims=True) + 1e-6)


def masked_softmax(s, mask):
    s = jnp.where(mask, s.astype(jnp.float32), NEG_INF)
    e = jnp.where(mask, jnp.exp(s - jnp.max(s, axis=-1, keepdims=True)), 0.0)
    return e / jnp.maximum(jnp.sum(e, axis=-1, keepdims=True), 1e-30)


def causal_conv_silu(xp, w):
    t = xp.shape[1] - CONV_WIDTH + 1
    y = sum(xp[:, j:j + t] * w[j] for j in range(CONV_WIDTH))
    return jax.nn.silu(y)


def gated_delta_chunked(q, k, v, g, beta, s0):
    b, t, h, dk = q.shape
    dv = v.shape[-1]
    c = min(GDN_CHUNK, t)
    pad = (-t) % c
    n = (t + pad) // c

    def blocks(a):
        a = jnp.pad(a, [(0, 0), (0, pad)] + [(0, 0)] * (a.ndim - 2))
        return jnp.moveaxis(a.reshape((b, n, c) + a.shape[2:]), 3, 1)

    q, k, v, g, beta = (blocks(a) for a in (q, k, v, g, beta))
    gc = jnp.cumsum(g, axis=-1)
    causal = jnp.tril(jnp.ones((c, c), dtype=bool))
    strict = jnp.tril(jnp.ones((c, c), dtype=bool), -1)
    decay = jnp.exp(jnp.where(causal, gc[..., :, None] - gc[..., None, :], -jnp.inf))
    a_mat = jnp.where(strict, beta[..., :, None] * jnp.einsum('bhnid,bhnjd->bhnij', k, k) * decay, 0.0)
    rhs = jnp.concatenate([v * beta[..., None], k * (beta * jnp.exp(gc))[..., None]], axis=-1)
    sol = lax.linalg.triangular_solve(a_mat + jnp.eye(c, dtype=a_mat.dtype), rhs,
                                      left_side=True, lower=True, unit_diagonal=True)
    u, w = sol[..., :dv], sol[..., dv:]
    attn = jnp.einsum('bhnid,bhnjd->bhnij', q, k) * decay
    q_dec = q * jnp.exp(gc)[..., None]
    k_dec = k * jnp.exp(gc[..., -1:] - gc)[..., None]
    last = jnp.exp(gc[..., -1])

    def step(s, xs):
        u_c, w_c, a_c, qd_c, kd_c, l_c = xs
        v_new = u_c - jnp.einsum('bhck,bhkv->bhcv', w_c, s)
        o_c = jnp.einsum('bhck,bhkv->bhcv', qd_c, s) + jnp.einsum('bhij,bhjv->bhiv', a_c, v_new)
        s = s * l_c[..., None, None] + jnp.einsum('bhck,bhcv->bhkv', kd_c, v_new)
        return s, o_c

    xs = tuple(jnp.moveaxis(a, 2, 0) for a in (u, w, attn, q_dec, k_dec, last))
    s_fin, o = lax.scan(step, s0, xs)
    o = jnp.moveaxis(jnp.moveaxis(o, 0, 2), 1, 3).reshape(b, n * c, h, dv)[:, :t]
    return o, s_fin


def gdn_branch(qkv_p, a, bt, z, s0, conv_w, a_log, dt_bias, head_gain):
    qkv = causal_conv_silu(qkv_p, conv_w)
    b, t, _ = qkv.shape
    shp = (b, t, GDN_HEADS, GDN_HEAD_DIM)
    q, k, v = jnp.split(qkv, 3, axis=-1)
    q = l2norm(q.reshape(shp)) * GDN_HEAD_DIM ** -0.5
    k = l2norm(k.reshape(shp))
    v = v.reshape(shp).astype(jnp.float32)
    g = -jnp.exp(a_log.astype(jnp.float32)) * jax.nn.softplus(a.astype(jnp.float32) + dt_bias.astype(jnp.float32))
    beta = jax.nn.sigmoid(bt.astype(jnp.float32))
    o, s_new = gated_delta_chunked(q, k, v, g, beta, s0.astype(jnp.float32))
    o = rmsnorm(o, head_gain) * jax.nn.silu(z.astype(jnp.float32)).reshape(shp)
    return o.reshape(b, t, GDN_DIM), s_new


def compress(kv, pos_w, lin_w):
    b, l = kv.shape[:2]
    m = l // CMP_STRIDE
    ch = kv[:, :m * CMP_STRIDE].reshape(b, m, CMP_STRIDE, NSA_KV_HEADS, NSA_HEAD_DIM)
    lo = jnp.einsum('bmjgd,jd->bmgd', ch, pos_w[:CMP_STRIDE])
    hi = jnp.einsum('bmjgd,jd->bmgd', ch, pos_w[CMP_STRIDE:])
    return jnp.einsum('bmgd,de->bmge', lo[:, :-1] + hi[:, 1:], lin_w)


def nsa_attend(q, pos_q, kc, vc, gather_sel, n_blocks, kw, vw, pos_w, gates):
    b, tq = q.shape[:2]
    qg = q.reshape(b, tq, NSA_KV_HEADS, NSA_GROUP, NSA_HEAD_DIM) * NSA_HEAD_DIM ** -0.5
    nc = kc.shape[1]
    c_end = jnp.arange(nc) * CMP_STRIDE + CMP_LEN - 1
    mask_c = (c_end[None, :] <= pos_q[:, None])[None, :, None, None, :]
    p_c = masked_softmax(jnp.einsum('bqghd,bcgd->bqghc', qg, kc), mask_c)
    o_c = jnp.einsum('bqghc,bcgd->bqghd', p_c, vc)
    imp = jnp.pad(p_c.sum(axis=3), ((0, 0), (0, 0), (0, 0), (0, n_blocks * SEL_RATIO - nc)))
    imp = imp.reshape(b, tq, NSA_KV_HEADS, n_blocks, SEL_RATIO).sum(axis=-1)
    blk = jnp.arange(n_blocks)[None, :]
    cur = pos_q[:, None] // SEL_BLOCK
    valid = blk * SEL_BLOCK <= pos_q[:, None]
    forced = (blk == 0) | (blk == cur) | (blk == cur - 1)
    score = jnp.where(valid[None, :, None, :], imp + FORCE_BONUS * forced[None, :, None, :], -1.0)
    _, idx = lax.top_k(score, min(N_SELECT, n_blocks))
    n_sel = idx.shape[-1]
    k_s, v_s = gather_sel(idx)
    k_s = k_s.reshape(b, tq, NSA_KV_HEADS, n_sel * SEL_BLOCK, NSA_HEAD_DIM)
    v_s = v_s.reshape(b, tq, NSA_KV_HEADS, n_sel * SEL_BLOCK, NSA_HEAD_DIM)
    k_pos = (idx[..., None] * SEL_BLOCK + jnp.arange(SEL_BLOCK)).reshape(b, tq, NSA_KV_HEADS, 1, n_sel * SEL_BLOCK)
    mask_s = k_pos <= pos_q[None, :, None, None, None]
    p_s = masked_softmax(jnp.einsum('bqghd,bqgmd->bqghm', qg, k_s), mask_s)
    o_s = jnp.einsum('bqghm,bqgmd->bqghd', p_s, v_s)
    dist = pos_q[:, None] - pos_w[None, :]
    mask_w = ((dist >= 0) & (dist <= WINDOW) & (pos_w[None, :] >= 0))[None, :, None, None, :]
    p_w = masked_softmax(jnp.einsum('bqghd,bkgd->bqghk', qg, kw), mask_w)
    o_w = jnp.einsum('bqghk,bkgd->bqghd', p_w, vw)
    gt = gates.reshape(b, tq, NSA_KV_HEADS, NSA_GROUP, 3)
    o = gt[..., 0:1] * o_c + gt[..., 1:2] * o_s + gt[..., 2:3] * o_w
    return o.reshape(b, tq, NSA_DIM)


def nsa_prompt(q, gates, k_cmp, v_cmp, k_slc, v_slc, k_win, v_win, pos_k, w_k, pos_v, w_v):
    b, t = q.shape[:2]
    kc = compress(k_cmp, pos_k, w_k)
    vc = compress(v_cmp, pos_v, w_v)
    n_blocks = t // SEL_BLOCK
    kb = k_slc.reshape(b, n_blocks, SEL_BLOCK, NSA_KV_HEADS, NSA_HEAD_DIM)
    vb = v_slc.reshape(b, n_blocks, SEL_BLOCK, NSA_KV_HEADS, NSA_HEAD_DIM)
    b_ix = jnp.arange(b)[:, None, None, None]
    g_ix = jnp.arange(NSA_KV_HEADS)[None, None, :, None]

    def gather_sel(idx):
        return kb[b_ix, idx, :, g_ix, :], vb[b_ix, idx, :, g_ix, :]

    span = WINDOW + NSA_QBLOCK
    kw_pad = jnp.pad(k_win, ((0, 0), (WINDOW, 0), (0, 0), (0, 0)))
    vw_pad = jnp.pad(v_win, ((0, 0), (WINDOW, 0), (0, 0), (0, 0)))
    nqb = t // NSA_QBLOCK
    qb = q.reshape(b, nqb, NSA_QBLOCK, NSA_DIM).swapaxes(0, 1)
    gb = gates.reshape(b, nqb, NSA_QBLOCK, 3 * NSA_HEADS).swapaxes(0, 1)

    def one_block(args):
        q_blk, g_blk, q0 = args
        pos_q = q0 + jnp.arange(NSA_QBLOCK, dtype=jnp.int32)
        kw = lax.dynamic_slice_in_dim(kw_pad, q0, span, axis=1)
        vw = lax.dynamic_slice_in_dim(vw_pad, q0, span, axis=1)
        pos_w = q0 - WINDOW + jnp.arange(span, dtype=jnp.int32)
        return nsa_attend(q_blk, pos_q, kc, vc, gather_sel, n_blocks, kw, vw, pos_w, g_blk)

    o = lax.map(one_block, (qb, gb, jnp.arange(nqb, dtype=jnp.int32) * NSA_QBLOCK))
    keep = min(WINDOW, t)
    return o.swapaxes(0, 1).reshape(b, t, NSA_DIM), (k_win[:, t - keep:], v_win[:, t - keep:])


def nsa_sample(q, gates, k_cmp, v_cmp, k_slc, v_slc, k_win, v_win,
               pool_k_cmp, pool_v_cmp, pool_k_slc, pool_v_slc, buf_k_win, buf_v_win, page_table,
               pos_k, w_k, pos_v, w_v):
    b, t = q.shape[:2]
    n_pages = page_table.shape[1]
    page = pool_k_cmp.shape[1]
    past = n_pages * page
    pos_q = past + jnp.arange(t, dtype=jnp.int32)

    def past_rows(pool):
        return pool[page_table].reshape(b, past, NSA_KV_HEADS, NSA_HEAD_DIM)

    kc = compress(jnp.concatenate([past_rows(pool_k_cmp), k_cmp.astype(pool_k_cmp.dtype)], axis=1), pos_k, w_k)
    vc = compress(jnp.concatenate([past_rows(pool_v_cmp), v_cmp.astype(pool_v_cmp.dtype)], axis=1), pos_v, w_v)
    n_blocks = -(-(past + t) // SEL_BLOCK)
    past_blocks = past // SEL_BLOCK
    per_page = page // SEL_BLOCK
    new_blocks = n_blocks - past_blocks
    pad_new = new_blocks * SEL_BLOCK - t
    kn = jnp.pad(k_slc, ((0, 0), (0, pad_new), (0, 0), (0, 0))).reshape(b, new_blocks, SEL_BLOCK, NSA_KV_HEADS, NSA_HEAD_DIM)
    vn = jnp.pad(v_slc, ((0, 0), (0, pad_new), (0, 0), (0, 0))).reshape(b, new_blocks, SEL_BLOCK, NSA_KV_HEADS, NSA_HEAD_DIM)
    kp = pool_k_slc.reshape(-1, SEL_BLOCK, NSA_KV_HEADS, NSA_HEAD_DIM)
    vp = pool_v_slc.reshape(-1, SEL_BLOCK, NSA_KV_HEADS, NSA_HEAD_DIM)
    b_ix = jnp.arange(b)[:, None, None, None]
    g_ix = jnp.arange(NSA_KV_HEADS)[None, None, :, None]

    def gather_sel(idx):
        is_past = (idx < past_blocks)[..., None, None]
        ip = jnp.minimum(idx, past_blocks - 1)
        phys = page_table[b_ix, ip // per_page] * per_page + ip % per_page
        inew = jnp.clip(idx - past_blocks, 0, new_blocks - 1)
        k_sel = jnp.where(is_past, kp[phys, :, g_ix, :], kn[b_ix, inew, :, g_ix, :].astype(kp.dtype))
        v_sel = jnp.where(is_past, vp[phys, :, g_ix, :], vn[b_ix, inew, :, g_ix, :].astype(vp.dtype))
        return k_sel, v_sel

    w_buf = buf_k_win.shape[1]
    kw = jnp.concatenate([buf_k_win, k_win.astype(buf_k_win.dtype)], axis=1)
    vw = jnp.concatenate([buf_v_win, v_win.astype(buf_v_win.dtype)], axis=1)
    pos_w = past - w_buf + jnp.arange(w_buf + t, dtype=jnp.int32)
    o = nsa_attend(q, pos_q, kc, vc, gather_sel, n_blocks, kw, vw, pos_w, gates)
    keep = min(WINDOW, w_buf + t)
    return o, (kw[:, w_buf + t - keep:], vw[:, w_buf + t - keep:])


def layer(x, conv_prefix, s0, nsa_fn, norm_gain, w_in, conv_w, a_log, dt_bias, gdn_gain,
          w_branch_a, w_branch_b, w_out):
    b, t, _ = x.shape
    h = rmsnorm(x, norm_gain)
    bounds = np.cumsum(IN_SPLITS)[:-1].tolist()
    qkv, a, bt, z_a, q_b, kv_b, g_b, z_b, merge = jnp.split(jnp.einsum('btd,de->bte', h, w_in), bounds, axis=-1)
    qkv_p = jnp.concatenate([conv_prefix.astype(qkv.dtype), qkv], axis=1)
    o_a, s_new = gdn_branch(qkv_p, a, bt, z_a, s0, conv_w, a_log, dt_bias, gdn_gain)
    kvs = [r.reshape(b, t, NSA_KV_HEADS, NSA_HEAD_DIM) for r in jnp.split(kv_b, 6, axis=-1)]
    o_nsa, win_state = nsa_fn(q_b, jax.nn.sigmoid(g_b.astype(jnp.float32)), *kvs)
    o_b = o_nsa * jax.nn.silu(z_b.astype(jnp.float32))
    gate_a, gate_b = jnp.split(jax.nn.sigmoid(merge.astype(jnp.float32)), 2, axis=-1)
    mixed = (gate_a * jnp.einsum('bte,ed->btd', o_a, w_branch_a)
             + gate_b * jnp.einsum('bte,ed->btd', o_b, w_branch_b))
    x_new = x + jnp.einsum('btd,de->bte', mixed, w_out).astype(x.dtype)
    new_state = (kvs[0], kvs[1], kvs[2], kvs[3], win_state[0], win_state[1],
                 qkv_p[:, -(CONV_WIDTH - 1):], s_new)
    return x_new, new_state


def setup_inputs(seed: int = 0) -> dict:
    key = jax.random.key(seed)
    ks = jax.random.split(key, 32)
    f32 = jnp.float32
    n_pages = PAST_LEN // PAGE_SIZE
    n_pool = (DEC_BATCH * n_pages * 5) // 4
    w_buf = min(WINDOW, PAST_LEN)

    def nrm(k, shape, scale):
        return jax.random.normal(k, shape, f32) * scale

    pool_shape = (DEPTH, n_pool, PAGE_SIZE, NSA_KV_HEADS, NSA_HEAD_DIM)
    win_shape = (DEPTH, DEC_BATCH, w_buf, NSA_KV_HEADS, NSA_HEAD_DIM)
    perm = jax.random.permutation(ks[10], n_pool)[:DEC_BATCH * n_pages]
    page_table = perm.reshape(DEC_BATCH, n_pages).astype(jnp.int32)
    return {
        'x_prompt': nrm(ks[0], (BATCH, SEQ, D_MODEL), 1.0),
        'x_sample': nrm(ks[1], (DEC_BATCH, DEC_SEQ, D_MODEL), 1.0),
        'cache_k_cmp': nrm(ks[2], pool_shape, 1.0),
        'cache_v_cmp': nrm(ks[3], pool_shape, 1.0),
        'cache_k_slc': nrm(ks[4], pool_shape, 1.0),
        'cache_v_slc': nrm(ks[5], pool_shape, 1.0),
        'state_win_k': nrm(ks[6], win_shape, 1.0),
        'state_win_v': nrm(ks[7], win_shape, 1.0),
        'state_conv': nrm(ks[8], (DEPTH, DEC_BATCH, CONV_WIDTH - 1, GDN_QKV_DIM), 1.0),
        'state_gdn': nrm(ks[9], (DEPTH, DEC_BATCH, GDN_HEADS, GDN_HEAD_DIM, GDN_HEAD_DIM), 0.5),
        'page_table': page_table,
        'norm_gain': 1.0 + nrm(ks[11], (DEPTH, D_MODEL), 0.02),
        'w_in': nrm(ks[12], (DEPTH, D_MODEL, IN_DIM), D_MODEL ** -0.5),
        'conv_w': nrm(ks[13], (DEPTH, CONV_WIDTH, GDN_QKV_DIM), CONV_WIDTH ** -0.5),
        'a_log': jnp.log(jax.random.uniform(ks[14], (DEPTH, GDN_HEADS), f32, 1.0, 16.0)),
        'dt_bias': nrm(ks[15], (DEPTH, GDN_HEADS), 0.1),
        'gdn_gain': 1.0 + nrm(ks[16], (DEPTH, GDN_HEAD_DIM), 0.02),
        'cmp_pos_k': (1.0 + nrm(ks[17], (DEPTH, CMP_LEN, NSA_HEAD_DIM), 0.1)) * CMP_LEN ** -0.5,
        'cmp_w_k': nrm(ks[18], (DEPTH, NSA_HEAD_DIM, NSA_HEAD_DIM), NSA_HEAD_DIM ** -0.5),
        'cmp_pos_v': (1.0 + nrm(ks[19], (DEPTH, CMP_LEN, NSA_HEAD_DIM), 0.1)) * CMP_LEN ** -0.5,
        'cmp_w_v': nrm(ks[20], (DEPTH, NSA_HEAD_DIM, NSA_HEAD_DIM), NSA_HEAD_DIM ** -0.5),
        'w_branch_a': nrm(ks[21], (DEPTH, GDN_DIM, D_MODEL), GDN_DIM ** -0.5),
        'w_branch_b': nrm(ks[22], (DEPTH, NSA_DIM, D_MODEL), NSA_DIM ** -0.5),
        'w_out': nrm(ks[23], (DEPTH, D_MODEL, D_MODEL), D_MODEL ** -0.5),
        'final_gain': 1.0 + nrm(ks[24], (D_MODEL,), 0.02),
    }


def reference(x_prompt, x_sample, cache_k_cmp, cache_v_cmp, cache_k_slc, cache_v_slc,
              state_win_k, state_win_v, state_conv, state_gdn, page_table,
              norm_gain, w_in, conv_w, a_log, dt_bias, gdn_gain,
              cmp_pos_k, cmp_w_k, cmp_pos_v, cmp_w_v,
              w_branch_a, w_branch_b, w_out, final_gain):
    hp, hs = x_prompt, x_sample
    bp = x_prompt.shape[0]
    prompt_states, sample_states = [], []
    for l in range(DEPTH):
        lw = (norm_gain[l], w_in[l], conv_w[l], a_log[l], dt_bias[l], gdn_gain[l],
              w_branch_a[l], w_branch_b[l], w_out[l])
        cmp_p = dict(pos_k=cmp_pos_k[l], w_k=cmp_w_k[l], pos_v=cmp_pos_v[l], w_v=cmp_w_v[l])
        nsa_p = functools.partial(nsa_prompt, **cmp_p)
        nsa_s = functools.partial(nsa_sample, pool_k_cmp=cache_k_cmp[l], pool_v_cmp=cache_v_cmp[l],
                                  pool_k_slc=cache_k_slc[l], pool_v_slc=cache_v_slc[l],
                                  buf_k_win=state_win_k[l], buf_v_win=state_win_v[l],
                                  page_table=page_table, **cmp_p)
        conv0 = jnp.zeros((bp, CONV_WIDTH - 1, GDN_QKV_DIM), hp.dtype)
        gdn0 = jnp.zeros((bp, GDN_HEADS, GDN_HEAD_DIM, GDN_HEAD_DIM), jnp.float32)
        hp, st_p = layer(hp, conv0, gdn0, nsa_p, *lw)
        hs, st_s = layer(hs, state_conv[l], state_gdn[l], nsa_s, *lw)
        prompt_states.append(st_p)
        sample_states.append(st_s)
    y_prompt = rmsnorm(hp, final_gain)
    y_sample = rmsnorm(hs, final_gain)
    (k_cmp_p, v_cmp_p, k_slc_p, v_slc_p, win_k_p, win_v_p, conv_p, gdn_p) = [jnp.stack(z) for z in zip(*prompt_states)]
    (k_cmp_s, v_cmp_s, k_slc_s, v_slc_s, win_k_s, win_v_s, conv_s, gdn_s) = [jnp.stack(z) for z in zip(*sample_states)]
    return (y_prompt, y_sample,
            k_cmp_p, v_cmp_p, k_slc_p, v_slc_p, win_k_p, win_v_p, conv_p, gdn_p,
            k_cmp_s, v_cmp_s, k_slc_s, v_slc_s, win_k_s, win_v_s, conv_s, gdn_s)
```

```python
import functools
import math

import jax
import jax.numpy as jnp
from jax import lax
from jax.experimental import pallas as pl
from jax.experimental.pallas import tpu as pltpu

F32 = jnp.float32
BF16 = jnp.bfloat16
HI = lax.Precision.HIGHEST

LANES = 128
SUBLANES = 8
VMEM_LIMIT = 56 * 1024 * 1024

D_MODEL = 1024
GDN_HEADS = 8
GDN_HEAD_DIM = 128
GDN_DIM = GDN_HEADS * GDN_HEAD_DIM
GDN_QKV_DIM = 3 * GDN_DIM
CONV_WIDTH = 4
GDN_CHUNK = 64
NSA_HEADS = 8
NSA_KV_HEADS = 2
NSA_GROUP = NSA_HEADS // NSA_KV_HEADS
NSA_HEAD_DIM = 64
NSA_DIM = NSA_HEADS * NSA_HEAD_DIM
NSA_KV_DIM = NSA_KV_HEADS * NSA_HEAD_DIM
CMP_STRIDE = 16
CMP_LEN = 2 * CMP_STRIDE
SEL_BLOCK = 64
SEL_RATIO = SEL_BLOCK // CMP_STRIDE
N_SELECT = 16
WINDOW = 512
FORCE_BONUS = 100.0
RMS_EPS = 1e-6
NEG_INF = -1e30
NSA_SCALE = NSA_HEAD_DIM ** -0.5

COL_QKV = 0
COL_ZA = 3072
COL_GATE_A = 4096
COL_GATE_B = 5120
COL_QB = 6144
COL_ZB = 6656
COL_KV = 7168
COL_SMALL = 7936
N_PACKED = 8064
SMALL_A = 0
SMALL_BT = GDN_HEADS
SMALL_GB = 2 * GDN_HEADS


def _dot(a, b, precision=None):
    return jnp.dot(a, b, precision=precision, preferred_element_type=F32)


def _dot_nt(a, b, precision=None):
    return lax.dot_general(a, b, (((1,), (1,)), ((), ())), precision=precision,
                           preferred_element_type=F32)


def _dot_tn(a, b, precision=None):
    return lax.dot_general(a, b, (((0,), (0,)), ((), ())), precision=precision,
                           preferred_element_type=F32)


def _silu(x):
    return x * jax.nn.sigmoid(x)


def _iota(shape, dim):
    return lax.broadcasted_iota(jnp.int32, shape, dim)


def _proj_kernel(x_ref, g_ref, w_ref, o_ref, h_ref):
    @pl.when(pl.program_id(1) == 0)
    def _():
        x = x_ref[...]
        y = x * lax.rsqrt(jnp.mean(x * x, axis=-1, keepdims=True) + RMS_EPS)
        h_ref[...] = (y * g_ref[...]).astype(BF16)

    o_ref[...] = jnp.dot(h_ref[...], w_ref[...], preferred_element_type=F32)


def _proj(x2d, gain, w_packed, tm, tn):
    rows, d = x2d.shape
    n = w_packed.shape[1]
    return pl.pallas_call(
        _proj_kernel,
        grid=(rows // tm, n // tn),
        in_specs=[
            pl.BlockSpec((tm, d), lambda i, j: (i, 0)),
            pl.BlockSpec((1, d), lambda i, j: (0, 0)),
            pl.BlockSpec((d, tn), lambda i, j: (0, j)),
        ],
        out_specs=pl.BlockSpec((tm, tn), lambda i, j: (i, j)),
        out_shape=jax.ShapeDtypeStruct((rows, n), F32),
        scratch_shapes=[pltpu.VMEM((tm, d), BF16)],
        compiler_params=pltpu.CompilerParams(
            dimension_semantics=("parallel", "arbitrary"), vmem_limit_bytes=VMEM_LIMIT),
    )(x2d, gain, w_packed)


def _gdn_kernel(q_ref, k_ref, v_ref, z_ref, sm_ref, cwq_ref, cwk_ref, cwv_ref,
                pq_ref, pk_ref, pv_ref, alog_ref, dtb_ref, gain_ref, s0_ref,
                o_ref, sfin_ref,
                s_sc, tail_sc, buf_sc, q_sc, k_sc, v_sc, g_sc, b_sc, *, tb, chunk, t_valid):
    h = pl.program_id(1)
    t = pl.program_id(2)

    @pl.when(t == 0)
    def _():
        s_sc[...] = s0_ref[...]
        tail_sc[0] = pq_ref[...]
        tail_sc[1] = pk_ref[...]
        tail_sc[2] = pv_ref[...]

    conv = []
    for idx, (x_ref, cw_ref) in enumerate(((q_ref, cwq_ref), (k_ref, cwk_ref), (v_ref, cwv_ref))):
        buf_sc[idx, 0:SUBLANES, :] = tail_sc[idx]
        buf_sc[idx, SUBLANES:SUBLANES + tb, :] = x_ref[...]
        tail_sc[idx] = x_ref[tb - SUBLANES:tb, :]
        cw = cw_ref[...]
        y = cw[0:1] * buf_sc[idx, SUBLANES - 3:SUBLANES - 3 + tb, :]
        for j in range(1, CONV_WIDTH):
            y = y + cw[j:j + 1] * buf_sc[idx, SUBLANES - 3 + j:SUBLANES - 3 + j + tb, :]
        conv.append(_silu(y))
    qc, kc, vc = conv
    q_sc[...] = qc * lax.rsqrt(jnp.sum(qc * qc, axis=-1, keepdims=True) + 1e-6) * (GDN_HEAD_DIM ** -0.5)
    k_sc[...] = kc * lax.rsqrt(jnp.sum(kc * kc, axis=-1, keepdims=True) + 1e-6)
    v_sc[...] = vc

    lane = _iota((1, LANES), 1)
    pick_a = lane == h + SMALL_A
    pick_b = lane == h + SMALL_BT
    sm = sm_ref[...]
    a_col = jnp.sum(jnp.where(pick_a, sm, 0.0), axis=1, keepdims=True)
    bt_col = jnp.sum(jnp.where(pick_b, sm, 0.0), axis=1, keepdims=True)
    neg_rate = -jnp.exp(jnp.sum(jnp.where(pick_a, alog_ref[...], 0.0), axis=1, keepdims=True))
    dtb = jnp.sum(jnp.where(pick_a, dtb_ref[...], 0.0), axis=1, keepdims=True)
    xx = a_col + dtb
    softplus = jnp.maximum(xx, 0.0) + jnp.log(1.0 + jnp.exp(-jnp.abs(xx)))
    row = t * tb + _iota((tb, 1), 0)
    live = row < t_valid
    g_col = jnp.where(live, neg_rate * softplus, 0.0)
    beta_col = jnp.where(live, jax.nn.sigmoid(bt_col), 0.0)
    g_sc[...] = jnp.broadcast_to(g_col, (tb, LANES))
    b_sc[...] = jnp.broadcast_to(beta_col, (tb, LANES))

    c = chunk
    ri = _iota((c, c), 0)
    ci = _iota((c, c), 1)
    causal = ri >= ci
    strict = ri > ci
    tril_f = causal.astype(F32)
    strict_f = strict.astype(F32)
    eye_f = (ri == ci).astype(F32)
    n_doubling = int(math.log2(c)) - 1
    gain = gain_ref[...]

    def chunk_body(ck, carry):
        r0 = pl.multiple_of(ck * c, c)
        qq = q_sc[pl.ds(r0, c), :]
        kk = k_sc[pl.ds(r0, c), :]
        vv = v_sc[pl.ds(r0, c), :]
        gb = g_sc[pl.ds(r0, c), :]
        bb = b_sc[pl.ds(r0, c), :]
        gc = _dot(tril_f, gb, HI)
        dlog = _dot(tril_f, gb[:, :c] * strict_f, HI)
        decay = jnp.where(causal, jnp.exp(dlog), 0.0)
        a_mat = jnp.where(strict, bb[:, :c] * _dot_nt(kk, kk, HI) * decay, 0.0)
        nk = -a_mat
        tinv = eye_f + nk
        for _ in range(n_doubling):
            nk = _dot(nk, nk, HI)
            tinv = tinv + _dot(tinv, nk, HI)
        egc = jnp.exp(gc)
        u = _dot(tinv, vv * bb, HI)
        w = _dot(tinv, kk * (bb * egc), HI)
        attn = _dot_nt(qq, kk, HI) * decay
        g_last = gc[c - 1:c, :]
        k_dec = kk * jnp.exp(g_last - gc)
        s = s_sc[...]
        v_new = u - _dot(w, s, HI)
        o = _dot(qq * egc, s, HI) + _dot(attn, v_new, HI)
        s_sc[...] = s * jnp.exp(g_last) + _dot_tn(k_dec, v_new, HI)
        on = o * lax.rsqrt(jnp.mean(o * o, axis=-1, keepdims=True) + RMS_EPS) * gain
        o_ref[pl.ds(r0, c), :] = on * _silu(z_ref[pl.ds(r0, c), :])
        return carry

    lax.fori_loop(0, tb // c, chunk_body, 0)

    @pl.when(t == pl.num_programs(2) - 1)
    def _():
        sfin_ref[...] = s_sc[...]


def _gdn(p3, conv_w, prefix, alog, dtb, gain, s0, *, tb, chunk, t_valid):
    b, t, _ = p3.shape
    hd = GDN_HEAD_DIM
    nq = GDN_DIM // hd

    def col(off):
        return lambda bi, h, ti: (bi, ti, off + h)

    def pcol(off):
        return lambda bi, h, ti: (bi, 0, off + h)

    def wcol(off):
        return lambda bi, h, ti: (0, off + h)

    const2 = lambda bi, h, ti: (0, 0)
    kern = functools.partial(_gdn_kernel, tb=tb, chunk=chunk, t_valid=t_valid)
    return pl.pallas_call(
        kern,
        grid=(b, GDN_HEADS, t // tb),
        in_specs=[
            pl.BlockSpec((None, tb, hd), col(COL_QKV // hd)),
            pl.BlockSpec((None, tb, hd), col(COL_QKV // hd + nq)),
            pl.BlockSpec((None, tb, hd), col(COL_QKV // hd + 2 * nq)),
            pl.BlockSpec((None, tb, hd), col(COL_ZA // hd)),
            pl.BlockSpec((None, tb, LANES), lambda bi, h, ti: (bi, ti, COL_SMALL // LANES)),
            pl.BlockSpec((CONV_WIDTH, hd), wcol(0)),
            pl.BlockSpec((CONV_WIDTH, hd), wcol(nq)),
            pl.BlockSpec((CONV_WIDTH, hd), wcol(2 * nq)),
            pl.BlockSpec((None, SUBLANES, hd), pcol(0)),
            pl.BlockSpec((None, SUBLANES, hd), pcol(nq)),
            pl.BlockSpec((None, SUBLANES, hd), pcol(2 * nq)),
            pl.BlockSpec((1, LANES), const2),
            pl.BlockSpec((1, LANES), const2),
            pl.BlockSpec((1, hd), const2),
            pl.BlockSpec((None, None, hd, hd), lambda bi, h, ti: (bi, h, 0, 0)),
        ],
        out_specs=[
            pl.BlockSpec((None, tb, hd), lambda bi, h, ti: (bi, ti, h)),
            pl.BlockSpec((None, None, hd, hd), lambda bi, h, ti: (bi, h, 0, 0)),
        ],
        out_shape=[
            jax.ShapeDtypeStruct((b, t, GDN_DIM), F32),
            jax.ShapeDtypeStruct((b, GDN_HEADS, hd, hd), F32),
        ],
        scratch_shapes=[
            pltpu.VMEM((hd, hd), F32),
            pltpu.VMEM((3, SUBLANES, hd), F32),
            pltpu.VMEM((3, tb + SUBLANES, hd), F32),
            pltpu.VMEM((tb, hd), F32),
            pltpu.VMEM((tb, hd), F32),
            pltpu.VMEM((tb, hd), F32),
            pltpu.VMEM((tb, LANES), F32),
            pltpu.VMEM((tb, LANES), F32),
        ],
        compiler_params=pltpu.CompilerParams(
            dimension_semantics=("parallel", "parallel", "arbitrary"), vmem_limit_bytes=VMEM_LIMIT),
    )(p3, p3, p3, p3, p3, conv_w, conv_w, conv_w, prefix, prefix, prefix, alog, dtb, gain, s0)


def _masked_softmax(s, mask):
    s = jnp.where(mask, s, NEG_INF)
    e = jnp.where(mask, jnp.exp(s - jnp.max(s, axis=-1, keepdims=True)), 0.0)
    return e / jnp.maximum(jnp.sum(e, axis=-1, keepdims=True), 1e-30)


def _stack_heads(q, g):
    hd = NSA_HEAD_DIM
    parts = [q[:, (NSA_GROUP * g + h) * hd:(NSA_GROUP * g + h + 1) * hd] for h in range(NSA_GROUP)]
    return jnp.concatenate(parts, axis=0) * NSA_SCALE


def _tile_rows(x):
    return jnp.concatenate([x] * NSA_GROUP, axis=0)


def _select_blocks(p_c, pos, tq, n_blocks, nbl):
    m = p_c.shape[1]
    imp = p_c[0:tq]
    for h in range(1, NSA_GROUP):
        imp = imp + p_c[h * tq:(h + 1) * tq]
    pool = ((_iota((m, nbl), 0) >> int(math.log2(SEL_RATIO))) == _iota((m, nbl), 1)).astype(F32)
    imp_sel = _dot(imp, pool, HI)
    lane = _iota((tq, nbl), 1)
    valid = lane * SEL_BLOCK <= pos
    cur = pos >> int(math.log2(SEL_BLOCK))
    forced = (lane == 0) | (lane == cur) | (lane == cur - 1)
    score = jnp.where(valid, imp_sel + FORCE_BONUS * forced.astype(F32), -1.0)
    score = jnp.where(lane < n_blocks, score, -2.0)
    rank = jnp.zeros((tq, nbl), F32)
    for j in range(n_blocks):
        sj = score[:, j:j + 1]
        rank = rank + ((sj > score) | ((sj == score) & (lane > j))).astype(F32)
    return (rank < float(min(N_SELECT, n_blocks))).astype(F32)


def _compress_rows(x, pw):
    n = x.shape[0] // CMP_STRIDE
    x3 = x.reshape(n, CMP_STRIDE, LANES)
    lo = jnp.sum(x3 * pw[:CMP_STRIDE][None], axis=1)
    hi = jnp.sum(x3 * pw[CMP_STRIDE:][None], axis=1)
    return lo, hi


def _combine_heads(o_ref, g, gates, zb, o_c, o_s, o_w, tq):
    hd = NSA_HEAD_DIM
    for h in range(NSA_GROUP):
        hh = NSA_GROUP * g + h
        c0 = SMALL_GB + 3 * hh
        rows = slice(h * tq, (h + 1) * tq)
        o_h = (gates[:, c0:c0 + 1] * o_c[rows] + gates[:, c0 + 1:c0 + 2] * o_s[rows]
               + gates[:, c0 + 2:c0 + 3] * o_w[rows])
        o_ref[:, hh * hd:(hh + 1) * hd] = o_h * _silu(zb[:, hh * hd:(hh + 1) * hd])


def _nsa_prompt_kernel(q_ref, zb_ref, sm_ref, kcmp_ref, vcmp_ref, kslc_ref, vslc_ref, kwin_ref, vwin_ref,
                       pk_ref, pv_ref, wk_ref, wv_ref, o_ref, kc_sc, vc_sc, *, t_len, tq, tk, span, nbl):
    i = pl.program_id(1)
    m = t_len // CMP_STRIDE
    n_blocks = t_len // SEL_BLOCK
    hd = NSA_HEAD_DIM

    @pl.when(i == 0)
    def _():
        for src, pw_ref, w_ref, dst in ((kcmp_ref, pk_ref, wk_ref, kc_sc), (vcmp_ref, pv_ref, wv_ref, vc_sc)):
            lo, hi = _compress_rows(src[...], pw_ref[...])
            hi_next = pltpu.roll(hi, m - 1, 0)
            pre = jnp.where(_iota((m, LANES), 0) < m - 1, lo + hi_next, 0.0)
            dst[...] = _dot(pre, w_ref[...], HI)

    q0 = i * tq
    pos = q0 + _iota((tq, 1), 0)
    pos4 = _tile_rows(pos)
    q = q_ref[...]
    zb = zb_ref[...]
    gates = jax.nn.sigmoid(sm_ref[...])
    c_end = _iota((1, m), 1) * CMP_STRIDE + (CMP_LEN - 1)
    n_kt = (q0 + tq + tk - 1) // tk
    w_start = pl.multiple_of(jnp.clip(q0 - WINDOW, 0, t_len - span), SUBLANES)
    w_pos = w_start + _iota((1, span), 1)
    w_dist = pos4 - w_pos
    w_mask = (w_dist >= 0) & (w_dist <= WINDOW)

    for g in range(NSA_KV_HEADS):
        gs = slice(g * hd, (g + 1) * hd)
        qs = _stack_heads(q, g)
        p_c = _masked_softmax(_dot_nt(qs, kc_sc[:, gs], HI), c_end <= pos4)
        o_c = _dot(p_c, vc_sc[:, gs])
        sel = _select_blocks(p_c, pos, tq, n_blocks, nbl).astype(BF16)

        def sel_body(kt, carry, qs=qs, sel=sel, gs=gs):
            m_run, l_run, acc = carry
            k0 = pl.multiple_of(kt * tk, tk)
            k_pos = k0 + _iota((1, tk), 1)
            expand = ((k0 + _iota((nbl, tk), 1)) >> int(math.log2(SEL_BLOCK))) == _iota((nbl, tk), 0)
            picked = _dot(sel, expand.astype(BF16)) > 0.5
            mask = _tile_rows(picked & (k_pos <= pos))
            s = jnp.where(mask, _dot_nt(qs, kslc_ref[pl.ds(k0, tk), gs]), NEG_INF)
            m_new = jnp.maximum(m_run, jnp.max(s, axis=-1, keepdims=True))
            alpha = jnp.exp(m_run - m_new)
            e = jnp.where(mask, jnp.exp(s - m_new), 0.0)
            l_new = alpha * l_run + jnp.sum(e, axis=-1, keepdims=True)
            acc_new = alpha * acc + _dot(e, vslc_ref[pl.ds(k0, tk), gs])
            return m_new, l_new, acc_new

        init = (jnp.full((NSA_GROUP * tq, 1), NEG_INF, F32), jnp.zeros((NSA_GROUP * tq, 1), F32),
                jnp.zeros((NSA_GROUP * tq, hd), F32))
        _, l_s, acc_s = lax.fori_loop(0, n_kt, sel_body, init)
        o_s = acc_s / jnp.maximum(l_s, 1e-30)

        p_w = _masked_softmax(_dot_nt(qs, kwin_ref[pl.ds(w_start, span), gs]), w_mask)
        o_w = _dot(p_w, vwin_ref[pl.ds(w_start, span), gs])
        _combine_heads(o_ref, g, gates, zb, o_c, o_s, o_w, tq)


def _nsa_prompt(p3, pos_k, pos_v, w_k, w_v, *, tq, tk):
    b, t, _ = p3.shape
    span = WINDOW + tq
    assert t % tq == 0 and t % tk == 0 and t >= span and t % SEL_BLOCK == 0
    m = t // CMP_STRIDE
    nbl = -(-(t // SEL_BLOCK) // LANES) * LANES
    kvc = COL_KV // LANES

    def full(c):
        return pl.BlockSpec((None, t, LANES), lambda bi, i: (bi, 0, c))

    const2 = lambda bi, i: (0, 0)
    kern = functools.partial(_nsa_prompt_kernel, t_len=t, tq=tq, tk=tk, span=span, nbl=nbl)
    return pl.pallas_call(
        kern,
        grid=(b, t // tq),
        in_specs=[
            pl.BlockSpec((None, tq, NSA_DIM), lambda bi, i: (bi, i, COL_QB // NSA_DIM)),
            pl.BlockSpec((None, tq, NSA_DIM), lambda bi, i: (bi, i, COL_ZB // NSA_DIM)),
            pl.BlockSpec((None, tq, LANES), lambda bi, i: (bi, i, COL_SMALL // LANES)),
            full(kvc), full(kvc + 1), full(kvc + 2), full(kvc + 3), full(kvc + 4), full(kvc + 5),
            pl.BlockSpec((CMP_LEN, LANES), const2),
            pl.BlockSpec((CMP_LEN, LANES), const2),
            pl.BlockSpec((LANES, LANES), const2),
            pl.BlockSpec((LANES, LANES), const2),
        ],
        out_specs=pl.BlockSpec((None, tq, NSA_DIM), lambda bi, i: (bi, i, 0)),
        out_shape=jax.ShapeDtypeStruct((b, t, NSA_DIM), F32),
        scratch_shapes=[pltpu.VMEM((m, LANES), F32), pltpu.VMEM((m, LANES), F32)],
        compiler_params=pltpu.CompilerParams(
            dimension_semantics=("parallel", "arbitrary"), vmem_limit_bytes=VMEM_LIMIT),
    )(p3, p3, p3, p3, p3, p3, p3, p3, p3, pos_k, pos_v, w_k, w_v)


def _cmp_paged_kernel(pt_ref, *refs, k_pages):
    del pt_ref
    kp = refs[:k_pages]
    kn = refs[k_pages]
    vp = refs[k_pages + 1:2 * k_pages + 1]
    vn = refs[2 * k_pages + 1]
    pk_ref, pv_ref, wk_ref, wv_ref, kc_ref, vc_ref = refs[2 * k_pages + 2:]
    n = kc_ref.shape[0]
    last_row = _iota((n, LANES), 0) == n - 1
    for pages, nxt, pw_ref, w_ref, out in ((kp, kn, pk_ref, wk_ref, kc_ref), (vp, vn, pv_ref, wv_ref, vc_ref)):
        pw = pw_ref[...]
        parts = [_compress_rows(r[...], pw) for r in pages]
        lo = jnp.concatenate([p[0] for p in parts], axis=0)
        hi = jnp.concatenate([p[1] for p in parts], axis=0)
        hi_head = jnp.sum(nxt[...] * pw[CMP_STRIDE:], axis=0, keepdims=True)
        hi_next = jnp.where(last_row, hi_head, pltpu.roll(hi, n - 1, 0))
        out[...] = _dot(lo + hi_next, w_ref[...], HI)


def _cmp_paged(pool_k, pool_v, page_table, pos_k, pos_v, w_k, w_v, *, k_pages):
    b, n_pages = page_table.shape
    page = pool_k.shape[1]
    assert page == LANES and n_pages % k_pages == 0
    per_step = k_pages * page // CMP_STRIDE

    def page_spec(i):
        return pl.BlockSpec((None, page, LANES), lambda bi, j, pt: (pt[bi, j * k_pages + i], 0, 0))

    next_spec = pl.BlockSpec(
        (None, CMP_STRIDE, LANES),
        lambda bi, j, pt: (pt[bi, jnp.minimum((j + 1) * k_pages, n_pages - 1)], 0, 0))
    const2 = lambda bi, j, pt: (0, 0)
    in_specs = ([page_spec(i) for i in range(k_pages)] + [next_spec]
                + [page_spec(i) for i in range(k_pages)] + [next_spec]
                + [pl.BlockSpec((CMP_LEN, LANES), const2)] * 2 + [pl.BlockSpec((LANES, LANES), const2)] * 2)
    out_spec = pl.BlockSpec((None, per_step, LANES), lambda bi, j, pt: (bi, j, 0))
    m = n_pages * page // CMP_STRIDE
    return pl.pallas_call(
        functools.partial(_cmp_paged_kernel, k_pages=k_pages),
        grid_spec=pltpu.PrefetchScalarGridSpec(
            num_scalar_prefetch=1, grid=(b, n_pages // k_pages), in_specs=in_specs,
            out_specs=[out_spec, out_spec]),
        out_shape=[jax.ShapeDtypeStruct((b, m, LANES), F32)] * 2,
        compiler_params=pltpu.CompilerParams(
            dimension_semantics=("parallel", "arbitrary"), vmem_limit_bytes=VMEM_LIMIT),
    )(page_table, *([pool_k] * (k_pages + 1)), *([pool_v] * (k_pages + 1)), pos_k, pos_v, w_k, w_v)


def _new_key_scores(qs, new_k, r, gs):
    return jnp.sum(qs * new_k[r:r + 1, gs], axis=-1, keepdims=True)


def _nsa_sample_a_kernel(q_ref, kc_ref, vc_ref, bk_ref, bv_ref, nk_ref, nv_ref,
                         oc_ref, ow_ref, sel_ref, wk_out, wv_out, *, past, t_valid, tp, n_blocks, nbl):
    hd = NSA_HEAD_DIM
    m = kc_ref.shape[0]
    w_buf = bk_ref.shape[0]
    pos = past + _iota((tp, 1), 0)
    pos4 = _tile_rows(pos)
    q = q_ref[...]
    new_k = nk_ref[...]
    new_v = nv_ref[...]
    c_end = _iota((1, m), 1) * CMP_STRIDE + (CMP_LEN - 1)
    buf_pos = (past - w_buf) + _iota((1, w_buf), 1)
    buf_dist = pos4 - buf_pos
    buf_mask = (buf_dist >= 0) & (buf_dist <= WINDOW)
    for g in range(NSA_KV_HEADS):
        gs = slice(g * hd, (g + 1) * hd)
        qs = _stack_heads(q, g)
        p_c = _masked_softmax(_dot_nt(qs, kc_ref[:, gs], HI), c_end <= pos4)
        oc_ref[g] = _dot(p_c, vc_ref[:, gs])
        sel_ref[g] = _select_blocks(p_c, pos, tp, n_blocks, nbl)
        s_buf = jnp.where(buf_mask, _dot_nt(qs, bk_ref[:, gs]), NEG_INF)
        s_new, new_mask = [], []
        for r in range(t_valid):
            dist = pos4 - (past + r)
            mk = (dist >= 0) & (dist <= WINDOW)
            new_mask.append(mk)
            s_new.append(jnp.where(mk, _new_key_scores(qs, new_k, r, gs), NEG_INF))
        m_all = jnp.max(s_buf, axis=-1, keepdims=True)
        for sn in s_new:
            m_all = jnp.maximum(m_all, sn)
        e_buf = jnp.where(buf_mask, jnp.exp(s_buf - m_all), 0.0)
        l_all = jnp.sum(e_buf, axis=-1, keepdims=True)
        acc = _dot(e_buf, bv_ref[:, gs])
        for r in range(t_valid):
            e_r = jnp.where(new_mask[r], jnp.exp(s_new[r] - m_all), 0.0)
            l_all = l_all + e_r
            acc = acc + e_r * new_v[r:r + 1, gs]
        ow_ref[g] = acc / jnp.maximum(l_all, 1e-30)
    wk_out[0:w_buf - t_valid, :] = bk_ref[t_valid:w_buf, :]
    wk_out[w_buf - t_valid:w_buf, :] = new_k[0:t_valid]
    wv_out[0:w_buf - t_valid, :] = bv_ref[t_valid:w_buf, :]
    wv_out[w_buf - t_valid:w_buf, :] = new_v[0:t_valid]


def _nsa_sample_a(p3, kc, vc, buf_k, buf_v, *, past, t_valid, n_blocks, nbl):
    b, tp, _ = p3.shape
    m = kc.shape[1]
    w_buf = buf_k.shape[1]
    assert w_buf == WINDOW and t_valid <= w_buf
    kvc = COL_KV // LANES
    rows = NSA_GROUP * tp
    kern = functools.partial(_nsa_sample_a_kernel, past=past, t_valid=t_valid, tp=tp,
                             n_blocks=n_blocks, nbl=nbl)
    per_b3 = lambda bi: (bi, 0, 0)
    per_b4 = lambda bi: (bi, 0, 0, 0)
    return pl.pallas_call(
        kern,
        grid=(b,),
        in_specs=[
            pl.BlockSpec((None, tp, NSA_DIM), lambda bi: (bi, 0, COL_QB // NSA_DIM)),
            pl.BlockSpec((None, m, LANES), per_b3),
            pl.BlockSpec((None, m, LANES), per_b3),
            pl.BlockSpec((None, w_buf, LANES), per_b3),
            pl.BlockSpec((None, w_buf, LANES), per_b3),
            pl.BlockSpec((None, tp, LANES), lambda bi: (bi, 0, kvc + 4)),
            pl.BlockSpec((None, tp, LANES), lambda bi: (bi, 0, kvc + 5)),
        ],
        out_specs=[
            pl.BlockSpec((None, NSA_KV_HEADS, rows, NSA_HEAD_DIM), per_b4),
            pl.BlockSpec((None, NSA_KV_HEADS, rows, NSA_HEAD_DIM), per_b4),
            pl.BlockSpec((None, NSA_KV_HEADS, tp, nbl), per_b4),
            pl.BlockSpec((None, w_buf, LANES), per_b3),
            pl.BlockSpec((None, w_buf, LANES), per_b3),
        ],
        out_shape=[
            jax.ShapeDtypeStruct((b, NSA_KV_HEADS, rows, NSA_HEAD_DIM), F32),
            jax.ShapeDtypeStruct((b, NSA_KV_HEADS, rows, NSA_HEAD_DIM), F32),
            jax.ShapeDtypeStruct((b, NSA_KV_HEADS, tp, nbl), F32),
            jax.ShapeDtypeStruct((b, w_buf, LANES), F32),
            jax.ShapeDtypeStruct((b, w_buf, LANES), F32),
        ],
        compiler_params=pltpu.CompilerParams(
            dimension_semantics=("parallel",), vmem_limit_bytes=VMEM_LIMIT),
    )(p3, kc, vc, buf_k, buf_v, p3, p3)


def _nsa_sample_b_kernel(pt_ref, q_ref, zb_ref, sm_ref, nk_ref, nv_ref, sel_ref, oc_ref, ow_ref, *refs,
                         k_pages, past, t_valid, tp, n_blocks, nbl):
    del pt_ref
    kp = refs[:k_pages]
    vp = refs[k_pages:2 * k_pages]
    o_ref, m_sc, l_sc, acc_sc = refs[2 * k_pages:]
    hd = NSA_HEAD_DIM
    j = pl.program_id(1)
    rows = NSA_GROUP * tp

    @pl.when(j == 0)
    def _():
        m_sc[...] = jnp.full(m_sc.shape, NEG_INF, F32)
        l_sc[...] = jnp.zeros(l_sc.shape, F32)
        acc_sc[...] = jnp.zeros(acc_sc.shape, F32)

    q = q_ref[...]
    k_all = jnp.concatenate([r[...] for r in kp], axis=0)
    v_all = jnp.concatenate([r[...] for r in vp], axis=0)
    n_keys = k_all.shape[0]
    blk0 = j * (n_keys // SEL_BLOCK)
    expand = (blk0 + (_iota((nbl, n_keys), 1) >> int(math.log2(SEL_BLOCK)))) == _iota((nbl, n_keys), 0)
    expand = expand.astype(BF16)
    for g in range(NSA_KV_HEADS):
        gs = slice(g * hd, (g + 1) * hd)
        qs = _stack_heads(q, g)
        mask = _tile_rows(_dot(sel_ref[g].astype(BF16), expand) > 0.5)
        s = jnp.where(mask, _dot_nt(qs, k_all[:, gs]), NEG_INF)
        m_run = m_sc[g]
        m_new = jnp.maximum(m_run, jnp.max(s, axis=-1, keepdims=True))
        alpha = jnp.exp(m_run - m_new)
        e = jnp.where(mask, jnp.exp(s - m_new), 0.0)
        l_sc[g] = alpha * l_sc[g] + jnp.sum(e, axis=-1, keepdims=True)
        acc_sc[g] = alpha * acc_sc[g] + _dot(e, v_all[:, gs])
        m_sc[g] = m_new

    @pl.when(j == pl.num_programs(1) - 1)
    def _():
        pos4 = _tile_rows(past + _iota((tp, 1), 0))
        gates = jax.nn.sigmoid(sm_ref[...])
        zb = zb_ref[...]
        new_k = nk_ref[...]
        new_v = nv_ref[...]
        for g in range(NSA_KV_HEADS):
            gs = slice(g * hd, (g + 1) * hd)
            qs = _stack_heads(q, g)
            picked = _tile_rows(sel_ref[g][:, n_blocks - 1:n_blocks] > 0.5)
            m_run, l_run, acc = m_sc[g], l_sc[g], acc_sc[g]
            s_new, new_mask = [], []
            for r in range(t_valid):
                mk = picked & (past + r <= pos4)
                new_mask.append(mk)
                s_new.append(jnp.where(mk, _new_key_scores(qs, new_k, r, gs), NEG_INF))
            m_all = m_run
            for sn in s_new:
                m_all = jnp.maximum(m_all, sn)
            alpha = jnp.exp(m_run - m_all)
            l_all = alpha * l_run
            acc = alpha * acc
            for r in range(t_valid):
                e_r = jnp.where(new_mask[r], jnp.exp(s_new[r] - m_all), 0.0)
                l_all = l_all + e_r
                acc = acc + e_r * new_v[r:r + 1, gs]
            o_s = acc / jnp.maximum(l_all, 1e-30)
            _combine_heads(o_ref, g, gates, zb, oc_ref[g], o_s, ow_ref[g], tp)


def _nsa_sample_b(p3, sel, o_c, o_w, pool_k, pool_v, page_table, *, k_pages, past, t_valid, n_blocks, nbl):
    b, tp, _ = p3.shape
    n_pages = page_table.shape[1]
    page = pool_k.shape[1]
    assert n_pages % k_pages == 0 and page % SEL_BLOCK == 0 and past == n_pages * page
    assert n_blocks == past // SEL_BLOCK + 1 and t_valid <= SEL_BLOCK
    kvc = COL_KV // LANES
    rows = NSA_GROUP * tp

    def page_spec(i):
        return pl.BlockSpec((None, page, LANES), lambda bi, j, pt: (pt[bi, j * k_pages + i], 0, 0))

    per_b4 = lambda bi, j, pt: (bi, 0, 0, 0)
    in_specs = [
        pl.BlockSpec((None, tp, NSA_DIM), lambda bi, j, pt: (bi, 0, COL_QB // NSA_DIM)),
        pl.BlockSpec((None, tp, NSA_DIM), lambda bi, j, pt: (bi, 0, COL_ZB // NSA_DIM)),
        pl.BlockSpec((None, tp, LANES), lambda bi, j, pt: (bi, 0, COL_SMALL // LANES)),
        pl.BlockSpec((None, tp, LANES), lambda bi, j, pt: (bi, 0, kvc + 2)),
        pl.BlockSpec((None, tp, LANES), lambda bi, j, pt: (bi, 0, kvc + 3)),
        pl.BlockSpec((None, NSA_KV_HEADS, tp, nbl), per_b4),
        pl.BlockSpec((None, NSA_KV_HEADS, rows, NSA_HEAD_DIM), per_b4),
        pl.BlockSpec((None, NSA_KV_HEADS, rows, NSA_HEAD_DIM), per_b4),
    ] + [page_spec(i) for i in range(k_pages)] * 2
    kern = functools.partial(_nsa_sample_b_kernel, k_pages=k_pages, past=past, t_valid=t_valid, tp=tp,
                             n_blocks=n_blocks, nbl=nbl)
    return pl.pallas_call(
        kern,
        grid_spec=pltpu.PrefetchScalarGridSpec(
            num_scalar_prefetch=1, grid=(b, n_pages // k_pages), in_specs=in_specs,
            out_specs=pl.BlockSpec((None, tp, NSA_DIM), lambda bi, j, pt: (bi, 0, 0)),
            scratch_shapes=[pltpu.VMEM((NSA_KV_HEADS, rows, 1), F32), pltpu.VMEM((NSA_KV_HEADS, rows, 1), F32),
                            pltpu.VMEM((NSA_KV_HEADS, rows, NSA_HEAD_DIM), F32)]),
        out_shape=jax.ShapeDtypeStruct((b, tp, NSA_DIM), F32),
        compiler_params=pltpu.CompilerParams(
            dimension_semantics=("parallel", "arbitrary"), vmem_limit_bytes=VMEM_LIMIT),
    )(page_table, p3, p3, p3, p3, p3, sel, o_c, o_w, *([pool_k] * k_pages), *([pool_v] * k_pages))


def _merge_kernel(x_ref, oa_ref, ob_ref, ga_ref, gb_ref, wa_ref, wb_ref, wo_ref, fg_ref, y_ref):
    mixed = (jax.nn.sigmoid(ga_ref[...]) * _dot(oa_ref[...].astype(BF16), wa_ref[...])
             + jax.nn.sigmoid(gb_ref[...]) * _dot(ob_ref[...].astype(BF16), wb_ref[...]))
    x_new = x_ref[...] + _dot(mixed.astype(BF16), wo_ref[...])
    y = x_new * lax.rsqrt(jnp.mean(x_new * x_new, axis=-1, keepdims=True) + RMS_EPS)
    y_ref[...] = y * fg_ref[...]


def _merge(x2d, o_a, o_b, p2d, w_a, w_b, w_o, final_gain, tm):
    rows, d = x2d.shape
    const2 = lambda i: (0, 0)
    return pl.pallas_call(
        _merge_kernel,
        grid=(rows // tm,),
        in_specs=[
            pl.BlockSpec((tm, d), lambda i: (i, 0)),
            pl.BlockSpec((tm, GDN_DIM), lambda i: (i, 0)),
            pl.BlockSpec((tm, NSA_DIM), lambda i: (i, 0)),
            pl.BlockSpec((tm, d), lambda i: (i, COL_GATE_A // D_MODEL)),
            pl.BlockSpec((tm, d), lambda i: (i, COL_GATE_B // D_MODEL)),
            pl.BlockSpec((GDN_DIM, d), const2),
            pl.BlockSpec((NSA_DIM, d), const2),
            pl.BlockSpec((d, d), const2),
            pl.BlockSpec((1, d), const2),
        ],
        out_specs=pl.BlockSpec((tm, d), lambda i: (i, 0)),
        out_shape=jax.ShapeDtypeStruct((rows, d), F32),
        compiler_params=pltpu.CompilerParams(
            dimension_semantics=("parallel",), vmem_limit_bytes=VMEM_LIMIT),
    )(x2d, o_a, o_b, p2d, p2d, w_a, w_b, w_o, final_gain)


def _pack_w_in(w_in):
    bounds = [0]
    for width in (GDN_QKV_DIM, GDN_HEADS, GDN_HEADS, GDN_DIM, NSA_DIM, 6 * NSA_KV_DIM, 3 * NSA_HEADS,
                  NSA_DIM, 2 * D_MODEL):
        bounds.append(bounds[-1] + width)
    qkv, a, bt, z_a, q_b, kv_b, g_b, z_b, merge = (w_in[:, bounds[i]:bounds[i + 1]] for i in range(9))
    small = jnp.concatenate([a, bt, g_b], axis=1)
    small = jnp.pad(small, ((0, 0), (0, LANES - small.shape[1])))
    packed = jnp.concatenate([qkv, z_a, merge, q_b, z_b, kv_b, small], axis=1)
    assert packed.shape[1] == N_PACKED
    return packed.astype(BF16)


def _lane_vec(v):
    return jnp.pad(v.astype(F32), (0, LANES - v.shape[0])).reshape(1, LANES)


def _block_diag2(w):
    z = jnp.zeros_like(w)
    return jnp.concatenate([jnp.concatenate([w, z], axis=1), jnp.concatenate([z, w], axis=1)], axis=0)


def _tile_for(n, candidates):
    for c in candidates:
        if n % c == 0:
            return c
    raise ValueError(f"no tile for {n}")


def kernel(x_prompt, x_sample, cache_k_cmp, cache_v_cmp, cache_k_slc, cache_v_slc, state_win_k, state_win_v,
           state_conv, state_gdn, page_table, norm_gain, w_in, conv_w, a_log, dt_bias, gdn_gain, cmp_pos_k,
           cmp_w_k, cmp_pos_v, cmp_w_v, w_branch_a, w_branch_b, w_out, final_gain):
    assert w_in.shape[0] == 1, "one layer"
    bp, tp_len, d = x_prompt.shape
    bs, ts, _ = x_sample.shape
    n_pool, page = cache_k_cmp.shape[1:3]
    n_pages = page_table.shape[1]
    past = n_pages * page
    assert past % CMP_STRIDE == 0 and ts < CMP_STRIDE and ts >= CONV_WIDTH - 1 and ts <= SUBLANES
    t_pad = SUBLANES

    w_packed = _pack_w_in(w_in[0])
    gain_in = norm_gain[0].reshape(1, d)
    fgain = final_gain.reshape(1, d)
    w_a = w_branch_a[0].astype(BF16)
    w_b = w_branch_b[0].astype(BF16)
    w_o = w_out[0].astype(BF16)
    cw = conv_w[0]
    alog = _lane_vec(a_log[0])
    dtb = _lane_vec(dt_bias[0])
    ggain = gdn_gain[0].reshape(1, GDN_HEAD_DIM)
    pos_k = jnp.tile(cmp_pos_k[0], (1, NSA_KV_HEADS))
    pos_v = jnp.tile(cmp_pos_v[0], (1, NSA_KV_HEADS))
    wk2 = _block_diag2(cmp_w_k[0])
    wv2 = _block_diag2(cmp_w_v[0])
    tn = _tile_for(N_PACKED, (1152, 896, 128))
    kvc = COL_KV

    rows_p = bp * tp_len
    x2p = x_prompt.reshape(rows_p, d)
    p2 = _proj(x2p, gain_in, w_packed, _tile_for(rows_p, (1024, 512, 256, 128)), tn)
    p3 = p2.reshape(bp, tp_len, N_PACKED)
    tb = _tile_for(tp_len, (512, 256, 128, 64))
    o_a, gdn_p = _gdn(p3, cw, jnp.zeros((bp, SUBLANES, GDN_QKV_DIM), F32), alog, dtb, ggain,
                      jnp.zeros((bp, GDN_HEADS, GDN_HEAD_DIM, GDN_HEAD_DIM), F32),
                      tb=tb, chunk=GDN_CHUNK, t_valid=tp_len)
    o_b = _nsa_prompt(p3, pos_k, pos_v, wk2, wv2, tq=128, tk=_tile_for(tp_len, (512, 256, 128)))
    y_p = _merge(x2p, o_a.reshape(rows_p, GDN_DIM), o_b.reshape(rows_p, NSA_DIM), p2, w_a, w_b, w_o, fgain,
                 _tile_for(rows_p, (512, 256, 128)))
    y_prompt = y_p.reshape(bp, tp_len, d)

    def kv_out(p, i, t_keep_from, t_to):
        return p[:, t_keep_from:t_to, kvc + i * LANES:kvc + (i + 1) * LANES].reshape(
            1, p.shape[0], t_to - t_keep_from, NSA_KV_HEADS, NSA_HEAD_DIM)

    keep_p = min(WINDOW, tp_len)
    prompt_state = ([kv_out(p3, i, 0, tp_len) for i in range(4)]
                    + [kv_out(p3, i, tp_len - keep_p, tp_len) for i in (4, 5)]
                    + [p3[:, tp_len - (CONV_WIDTH - 1):, :GDN_QKV_DIM][None], gdn_p[None]])

    xs_pad = jnp.pad(x_sample, ((0, 0), (0, t_pad - ts), (0, 0)))
    rows_s = bs * t_pad
    x2s = xs_pad.reshape(rows_s, d)
    ps2 = _proj(x2s, gain_in, w_packed, _tile_for(rows_s, (512, 256, 128, 8)), tn)
    ps3 = ps2.reshape(bs, t_pad, N_PACKED)
    prefix_s = jnp.pad(state_conv[0], ((0, 0), (SUBLANES - (CONV_WIDTH - 1), 0), (0, 0)))
    o_a_s, gdn_s = _gdn(ps3, cw, prefix_s, alog, dtb, ggain, state_gdn[0], tb=t_pad, chunk=t_pad, t_valid=ts)

    pools = [c[0].reshape(n_pool, page, NSA_KV_DIM) for c in (cache_k_cmp, cache_v_cmp, cache_k_slc, cache_v_slc)]
    k_pages = _tile_for(n_pages, (8, 4, 2, 1))
    kc_s, vc_s = _cmp_paged(pools[0], pools[1], page_table, pos_k, pos_v, wk2, wv2, k_pages=k_pages)
    n_blocks = -(-(past + ts) // SEL_BLOCK)
    nbl = -(-n_blocks // LANES) * LANES
    w_buf = state_win_k.shape[2]
    o_c_s, o_w_s, sel_s, win_k_s, win_v_s = _nsa_sample_a(
        ps3, kc_s, vc_s, state_win_k[0].reshape(bs, w_buf, NSA_KV_DIM), state_win_v[0].reshape(bs, w_buf, NSA_KV_DIM),
        past=past, t_valid=ts, n_blocks=n_blocks, nbl=nbl)
    o_b_s = _nsa_sample_b(ps3, sel_s, o_c_s, o_w_s, pools[2], pools[3], page_table, k_pages=k_pages,
                          past=past, t_valid=ts, n_blocks=n_blocks, nbl=nbl)
    y_s = _merge(x2s, o_a_s.reshape(rows_s, GDN_DIM), o_b_s.reshape(rows_s, NSA_DIM), ps2, w_a, w_b, w_o, fgain,
                 _tile_for(rows_s, (512, 256, 128, 8)))
    y_sample = y_s.reshape(bs, t_pad, d)[:, :ts]

    keep_s = min(WINDOW, w_buf + ts)
    assert keep_s == w_buf
    sample_state = ([kv_out(ps3, i, 0, ts) for i in range(4)]
                    + [win_k_s.reshape(1, bs, keep_s, NSA_KV_HEADS, NSA_HEAD_DIM),
                       win_v_s.reshape(1, bs, keep_s, NSA_KV_HEADS, NSA_HEAD_DIM),
                       ps3[:, ts - (CONV_WIDTH - 1):ts, :GDN_QKV_DIM][None], gdn_s[None]])
    return (y_prompt, y_sample, *prompt_state, *sample_state)
```

```python
import functools
import math

import jax
import jax.numpy as jnp
from jax import lax
from jax.experimental import pallas as pl
from jax.experimental.pallas import tpu as pltpu

F32 = jnp.float32
BF16 = jnp.bfloat16
HI = lax.Precision.HIGHEST

LANES = 128
SUBLANES = 8
VMEM_LIMIT = 56 * 1024 * 1024

D_MODEL = 1024
GDN_HEADS = 8
GDN_HEAD_DIM = 128
GDN_DIM = GDN_HEADS * GDN_HEAD_DIM
GDN_QKV_DIM = 3 * GDN_DIM
CONV_WIDTH = 4
GDN_CHUNK = 64
NSA_HEADS = 8
NSA_KV_HEADS = 2
NSA_GROUP = NSA_HEADS // NSA_KV_HEADS
NSA_HEAD_DIM = 64
NSA_DIM = NSA_HEADS * NSA_HEAD_DIM
NSA_KV_DIM = NSA_KV_HEADS * NSA_HEAD_DIM
CMP_STRIDE = 16
CMP_LEN = 2 * CMP_STRIDE
SEL_BLOCK = 64
SEL_SHIFT = 6
SEL_RATIO = SEL_BLOCK // CMP_STRIDE
N_SELECT = 16
WINDOW = 512
FORCE_BONUS = 100.0
RMS_EPS = 1e-6
NEG_INF = -1e30
NSA_SCALE = NSA_HEAD_DIM ** -0.5
LOG2E = math.log2(math.e)

COL_QKV = 0
COL_ZA = 3072
COL_GATE_A = 4096
COL_GATE_B = 5120
COL_QB = 6144
COL_ZB = 6656
COL_KV = 7168
COL_SMALL = 7936
N_PACKED = 8064
SMALL_A = 0
SMALL_BT = GDN_HEADS
SMALL_GB = 2 * GDN_HEADS


def _dot(a, b, precision=None):
    return jnp.dot(a, b, precision=precision, preferred_element_type=F32)


def _dot_nt(a, b, precision=None):
    return lax.dot_general(a, b, (((1,), (1,)), ((), ())), precision=precision,
                           preferred_element_type=F32)


def _dot_tn(a, b, precision=None):
    return lax.dot_general(a, b, (((0,), (0,)), ((), ())), precision=precision,
                           preferred_element_type=F32)


def _mm(a, b):
    return jnp.dot(a.astype(BF16), b.astype(BF16), preferred_element_type=F32)


def _silu(x):
    return x * jax.nn.sigmoid(x)


def _iota(shape, dim):
    return lax.broadcasted_iota(jnp.int32, shape, dim)


def _proj_kernel(x_ref, g_ref, w_ref, o_ref, h_ref):
    @pl.when(pl.program_id(1) == 0)
    def _():
        x = x_ref[...]
        y = x * lax.rsqrt(jnp.mean(x * x, axis=-1, keepdims=True) + RMS_EPS)
        h_ref[...] = (y * g_ref[...]).astype(BF16)

    o_ref[...] = jnp.dot(h_ref[...], w_ref[...], preferred_element_type=F32)


def _proj(x2d, gain, w_packed, tm, tn):
    rows, d = x2d.shape
    n = w_packed.shape[1]
    return pl.pallas_call(
        _proj_kernel,
        grid=(rows // tm, n // tn),
        in_specs=[
            pl.BlockSpec((tm, d), lambda i, j: (i, 0)),
            pl.BlockSpec((1, d), lambda i, j: (0, 0)),
            pl.BlockSpec((d, tn), lambda i, j: (0, j)),
        ],
        out_specs=pl.BlockSpec((tm, tn), lambda i, j: (i, j)),
        out_shape=jax.ShapeDtypeStruct((rows, n), F32),
        scratch_shapes=[pltpu.VMEM((tm, d), BF16)],
        compiler_params=pltpu.CompilerParams(
            dimension_semantics=("parallel", "arbitrary"), vmem_limit_bytes=VMEM_LIMIT),
    )(x2d, gain, w_packed)


def _gdn_kernel(q_ref, k_ref, v_ref, z_ref, sm_ref, cw_ref, pre_ref, alog_ref, dtb_ref, gain_ref, s0_ref,
                o_ref, sfin_ref,
                s_sc, tail_sc, q_sc, k_sc, v_sc, gc_sc, b_sc, gct_sc, *, tb, chunk, t_valid):
    t = pl.program_id(1)
    c = chunk
    hd = GDN_HEAD_DIM
    nh = GDN_HEADS

    @pl.when(t == 0)
    def _():
        s_sc[...] = s0_ref[...]
        tail_sc[...] = pre_ref[...]

    for idx, (x_ref, dst) in enumerate(((q_ref, q_sc), (k_ref, k_sc), (v_ref, v_sc))):
        cols = slice(idx * GDN_DIM, (idx + 1) * GDN_DIM)
        x = x_ref[...]
        cw = cw_ref[:, cols]
        last = CONV_WIDTH - 1
        head = jnp.concatenate([tail_sc[:, cols], x[0:SUBLANES]], axis=0)
        tail_sc[:, cols] = x[tb - SUBLANES:tb]
        y = cw[last:last + 1] * head[SUBLANES:2 * SUBLANES]
        for j in range(1, CONV_WIDTH):
            y = y + cw[last - j:last - j + 1] * head[SUBLANES - j:2 * SUBLANES - j]
        if tb > SUBLANES:
            rest = cw[last:last + 1] * x
            for j in range(1, CONV_WIDTH):
                rest = rest + cw[last - j:last - j + 1] * pltpu.roll(x, j, 0)
            y = jnp.concatenate([y, rest[SUBLANES:]], axis=0)
        y = _silu(y)
        if idx == 2:
            dst[...] = y
        else:
            scale = GDN_HEAD_DIM ** -0.5 if idx == 0 else 1.0
            for h in range(nh):
                yh = y[:, h * hd:(h + 1) * hd]
                dst[:, h * hd:(h + 1) * hd] = yh * (lax.rsqrt(jnp.sum(yh * yh, axis=-1, keepdims=True) + 1e-6) * scale)

    sm = sm_ref[...]
    xx = sm + dtb_ref[...]
    softplus = jnp.maximum(xx, 0.0) + jnp.log(1.0 + jnp.exp(-jnp.abs(xx)))
    row = _iota((tb, 1), 0)
    live = t * tb + row < t_valid
    g_small = jnp.where(live, -jnp.exp(alog_ref[...]) * softplus, 0.0)
    beta_small = jnp.where(live, jax.nn.sigmoid(sm), 0.0)
    row_in_chunk = row & (c - 1)
    gc_small = g_small
    shift = 1
    while shift < c:
        gc_small = gc_small + jnp.where(row_in_chunk >= shift, pltpu.roll(gc_small, shift, 0), 0.0)
        shift *= 2
    li = _iota((LANES, GDN_DIM), 0)
    head_of_lane = _iota((LANES, GDN_DIM), 1) >> int(math.log2(hd))

    def spread(x, first_lane):
        onehot = (li == head_of_lane + first_lane).astype(BF16)
        hi = x.astype(BF16)
        r1 = x - hi.astype(F32)
        mid = r1.astype(BF16)
        lo = (r1 - mid.astype(F32)).astype(BF16)
        return _dot(hi, onehot) + _dot(mid, onehot) + _dot(lo, onehot)

    gc_sc[...] = spread(gc_small, SMALL_A)
    b_sc[...] = spread(beta_small, SMALL_BT)
    for r0 in range(0, tb, LANES):
        n = min(LANES, tb - r0)
        part = gc_small[r0:r0 + n]
        if n < LANES:
            part = jnp.concatenate([part, jnp.zeros((LANES - n, LANES), F32)], axis=0)
        part_t = part.T
        for off in range(0, n, c):
            gct_sc[(r0 + off) // c] = part_t[:, off:off + c]

    ri = _iota((c, c), 0)
    ci = _iota((c, c), 1)
    causal = ri >= ci
    strict = ri > ci
    eye_f = (ri == ci).astype(F32)
    n_doubling = int(math.log2(c)) - 1
    gain = gain_ref[...]

    def chunk_body(ck, carry):
        r0 = pl.multiple_of(ck * c, c)
        rows = pl.ds(r0, c)
        heads = range(nh)
        hs = [slice(h * hd, (h + 1) * hd) for h in heads]
        kq = []
        for h in heads:
            kk = k_sc[rows, hs[h]].astype(BF16)
            kq.append(_dot_nt(jnp.concatenate([kk, q_sc[rows, hs[h]].astype(BF16)], axis=0), kk))
        decay = [jnp.where(causal, jnp.exp(gc_sc[rows, hs[h]][:, :c] - gct_sc[ck, h:h + 1, :]), 0.0) for h in heads]
        attn = [kq[h][c:] * decay[h] for h in heads]
        nk = [jnp.where(strict, -(b_sc[rows, hs[h]][:, :c] * kq[h][:c] * decay[h]), 0.0) for h in heads]
        tinv = [eye_f + nk[h] for h in heads]
        for _ in range(n_doubling):
            nk = [_mm(nk[h], nk[h]) for h in heads]
            tinv = [tinv[h] + _mm(tinv[h], nk[h]) for h in heads]
        uw = []
        for h in heads:
            bb = b_sc[rows, hs[h]]
            rhs = jnp.concatenate([v_sc[rows, hs[h]] * bb,
                                   k_sc[rows, hs[h]] * (bb * jnp.exp(gc_sc[rows, hs[h]]))], axis=1)
            uw.append(_mm(tinv[h], rhs))
        ws_qs = []
        for h in heads:
            q_dec = q_sc[rows, hs[h]] * jnp.exp(gc_sc[rows, hs[h]])
            ws_qs.append(_mm(jnp.concatenate([uw[h][:, hd:], q_dec], axis=0), s_sc[h]))
        v_new = [uw[h][:, :hd] - ws_qs[h][:c] for h in heads]
        o = [ws_qs[h][c:] + _mm(attn[h], v_new[h]) for h in heads]
        new_states = []
        for h in heads:
            gc = gc_sc[rows, hs[h]]
            g_last = gc[c - 1:c, :]
            k_dec = k_sc[rows, hs[h]] * jnp.exp(g_last - gc)
            new_states.append(s_sc[h] * jnp.exp(g_last) + _dot_tn(k_dec, v_new[h]))
        for h in heads:
            s_sc[h] = new_states[h]
            on = o[h] * lax.rsqrt(jnp.mean(o[h] * o[h], axis=-1, keepdims=True) + RMS_EPS) * gain
            o_ref[rows, hs[h]] = on * _silu(z_ref[rows, hs[h]])
        return carry

    lax.fori_loop(0, tb // c, chunk_body, 0)

    @pl.when(t == pl.num_programs(1) - 1)
    def _():
        sfin_ref[...] = s_sc[...]


def _gdn(p3, conv_w, prefix, alog, dtb, gain, s0, *, tb, chunk, t_valid):
    b, t, _ = p3.shape
    hd = GDN_HEAD_DIM
    gd = GDN_DIM
    const2 = lambda bi, ti: (0, 0)
    kern = functools.partial(_gdn_kernel, tb=tb, chunk=chunk, t_valid=t_valid)
    return pl.pallas_call(
        kern,
        grid=(b, t // tb),
        in_specs=[
            pl.BlockSpec((None, tb, gd), lambda bi, ti: (bi, ti, COL_QKV // gd)),
            pl.BlockSpec((None, tb, gd), lambda bi, ti: (bi, ti, COL_QKV // gd + 1)),
            pl.BlockSpec((None, tb, gd), lambda bi, ti: (bi, ti, COL_QKV // gd + 2)),
            pl.BlockSpec((None, tb, gd), lambda bi, ti: (bi, ti, COL_ZA // gd)),
            pl.BlockSpec((None, tb, LANES), lambda bi, ti: (bi, ti, COL_SMALL // LANES)),
            pl.BlockSpec((CONV_WIDTH, GDN_QKV_DIM), const2),
            pl.BlockSpec((None, SUBLANES, GDN_QKV_DIM), lambda bi, ti: (bi, 0, 0)),
            pl.BlockSpec((1, LANES), const2),
            pl.BlockSpec((1, LANES), const2),
            pl.BlockSpec((1, hd), const2),
            pl.BlockSpec((None, GDN_HEADS, hd, hd), lambda bi, ti: (bi, 0, 0, 0)),
        ],
        out_specs=[
            pl.BlockSpec((None, tb, gd), lambda bi, ti: (bi, ti, 0)),
            pl.BlockSpec((None, GDN_HEADS, hd, hd), lambda bi, ti: (bi, 0, 0, 0)),
        ],
        out_shape=[
            jax.ShapeDtypeStruct((b, t, gd), F32),
            jax.ShapeDtypeStruct((b, GDN_HEADS, hd, hd), F32),
        ],
        scratch_shapes=[
            pltpu.VMEM((GDN_HEADS, hd, hd), F32),
            pltpu.VMEM((SUBLANES, GDN_QKV_DIM), F32),
            pltpu.VMEM((tb, gd), F32),
            pltpu.VMEM((tb, gd), F32),
            pltpu.VMEM((tb, gd), F32),
            pltpu.VMEM((tb, gd), F32),
            pltpu.VMEM((tb, gd), F32),
            pltpu.VMEM((tb // chunk, LANES, chunk), F32),
        ],
        compiler_params=pltpu.CompilerParams(
            dimension_semantics=("parallel", "arbitrary"), vmem_limit_bytes=VMEM_LIMIT),
    )(p3, p3, p3, p3, p3, conv_w, prefix, alog, dtb, gain, s0)


def _masked_softmax2(s, mask):
    s = jnp.where(mask, s, NEG_INF)
    e = jnp.where(mask, jnp.exp2(s - jnp.max(s, axis=-1, keepdims=True)), 0.0)
    return e / jnp.maximum(jnp.sum(e, axis=-1, keepdims=True), 1e-30)


def _stack_heads(q, g):
    hd = NSA_HEAD_DIM
    parts = [q[:, (NSA_GROUP * g + h) * hd:(NSA_GROUP * g + h + 1) * hd] for h in range(NSA_GROUP)]
    return jnp.concatenate(parts, axis=0) * (NSA_SCALE * LOG2E)


def _tile_rows(x):
    return jnp.concatenate([x] * NSA_GROUP, axis=0)


def _group_importance(p_c, tq):
    imp = p_c[0:tq]
    for h in range(1, NSA_GROUP):
        imp = imp + p_c[h * tq:(h + 1) * tq]
    return imp


def _select_blocks(p_c, pos, tq, n_blocks, nbl):
    m = p_c.shape[1]
    imp = _group_importance(p_c, tq)
    pool = ((_iota((m, nbl), 0) >> int(math.log2(SEL_RATIO))) == _iota((m, nbl), 1)).astype(F32)
    imp_sel = _dot(imp, pool, HI)
    lane = _iota((tq, nbl), 1)
    valid = lane * SEL_BLOCK <= pos
    cur = pos >> SEL_SHIFT
    forced = (lane == 0) | (lane == cur) | (lane == cur - 1)
    score = jnp.where(valid, imp_sel + FORCE_BONUS * forced.astype(F32), -1.0)
    score = jnp.where(lane < n_blocks, score, -2.0)
    rank = jnp.zeros((tq, nbl), F32)
    for j in range(n_blocks):
        sj = score[:, j:j + 1]
        rank = rank + ((sj > score) | ((sj == score) & (lane > j))).astype(F32)
    return (rank < float(min(N_SELECT, n_blocks))).astype(F32)


def _select_blocks_t(p_c, q0, tq, n_blocks, nbl):
    m = p_c.shape[1]
    n_sel = min(N_SELECT, n_blocks)
    nbs = -(-n_blocks // SUBLANES) * SUBLANES
    imp = _group_importance(p_c, tq)
    pool_t = ((_iota((nbs, m), 1) >> int(math.log2(SEL_RATIO))) == _iota((nbs, m), 0)).astype(F32)
    imp_t = _dot_nt(pool_t, imp, HI)
    pos_r = q0 + _iota((1, tq), 1)
    blk = _iota((nbs, tq), 0)
    valid = blk * SEL_BLOCK <= pos_r

    def ranked():
        cur = pos_r >> SEL_SHIFT
        forced = (blk == 0) | (blk == cur) | (blk == cur - 1)
        score = jnp.where(valid, imp_t + FORCE_BONUS * forced.astype(F32), -1.0)
        score = jnp.where(blk < n_blocks, score, -2.0)
        rank = jnp.zeros((nbs, tq), F32)
        for j in range(n_blocks):
            sj = score[j:j + 1, :]
            rank = rank + ((sj > score) | ((sj == score) & (blk > j))).astype(F32)
        return (rank < float(n_sel)).astype(F32)

    sel_t = lax.cond(q0 + tq <= n_sel * SEL_BLOCK, lambda: valid.astype(F32), ranked)
    sel_t = jnp.concatenate([sel_t, jnp.zeros((nbl - nbs, tq), F32)], axis=0)
    return sel_t.T


def _compress_rows(x, pw):
    n = x.shape[0] // CMP_STRIDE
    x3 = x.reshape(n, CMP_STRIDE, LANES)
    lo = jnp.sum(x3 * pw[:CMP_STRIDE][None], axis=1)
    hi = jnp.sum(x3 * pw[CMP_STRIDE:][None], axis=1)
    return lo, hi


def _combine_heads(o_ref, g, gates, zb, o_c, o_s, o_w, tq):
    hd = NSA_HEAD_DIM
    for h in range(NSA_GROUP):
        hh = NSA_GROUP * g + h
        c0 = SMALL_GB + 3 * hh
        rows = slice(h * tq, (h + 1) * tq)
        o_h = (gates[:, c0:c0 + 1] * o_c[rows] + gates[:, c0 + 1:c0 + 2] * o_s[rows]
               + gates[:, c0 + 2:c0 + 3] * o_w[rows])
        o_ref[:, hh * hd:(hh + 1) * hd] = o_h * _silu(zb[:, hh * hd:(hh + 1) * hd])


def _nsa_prompt_kernel(q_ref, zb_ref, sm_ref, kcmp_ref, vcmp_ref, kslc_ref, vslc_ref, kwin_ref, vwin_ref,
                       pk_ref, pv_ref, wk_ref, wv_ref, o_ref, kc_sc, vc_sc, kv_sc, *, t_len, tq, tk, span, nbl):
    i = pl.program_id(1)
    m = t_len // CMP_STRIDE
    n_blocks = t_len // SEL_BLOCK
    hd = NSA_HEAD_DIM

    @pl.when(i == 0)
    def _():
        for src, pw_ref, w_ref, dst in ((kcmp_ref, pk_ref, wk_ref, kc_sc), (vcmp_ref, pv_ref, wv_ref, vc_sc)):
            lo, hi = _compress_rows(src[...], pw_ref[...])
            hi_next = pltpu.roll(hi, m - 1, 0)
            pre = jnp.where(_iota((m, LANES), 0) < m - 1, lo + hi_next, 0.0)
            dst[...] = _dot(pre, w_ref[...], HI)
        for idx, src in enumerate((kslc_ref, vslc_ref, kwin_ref, vwin_ref)):
            kv_sc[idx] = src[...].astype(BF16)

    q0 = i * tq
    pos = q0 + _iota((tq, 1), 0)
    pos4 = _tile_rows(pos)
    q = q_ref[...]
    zb = zb_ref[...]
    gates = jax.nn.sigmoid(sm_ref[...])
    c_end = _iota((1, m), 1) * CMP_STRIDE + (CMP_LEN - 1)
    n_kt = (q0 + tq + tk - 1) // tk
    w_start = pl.multiple_of(jnp.clip(q0 - WINDOW, 0, t_len - span), tq)
    w_dist = pos - (w_start + _iota((1, span), 1))
    w_bias = _tile_rows(jnp.where((w_dist >= 0) & (w_dist <= WINDOW), 0.0, NEG_INF))

    for g in range(NSA_KV_HEADS):
        gs = slice(g * hd, (g + 1) * hd)
        qs = _stack_heads(q, g)
        qs_bf = qs.astype(BF16)
        p_c = _masked_softmax2(_dot_nt(qs, kc_sc[:, gs], HI), c_end <= pos4)
        o_c = _dot(p_c, vc_sc[:, gs])
        sel = _select_blocks_t(p_c, q0, tq, n_blocks, nbl)
        sel_bias = ((sel - 1.0) * -NEG_INF).astype(BF16)

        def sel_body(kt, carry, qs_bf=qs_bf, sel_bias=sel_bias, gs=gs):
            m_run, l_run, acc = carry
            k0 = pl.multiple_of(kt * tk, tk)
            expand = ((k0 + _iota((nbl, tk), 1)) >> SEL_SHIFT) == _iota((nbl, tk), 0)
            bias = _dot(sel_bias, expand.astype(BF16)) + jnp.where(k0 + _iota((1, tk), 1) <= pos, 0.0, NEG_INF)
            s = _dot_nt(qs_bf, kv_sc[0, pl.ds(k0, tk), gs]) + _tile_rows(bias)
            m_new = jnp.maximum(m_run, jnp.max(s, axis=-1, keepdims=True))
            alpha = jnp.exp2(m_run - m_new)
            e = jnp.exp2(s - m_new)
            l_new = alpha * l_run + jnp.sum(e, axis=-1, keepdims=True)
            acc_new = alpha * acc + _dot(e.astype(BF16), kv_sc[1, pl.ds(k0, tk), gs])
            return m_new, l_new, acc_new

        init = (jnp.full((NSA_GROUP * tq, 1), NEG_INF, F32), jnp.zeros((NSA_GROUP * tq, 1), F32),
                jnp.zeros((NSA_GROUP * tq, hd), F32))
        _, l_s, acc_s = lax.fori_loop(0, n_kt, sel_body, init)
        o_s = acc_s / jnp.maximum(l_s, 1e-30)

        s_w = _dot_nt(qs_bf, kv_sc[2, pl.ds(w_start, span), gs]) + w_bias
        e_w = jnp.exp2(s_w - jnp.max(s_w, axis=-1, keepdims=True))
        o_w = (_dot(e_w.astype(BF16), kv_sc[3, pl.ds(w_start, span), gs])
               / jnp.maximum(jnp.sum(e_w, axis=-1, keepdims=True), 1e-30))
        _combine_heads(o_ref, g, gates, zb, o_c, o_s, o_w, tq)


def _nsa_prompt(p3, pos_k, pos_v, w_k, w_v, *, tq, tk):
    b, t, _ = p3.shape
    span = WINDOW + tq
    assert t % tq == 0 and t % tk == 0 and t >= span and t % SEL_BLOCK == 0 and tq % LANES == 0
    assert WINDOW % tq == 0
    m = t // CMP_STRIDE
    nbl = -(-(t // SEL_BLOCK) // LANES) * LANES
    kvc = COL_KV // LANES

    def full(c):
        return pl.BlockSpec((None, t, LANES), lambda bi, i: (bi, 0, c))

    const2 = lambda bi, i: (0, 0)
    kern = functools.partial(_nsa_prompt_kernel, t_len=t, tq=tq, tk=tk, span=span, nbl=nbl)
    return pl.pallas_call(
        kern,
        grid=(b, t // tq),
        in_specs=[
            pl.BlockSpec((None, tq, NSA_DIM), lambda bi, i: (bi, i, COL_QB // NSA_DIM)),
            pl.BlockSpec((None, tq, NSA_DIM), lambda bi, i: (bi, i, COL_ZB // NSA_DIM)),
            pl.BlockSpec((None, tq, LANES), lambda bi, i: (bi, i, COL_SMALL // LANES)),
            full(kvc), full(kvc + 1), full(kvc + 2), full(kvc + 3), full(kvc + 4), full(kvc + 5),
            pl.BlockSpec((CMP_LEN, LANES), const2),
            pl.BlockSpec((CMP_LEN, LANES), const2),
            pl.BlockSpec((LANES, LANES), const2),
            pl.BlockSpec((LANES, LANES), const2),
        ],
        out_specs=pl.BlockSpec((None, tq, NSA_DIM), lambda bi, i: (bi, i, 0)),
        out_shape=jax.ShapeDtypeStruct((b, t, NSA_DIM), F32),
        scratch_shapes=[pltpu.VMEM((m, LANES), F32), pltpu.VMEM((m, LANES), F32),
                        pltpu.VMEM((4, t, LANES), BF16)],
        compiler_params=pltpu.CompilerParams(
            dimension_semantics=("parallel", "arbitrary"), vmem_limit_bytes=VMEM_LIMIT),
    )(p3, p3, p3, p3, p3, p3, p3, p3, p3, pos_k, pos_v, w_k, w_v)


def _cmp_paged_kernel(pt_ref, *refs, k_pages):
    del pt_ref
    kp = refs[:k_pages]
    kn = refs[k_pages]
    vp = refs[k_pages + 1:2 * k_pages + 1]
    vn = refs[2 * k_pages + 1]
    pk_ref, pv_ref, wk_ref, wv_ref, kc_ref, vc_ref = refs[2 * k_pages + 2:]
    n = kc_ref.shape[0]
    last_row = _iota((n, LANES), 0) == n - 1
    for pages, nxt, pw_ref, w_ref, out in ((kp, kn, pk_ref, wk_ref, kc_ref), (vp, vn, pv_ref, wv_ref, vc_ref)):
        pw = pw_ref[...]
        parts = [_compress_rows(r[...], pw) for r in pages]
        lo = jnp.concatenate([p[0] for p in parts], axis=0)
        hi = jnp.concatenate([p[1] for p in parts], axis=0)
        hi_head = jnp.sum(nxt[...] * pw[CMP_STRIDE:], axis=0, keepdims=True)
        hi_next = jnp.where(last_row, hi_head, pltpu.roll(hi, n - 1, 0))
        out[...] = _dot(lo + hi_next, w_ref[...], HI)


def _cmp_paged(pool_k, pool_v, page_table, pos_k, pos_v, w_k, w_v, *, k_pages):
    b, n_pages = page_table.shape
    page = pool_k.shape[1]
    assert page == LANES and n_pages % k_pages == 0
    per_step = k_pages * page // CMP_STRIDE

    def page_spec(i):
        return pl.BlockSpec((None, page, LANES), lambda bi, j, pt: (pt[bi, j * k_pages + i], 0, 0))

    next_spec = pl.BlockSpec(
        (None, CMP_STRIDE, LANES),
        lambda bi, j, pt: (pt[bi, jnp.minimum((j + 1) * k_pages, n_pages - 1)], 0, 0))
    const2 = lambda bi, j, pt: (0, 0)
    in_specs = ([page_spec(i) for i in range(k_pages)] + [next_spec]
                + [page_spec(i) for i in range(k_pages)] + [next_spec]
                + [pl.BlockSpec((CMP_LEN, LANES), const2)] * 2 + [pl.BlockSpec((LANES, LANES), const2)] * 2)
    out_spec = pl.BlockSpec((None, per_step, LANES), lambda bi, j, pt: (bi, j, 0))
    m = n_pages * page // CMP_STRIDE
    return pl.pallas_call(
        functools.partial(_cmp_paged_kernel, k_pages=k_pages),
        grid_spec=pltpu.PrefetchScalarGridSpec(
            num_scalar_prefetch=1, grid=(b, n_pages // k_pages), in_specs=in_specs,
            out_specs=[out_spec, out_spec]),
        out_shape=[jax.ShapeDtypeStruct((b, m, LANES), F32)] * 2,
        compiler_params=pltpu.CompilerParams(
            dimension_semantics=("parallel", "arbitrary"), vmem_limit_bytes=VMEM_LIMIT),
    )(page_table, *([pool_k] * (k_pages + 1)), *([pool_v] * (k_pages + 1)), pos_k, pos_v, w_k, w_v)


def _new_key_scores(qs, new_k, r, gs):
    return jnp.sum(qs * new_k[r:r + 1, gs], axis=-1, keepdims=True)


def _nsa_sample_a_kernel(q_ref, kc_ref, vc_ref, bk_ref, bv_ref, nk_ref, nv_ref,
                         oc_ref, ow_ref, sel_ref, wk_out, wv_out, *, past, t_valid, tp, n_blocks, nbl):
    hd = NSA_HEAD_DIM
    m = kc_ref.shape[0]
    w_buf = bk_ref.shape[0]
    pos = past + _iota((tp, 1), 0)
    pos4 = _tile_rows(pos)
    q = q_ref[...]
    new_k = nk_ref[...]
    new_v = nv_ref[...]
    c_end = _iota((1, m), 1) * CMP_STRIDE + (CMP_LEN - 1)
    buf_pos = (past - w_buf) + _iota((1, w_buf), 1)
    buf_dist = pos4 - buf_pos
    buf_mask = (buf_dist >= 0) & (buf_dist <= WINDOW)
    for g in range(NSA_KV_HEADS):
        gs = slice(g * hd, (g + 1) * hd)
        qs = _stack_heads(q, g)
        p_c = _masked_softmax2(_dot_nt(qs, kc_ref[:, gs], HI), c_end <= pos4)
        oc_ref[g] = _dot(p_c, vc_ref[:, gs])
        sel_ref[g] = _select_blocks(p_c, pos, tp, n_blocks, nbl)
        s_buf = jnp.where(buf_mask, _dot_nt(qs, bk_ref[:, gs]), NEG_INF)
        s_new, new_mask = [], []
        for r in range(t_valid):
            dist = pos4 - (past + r)
            mk = (dist >= 0) & (dist <= WINDOW)
            new_mask.append(mk)
            s_new.append(jnp.where(mk, _new_key_scores(qs, new_k, r, gs), NEG_INF))
        m_all = jnp.max(s_buf, axis=-1, keepdims=True)
        for sn in s_new:
            m_all = jnp.maximum(m_all, sn)
        e_buf = jnp.where(buf_mask, jnp.exp2(s_buf - m_all), 0.0)
        l_all = jnp.sum(e_buf, axis=-1, keepdims=True)
        acc = _dot(e_buf, bv_ref[:, gs])
        for r in range(t_valid):
            e_r = jnp.where(new_mask[r], jnp.exp2(s_new[r] - m_all), 0.0)
            l_all = l_all + e_r
            acc = acc + e_r * new_v[r:r + 1, gs]
        ow_ref[g] = acc / jnp.maximum(l_all, 1e-30)
    wk_out[0:w_buf - t_valid, :] = bk_ref[t_valid:w_buf, :]
    wk_out[w_buf - t_valid:w_buf, :] = new_k[0:t_valid]
    wv_out[0:w_buf - t_valid, :] = bv_ref[t_valid:w_buf, :]
    wv_out[w_buf - t_valid:w_buf, :] = new_v[0:t_valid]


def _nsa_sample_a(p3, kc, vc, buf_k, buf_v, *, past, t_valid, n_blocks, nbl):
    b, tp, _ = p3.shape
    m = kc.shape[1]
    w_buf = buf_k.shape[1]
    assert w_buf == WINDOW and t_valid <= w_buf
    kvc = COL_KV // LANES
    rows = NSA_GROUP * tp
    kern = functools.partial(_nsa_sample_a_kernel, past=past, t_valid=t_valid, tp=tp,
                             n_blocks=n_blocks, nbl=nbl)
    per_b3 = lambda bi: (bi, 0, 0)
    per_b4 = lambda bi: (bi, 0, 0, 0)
    return pl.pallas_call(
        kern,
        grid=(b,),
        in_specs=[
            pl.BlockSpec((None, tp, NSA_DIM), lambda bi: (bi, 0, COL_QB // NSA_DIM)),
            pl.BlockSpec((None, m, LANES), per_b3),
            pl.BlockSpec((None, m, LANES), per_b3),
            pl.BlockSpec((None, w_buf, LANES), per_b3),
            pl.BlockSpec((None, w_buf, LANES), per_b3),
            pl.BlockSpec((None, tp, LANES), lambda bi: (bi, 0, kvc + 4)),
            pl.BlockSpec((None, tp, LANES), lambda bi: (bi, 0, kvc + 5)),
        ],
        out_specs=[
            pl.BlockSpec((None, NSA_KV_HEADS, rows, NSA_HEAD_DIM), per_b4),
            pl.BlockSpec((None, NSA_KV_HEADS, rows, NSA_HEAD_DIM), per_b4),
            pl.BlockSpec((None, NSA_KV_HEADS, tp, nbl), per_b4),
            pl.BlockSpec((None, w_buf, LANES), per_b3),
            pl.BlockSpec((None, w_buf, LANES), per_b3),
        ],
        out_shape=[
            jax.ShapeDtypeStruct((b, NSA_KV_HEADS, rows, NSA_HEAD_DIM), F32),
            jax.ShapeDtypeStruct((b, NSA_KV_HEADS, rows, NSA_HEAD_DIM), F32),
            jax.ShapeDtypeStruct((b, NSA_KV_HEADS, tp, nbl), F32),
            jax.ShapeDtypeStruct((b, w_buf, LANES), F32),
            jax.ShapeDtypeStruct((b, w_buf, LANES), F32),
        ],
        compiler_params=pltpu.CompilerParams(
            dimension_semantics=("parallel",), vmem_limit_bytes=VMEM_LIMIT),
    )(p3, kc, vc, buf_k, buf_v, p3, p3)


def _nsa_sample_b_kernel(pt_ref, q_ref, zb_ref, sm_ref, nk_ref, nv_ref, sel_ref, oc_ref, ow_ref, *refs,
                         k_pages, past, t_valid, tp, n_blocks, nbl):
    del pt_ref
    kp = refs[:k_pages]
    vp = refs[k_pages:2 * k_pages]
    o_ref, m_sc, l_sc, acc_sc = refs[2 * k_pages:]
    hd = NSA_HEAD_DIM
    j = pl.program_id(1)
    rows = NSA_GROUP * tp

    @pl.when(j == 0)
    def _():
        m_sc[...] = jnp.full(m_sc.shape, NEG_INF, F32)
        l_sc[...] = jnp.zeros(l_sc.shape, F32)
        acc_sc[...] = jnp.zeros(acc_sc.shape, F32)

    q = q_ref[...]
    k_all = jnp.concatenate([r[...] for r in kp], axis=0)
    v_all = jnp.concatenate([r[...] for r in vp], axis=0)
    n_keys = k_all.shape[0]
    blk0 = j * (n_keys // SEL_BLOCK)
    expand = (blk0 + (_iota((nbl, n_keys), 1) >> SEL_SHIFT)) == _iota((nbl, n_keys), 0)
    expand = expand.astype(BF16)
    for g in range(NSA_KV_HEADS):
        gs = slice(g * hd, (g + 1) * hd)
        qs = _stack_heads(q, g)
        mask = _tile_rows(_dot(sel_ref[g].astype(BF16), expand) > 0.5)
        s = jnp.where(mask, _dot_nt(qs, k_all[:, gs]), NEG_INF)
        m_run = m_sc[g]
        m_new = jnp.maximum(m_run, jnp.max(s, axis=-1, keepdims=True))
        alpha = jnp.exp2(m_run - m_new)
        e = jnp.where(mask, jnp.exp2(s - m_new), 0.0)
        l_sc[g] = alpha * l_sc[g] + jnp.sum(e, axis=-1, keepdims=True)
        acc_sc[g] = alpha * acc_sc[g] + _dot(e, v_all[:, gs])
        m_sc[g] = m_new

    @pl.when(j == pl.num_programs(1) - 1)
    def _():
        pos4 = _tile_rows(past + _iota((tp, 1), 0))
        gates = jax.nn.sigmoid(sm_ref[...])
        zb = zb_ref[...]
        new_k = nk_ref[...]
        new_v = nv_ref[...]
        for g in range(NSA_KV_HEADS):
            gs = slice(g * hd, (g + 1) * hd)
            qs = _stack_heads(q, g)
            picked = _tile_rows(sel_ref[g][:, n_blocks - 1:n_blocks] > 0.5)
            m_run, l_run, acc = m_sc[g], l_sc[g], acc_sc[g]
            s_new, new_mask = [], []
            for r in range(t_valid):
                mk = picked & (past + r <= pos4)
                new_mask.append(mk)
                s_new.append(jnp.where(mk, _new_key_scores(qs, new_k, r, gs), NEG_INF))
            m_all = m_run
            for sn in s_new:
                m_all = jnp.maximum(m_all, sn)
            alpha = jnp.exp2(m_run - m_all)
            l_all = alpha * l_run
            acc = alpha * acc
            for r in range(t_valid):
                e_r = jnp.where(new_mask[r], jnp.exp2(s_new[r] - m_all), 0.0)
                l_all = l_all + e_r
                acc = acc + e_r * new_v[r:r + 1, gs]
            o_s = acc / jnp.maximum(l_all, 1e-30)
            _combine_heads(o_ref, g, gates, zb, oc_ref[g], o_s, ow_ref[g], tp)


def _nsa_sample_b(p3, sel, o_c, o_w, pool_k, pool_v, page_table, *, k_pages, past, t_valid, n_blocks, nbl):
    b, tp, _ = p3.shape
    n_pages = page_table.shape[1]
    page = pool_k.shape[1]
    assert n_pages % k_pages == 0 and page % SEL_BLOCK == 0 and past == n_pages * page
    assert n_blocks == past // SEL_BLOCK + 1 and t_valid <= SEL_BLOCK
    kvc = COL_KV // LANES
    rows = NSA_GROUP * tp

    def page_spec(i):
        return pl.BlockSpec((None, page, LANES), lambda bi, j, pt: (pt[bi, j * k_pages + i], 0, 0))

    per_b4 = lambda bi, j, pt: (bi, 0, 0, 0)
    in_specs = [
        pl.BlockSpec((None, tp, NSA_DIM), lambda bi, j, pt: (bi, 0, COL_QB // NSA_DIM)),
        pl.BlockSpec((None, tp, NSA_DIM), lambda bi, j, pt: (bi, 0, COL_ZB // NSA_DIM)),
        pl.BlockSpec((None, tp, LANES), lambda bi, j, pt: (bi, 0, COL_SMALL // LANES)),
        pl.BlockSpec((None, tp, LANES), lambda bi, j, pt: (bi, 0, kvc + 2)),
        pl.BlockSpec((None, tp, LANES), lambda bi, j, pt: (bi, 0, kvc + 3)),
        pl.BlockSpec((None, NSA_KV_HEADS, tp, nbl), per_b4),
        pl.BlockSpec((None, NSA_KV_HEADS, rows, NSA_HEAD_DIM), per_b4),
        pl.BlockSpec((None, NSA_KV_HEADS, rows, NSA_HEAD_DIM), per_b4),
    ] + [page_spec(i) for i in range(k_pages)] * 2
    kern = functools.partial(_nsa_sample_b_kernel, k_pages=k_pages, past=past, t_valid=t_valid, tp=tp,
                             n_blocks=n_blocks, nbl=nbl)
    return pl.pallas_call(
        kern,
        grid_spec=pltpu.PrefetchScalarGridSpec(
            num_scalar_prefetch=1, grid=(b, n_pages // k_pages), in_specs=in_specs,
            out_specs=pl.BlockSpec((None, tp, NSA_DIM), lambda bi, j, pt: (bi, 0, 0)),
            scratch_shapes=[pltpu.VMEM((NSA_KV_HEADS, rows, 1), F32), pltpu.VMEM((NSA_KV_HEADS, rows, 1), F32),
                            pltpu.VMEM((NSA_KV_HEADS, rows, NSA_HEAD_DIM), F32)]),
        out_shape=jax.ShapeDtypeStruct((b, tp, NSA_DIM), F32),
        compiler_params=pltpu.CompilerParams(
            dimension_semantics=("parallel", "arbitrary"), vmem_limit_bytes=VMEM_LIMIT),
    )(page_table, p3, p3, p3, p3, p3, sel, o_c, o_w, *([pool_k] * k_pages), *([pool_v] * k_pages))


def _merge_kernel(x_ref, oa_ref, ob_ref, ga_ref, gb_ref, wa_ref, wb_ref, wo_ref, fg_ref, y_ref):
    mixed = (jax.nn.sigmoid(ga_ref[...]) * _dot(oa_ref[...].astype(BF16), wa_ref[...])
             + jax.nn.sigmoid(gb_ref[...]) * _dot(ob_ref[...].astype(BF16), wb_ref[...]))
    x_new = x_ref[...] + _dot(mixed.astype(BF16), wo_ref[...])
    y = x_new * lax.rsqrt(jnp.mean(x_new * x_new, axis=-1, keepdims=True) + RMS_EPS)
    y_ref[...] = y * fg_ref[...]


def _merge(x2d, o_a, o_b, p2d, w_a, w_b, w_o, final_gain, tm):
    rows, d = x2d.shape
    const2 = lambda i: (0, 0)
    return pl.pallas_call(
        _merge_kernel,
        grid=(rows // tm,),
        in_specs=[
            pl.BlockSpec((tm, d), lambda i: (i, 0)),
            pl.BlockSpec((tm, GDN_DIM), lambda i: (i, 0)),
            pl.BlockSpec((tm, NSA_DIM), lambda i: (i, 0)),
            pl.BlockSpec((tm, d), lambda i: (i, COL_GATE_A // D_MODEL)),
            pl.BlockSpec((tm, d), lambda i: (i, COL_GATE_B // D_MODEL)),
            pl.BlockSpec((GDN_DIM, d), const2),
            pl.BlockSpec((NSA_DIM, d), const2),
            pl.BlockSpec((d, d), const2),
            pl.BlockSpec((1, d), const2),
        ],
        out_specs=pl.BlockSpec((tm, d), lambda i: (i, 0)),
        out_shape=jax.ShapeDtypeStruct((rows, d), F32),
        compiler_params=pltpu.CompilerParams(
            dimension_semantics=("parallel",), vmem_limit_bytes=VMEM_LIMIT),
    )(x2d, o_a, o_b, p2d, p2d, w_a, w_b, w_o, final_gain)


def _pack_w_in(w_in):
    bounds = [0]
    for width in (GDN_QKV_DIM, GDN_HEADS, GDN_HEADS, GDN_DIM, NSA_DIM, 6 * NSA_KV_DIM, 3 * NSA_HEADS,
                  NSA_DIM, 2 * D_MODEL):
        bounds.append(bounds[-1] + width)
    qkv, a, bt, z_a, q_b, kv_b, g_b, z_b, merge = (w_in[:, bounds[i]:bounds[i + 1]] for i in range(9))
    small = jnp.concatenate([a, bt, g_b], axis=1)
    small = jnp.pad(small, ((0, 0), (0, LANES - small.shape[1])))
    packed = jnp.concatenate([qkv, z_a, merge, q_b, z_b, kv_b, small], axis=1)
    assert packed.shape[1] == N_PACKED
    return packed.astype(BF16)


def _lane_vec(v):
    return jnp.pad(v.astype(F32), (0, LANES - v.shape[0])).reshape(1, LANES)


def _block_diag2(w):
    z = jnp.zeros_like(w)
    return jnp.concatenate([jnp.concatenate([w, z], axis=1), jnp.concatenate([z, w], axis=1)], axis=0)


def _tile_for(n, candidates):
    for c in candidates:
        if n % c == 0:
            return c
    raise ValueError(f"no tile for {n}")


def kernel(x_prompt, x_sample, cache_k_cmp, cache_v_cmp, cache_k_slc, cache_v_slc, state_win_k, state_win_v,
           state_conv, state_gdn, page_table, norm_gain, w_in, conv_w, a_log, dt_bias, gdn_gain, cmp_pos_k,
           cmp_w_k, cmp_pos_v, cmp_w_v, w_branch_a, w_branch_b, w_out, final_gain):
    assert w_in.shape[0] == 1, "one layer"
    bp, tp_len, d = x_prompt.shape
    bs, ts, _ = x_sample.shape
    n_pool, page = cache_k_cmp.shape[1:3]
    n_pages = page_table.shape[1]
    past = n_pages * page
    assert past % CMP_STRIDE == 0 and ts < CMP_STRIDE and ts >= CONV_WIDTH - 1 and ts <= SUBLANES
    t_pad = SUBLANES

    w_packed = _pack_w_in(w_in[0])
    gain_in = norm_gain[0].reshape(1, d)
    fgain = final_gain.reshape(1, d)
    w_a = w_branch_a[0].astype(BF16)
    w_b = w_branch_b[0].astype(BF16)
    w_o = w_out[0].astype(BF16)
    cw = conv_w[0]
    alog = _lane_vec(a_log[0])
    dtb = _lane_vec(dt_bias[0])
    ggain = gdn_gain[0].reshape(1, GDN_HEAD_DIM)
    pos_k = jnp.tile(cmp_pos_k[0], (1, NSA_KV_HEADS))
    pos_v = jnp.tile(cmp_pos_v[0], (1, NSA_KV_HEADS))
    wk2 = _block_diag2(cmp_w_k[0])
    wv2 = _block_diag2(cmp_w_v[0])
    tn = _tile_for(N_PACKED, (1152, 896, 128))
    kvc = COL_KV

    rows_p = bp * tp_len
    x2p = x_prompt.reshape(rows_p, d)
    p2 = _proj(x2p, gain_in, w_packed, _tile_for(rows_p, (1024, 512, 256, 128)), tn)
    p3 = p2.reshape(bp, tp_len, N_PACKED)
    tb = _tile_for(tp_len, (256, 128, 64))
    o_a, gdn_p = _gdn(p3, cw, jnp.zeros((bp, SUBLANES, GDN_QKV_DIM), F32), alog, dtb, ggain,
                      jnp.zeros((bp, GDN_HEADS, GDN_HEAD_DIM, GDN_HEAD_DIM), F32),
                      tb=tb, chunk=GDN_CHUNK, t_valid=tp_len)
    o_b = _nsa_prompt(p3, pos_k, pos_v, wk2, wv2, tq=128, tk=_tile_for(tp_len, (512, 256, 128)))
    y_p = _merge(x2p, o_a.reshape(rows_p, GDN_DIM), o_b.reshape(rows_p, NSA_DIM), p2, w_a, w_b, w_o, fgain,
                 _tile_for(rows_p, (512, 256, 128)))
    y_prompt = y_p.reshape(bp, tp_len, d)

    def kv_out(p, i, t_keep_from, t_to):
        return p[:, t_keep_from:t_to, kvc + i * LANES:kvc + (i + 1) * LANES].reshape(
            1, p.shape[0], t_to - t_keep_from, NSA_KV_HEADS, NSA_HEAD_DIM)

    keep_p = min(WINDOW, tp_len)
    prompt_state = ([kv_out(p3, i, 0, tp_len) for i in range(4)]
                    + [kv_out(p3, i, tp_len - keep_p, tp_len) for i in (4, 5)]
                    + [p3[:, tp_len - (CONV_WIDTH - 1):, :GDN_QKV_DIM][None], gdn_p[None]])

    xs_pad = jnp.pad(x_sample, ((0, 0), (0, t_pad - ts), (0, 0)))
    rows_s = bs * t_pad
    x2s = xs_pad.reshape(rows_s, d)
    ps2 = _proj(x2s, gain_in, w_packed, _tile_for(rows_s, (512, 256, 128, 8)), tn)
    ps3 = ps2.reshape(bs, t_pad, N_PACKED)
    prefix_s = jnp.pad(state_conv[0], ((0, 0), (SUBLANES - (CONV_WIDTH - 1), 0), (0, 0)))
    o_a_s, gdn_s = _gdn(ps3, cw, prefix_s, alog, dtb, ggain, state_gdn[0], tb=t_pad, chunk=t_pad, t_valid=ts)

    pools = [c[0].reshape(n_pool, page, NSA_KV_DIM) for c in (cache_k_cmp, cache_v_cmp, cache_k_slc, cache_v_slc)]
    kc_s, vc_s = _cmp_paged(pools[0], pools[1], page_table, pos_k, pos_v, wk2, wv2,
                            k_pages=_tile_for(n_pages, (16, 8, 4, 2, 1)))
    n_blocks = -(-(past + ts) // SEL_BLOCK)
    nbl = -(-n_blocks // LANES) * LANES
    w_buf = state_win_k.shape[2]
    o_c_s, o_w_s, sel_s, win_k_s, win_v_s = _nsa_sample_a(
        ps3, kc_s, vc_s, state_win_k[0].reshape(bs, w_buf, NSA_KV_DIM), state_win_v[0].reshape(bs, w_buf, NSA_KV_DIM),
        past=past, t_valid=ts, n_blocks=n_blocks, nbl=nbl)
    o_b_s = _nsa_sample_b(ps3, sel_s, o_c_s, o_w_s, pools[2], pools[3], page_table,
                          k_pages=_tile_for(n_pages, (8, 4, 2, 1)),
                          past=past, t_valid=ts, n_blocks=n_blocks, nbl=nbl)
    y_s = _merge(x2s, o_a_s.reshape(rows_s, GDN_DIM), o_b_s.reshape(rows_s, NSA_DIM), ps2, w_a, w_b, w_o, fgain,
                 _tile_for(rows_s, (512, 256, 128, 8)))
    y_sample = y_s.reshape(bs, t_pad, d)[:, :ts]

    keep_s = min(WINDOW, w_buf + ts)
    assert keep_s == w_buf
    sample_state = ([kv_out(ps3, i, 0, ts) for i in range(4)]
                    + [win_k_s.reshape(1, bs, keep_s, NSA_KV_HEADS, NSA_HEAD_DIM),
                       win_v_s.reshape(1, bs, keep_s, NSA_KV_HEADS, NSA_HEAD_DIM),
                       ps3[:, ts - (CONV_WIDTH - 1):ts, :GDN_QKV_DIM][None], gdn_s[None]])
    return (y_prompt, y_sample, *prompt_state, *sample_state)
```

```python
import functools
import math

import jax
import jax.numpy as jnp
from jax import lax
from jax.experimental import pallas as pl
from jax.experimental.pallas import tpu as pltpu

F32 = jnp.float32
BF16 = jnp.bfloat16
HI = lax.Precision.HIGHEST

LANES = 128
SUBLANES = 8
VMEM_LIMIT = 56 * 1024 * 1024

D_MODEL = 1024
GDN_HEADS = 8
GDN_HEAD_DIM = 128
GDN_DIM = GDN_HEADS * GDN_HEAD_DIM
GDN_QKV_DIM = 3 * GDN_DIM
CONV_WIDTH = 4
GDN_CHUNK = 64
NSA_HEADS = 8
NSA_KV_HEADS = 2
NSA_GROUP = NSA_HEADS // NSA_KV_HEADS
NSA_HEAD_DIM = 64
NSA_DIM = NSA_HEADS * NSA_HEAD_DIM
NSA_KV_DIM = NSA_KV_HEADS * NSA_HEAD_DIM
CMP_STRIDE = 16
CMP_LEN = 2 * CMP_STRIDE
SEL_BLOCK = 64
SEL_SHIFT = 6
SEL_RATIO = SEL_BLOCK // CMP_STRIDE
N_SELECT = 16
WINDOW = 512
FORCE_BONUS = 100.0
RMS_EPS = 1e-6
NEG_INF = -1e30
NSA_SCALE = NSA_HEAD_DIM ** -0.5
LOG2E = math.log2(math.e)

COL_QKV = 0
COL_ZA = 3072
COL_GATE_A = 4096
COL_GATE_B = 5120
COL_QB = 6144
COL_ZB = 6656
COL_KV = 7168
COL_SMALL = 7936
N_PACKED = 8064
SMALL_A = 0
SMALL_BT = GDN_HEADS
SMALL_GB = 2 * GDN_HEADS


def _dot(a, b, precision=None):
    return jnp.dot(a, b, precision=precision, preferred_element_type=F32)


def _dot_nt(a, b, precision=None):
    return lax.dot_general(a, b, (((1,), (1,)), ((), ())), precision=precision,
                           preferred_element_type=F32)


def _dot_tn(a, b, precision=None):
    return lax.dot_general(a, b, (((0,), (0,)), ((), ())), precision=precision,
                           preferred_element_type=F32)


def _mm(a, b):
    return jnp.dot(a.astype(BF16), b.astype(BF16), preferred_element_type=F32)


def _silu(x):
    return x * jax.nn.sigmoid(x)


def _iota(shape, dim):
    return lax.broadcasted_iota(jnp.int32, shape, dim)


def _proj_kernel(x_ref, g_ref, w_ref, o_ref, h_ref):
    @pl.when(pl.program_id(1) == 0)
    def _():
        x = x_ref[...]
        y = x * lax.rsqrt(jnp.mean(x * x, axis=-1, keepdims=True) + RMS_EPS)
        h_ref[...] = (y * g_ref[...]).astype(BF16)

    o_ref[...] = jnp.dot(h_ref[...], w_ref[...], preferred_element_type=F32)


def _proj(x2d, gain, w_packed, tm, tn):
    rows, d = x2d.shape
    n = w_packed.shape[1]
    return pl.pallas_call(
        _proj_kernel,
        grid=(rows // tm, n // tn),
        in_specs=[
            pl.BlockSpec((tm, d), lambda i, j: (i, 0)),
            pl.BlockSpec((1, d), lambda i, j: (0, 0)),
            pl.BlockSpec((d, tn), lambda i, j: (0, j)),
        ],
        out_specs=pl.BlockSpec((tm, tn), lambda i, j: (i, j)),
        out_shape=jax.ShapeDtypeStruct((rows, n), F32),
        scratch_shapes=[pltpu.VMEM((tm, d), BF16)],
        compiler_params=pltpu.CompilerParams(
            dimension_semantics=("parallel", "arbitrary"), vmem_limit_bytes=VMEM_LIMIT),
    )(x2d, gain, w_packed)


def _gdn_kernel(q_ref, k_ref, v_ref, z_ref, sm_ref, cw_ref, pre_ref, alog_ref, dtb_ref, gain_ref, s0_ref,
                o_ref, sfin_ref,
                s_sc, tail_sc, q_sc, k_sc, v_sc, gc_sc, b_sc, gct_sc, *, tb, chunk, t_valid):
    t = pl.program_id(1)
    c = chunk
    hd = GDN_HEAD_DIM
    nh = GDN_HEADS

    @pl.when(t == 0)
    def _():
        s_sc[...] = s0_ref[...]
        tail_sc[...] = pre_ref[...]

    for idx, (x_ref, dst) in enumerate(((q_ref, q_sc), (k_ref, k_sc), (v_ref, v_sc))):
        cols = slice(idx * GDN_DIM, (idx + 1) * GDN_DIM)
        x = x_ref[...]
        cw = cw_ref[:, cols]
        last = CONV_WIDTH - 1
        head = jnp.concatenate([tail_sc[:, cols], x[0:SUBLANES]], axis=0)
        tail_sc[:, cols] = x[tb - SUBLANES:tb]
        y = cw[last:last + 1] * head[SUBLANES:2 * SUBLANES]
        for j in range(1, CONV_WIDTH):
            y = y + cw[last - j:last - j + 1] * head[SUBLANES - j:2 * SUBLANES - j]
        if tb > SUBLANES:
            rest = cw[last:last + 1] * x
            for j in range(1, CONV_WIDTH):
                rest = rest + cw[last - j:last - j + 1] * pltpu.roll(x, j, 0)
            y = jnp.concatenate([y, rest[SUBLANES:]], axis=0)
        y = _silu(y)
        if idx == 2:
            dst[...] = y
        else:
            scale = GDN_HEAD_DIM ** -0.5 if idx == 0 else 1.0
            for h in range(nh):
                yh = y[:, h * hd:(h + 1) * hd]
                dst[:, h * hd:(h + 1) * hd] = yh * (lax.rsqrt(jnp.sum(yh * yh, axis=-1, keepdims=True) + 1e-6) * scale)

    sm = sm_ref[...]
    xx = sm + dtb_ref[...]
    softplus = jnp.maximum(xx, 0.0) + jnp.log(1.0 + jnp.exp(-jnp.abs(xx)))
    row = _iota((tb, 1), 0)
    live = t * tb + row < t_valid
    g_small = jnp.where(live, -jnp.exp(alog_ref[...]) * softplus, 0.0)
    beta_small = jnp.where(live, jax.nn.sigmoid(sm), 0.0)
    row_in_chunk = row & (c - 1)
    gc_small = g_small
    shift = 1
    while shift < c:
        gc_small = gc_small + jnp.where(row_in_chunk >= shift, pltpu.roll(gc_small, shift, 0), 0.0)
        shift *= 2
    li = _iota((LANES, GDN_DIM), 0)
    head_of_lane = _iota((LANES, GDN_DIM), 1) >> int(math.log2(hd))

    def spread(x, first_lane):
        onehot = (li == head_of_lane + first_lane).astype(BF16)
        hi = x.astype(BF16)
        r1 = x - hi.astype(F32)
        mid = r1.astype(BF16)
        lo = (r1 - mid.astype(F32)).astype(BF16)
        return _dot(hi, onehot) + _dot(mid, onehot) + _dot(lo, onehot)

    gc_sc[...] = spread(gc_small, SMALL_A)
    b_sc[...] = spread(beta_small, SMALL_BT)
    for r0 in range(0, tb, LANES):
        n = min(LANES, tb - r0)
        part = gc_small[r0:r0 + n]
        if n < LANES:
            part = jnp.concatenate([part, jnp.zeros((LANES - n, LANES), F32)], axis=0)
        part_t = part.T
        for off in range(0, n, c):
            gct_sc[(r0 + off) // c] = part_t[:, off:off + c]

    ri = _iota((c, c), 0)
    ci = _iota((c, c), 1)
    causal = ri >= ci
    strict = ri > ci
    eye_f = (ri == ci).astype(F32)
    n_doubling = int(math.log2(c)) - 1
    gain = gain_ref[...]

    def chunk_body(ck, carry):
        r0 = pl.multiple_of(ck * c, c)
        rows = pl.ds(r0, c)
        heads = range(nh)
        hs = [slice(h * hd, (h + 1) * hd) for h in heads]
        kq = []
        for h in heads:
            kk = k_sc[rows, hs[h]].astype(BF16)
            kq.append(_dot_nt(jnp.concatenate([kk, q_sc[rows, hs[h]].astype(BF16)], axis=0), kk))
        decay = [jnp.where(causal, jnp.exp(gc_sc[rows, hs[h]][:, :c] - gct_sc[ck, h:h + 1, :]), 0.0) for h in heads]
        attn = [kq[h][c:] * decay[h] for h in heads]
        nk = [jnp.where(strict, -(b_sc[rows, hs[h]][:, :c] * kq[h][:c] * decay[h]), 0.0) for h in heads]
        tinv = [eye_f + nk[h] for h in heads]
        for _ in range(n_doubling):
            nk = [_mm(nk[h], nk[h]) for h in heads]
            tinv = [tinv[h] + _mm(tinv[h], nk[h]) for h in heads]
        uw = []
        for h in heads:
            bb = b_sc[rows, hs[h]]
            rhs = jnp.concatenate([v_sc[rows, hs[h]] * bb,
                                   k_sc[rows, hs[h]] * (bb * jnp.exp(gc_sc[rows, hs[h]]))], axis=1)
            uw.append(_mm(tinv[h], rhs))
        ws_qs = []
        for h in heads:
            q_dec = q_sc[rows, hs[h]] * jnp.exp(gc_sc[rows, hs[h]])
            ws_qs.append(_mm(jnp.concatenate([uw[h][:, hd:], q_dec], axis=0), s_sc[h]))
        v_new = [uw[h][:, :hd] - ws_qs[h][:c] for h in heads]
        o = [ws_qs[h][c:] + _mm(attn[h], v_new[h]) for h in heads]
        new_states = []
        for h in heads:
            gc = gc_sc[rows, hs[h]]
            g_last = gc[c - 1:c, :]
            k_dec = k_sc[rows, hs[h]] * jnp.exp(g_last - gc)
            new_states.append(s_sc[h] * jnp.exp(g_last) + _dot_tn(k_dec, v_new[h]))
        for h in heads:
            s_sc[h] = new_states[h]
            on = o[h] * lax.rsqrt(jnp.mean(o[h] * o[h], axis=-1, keepdims=True) + RMS_EPS) * gain
            o_ref[rows, hs[h]] = on * _silu(z_ref[rows, hs[h]])
        return carry

    lax.fori_loop(0, tb // c, chunk_body, 0)

    @pl.when(t == pl.num_programs(1) - 1)
    def _():
        sfin_ref[...] = s_sc[...]


def _gdn(p3, conv_w, prefix, alog, dtb, gain, s0, *, tb, chunk, t_valid):
    b, t, _ = p3.shape
    hd = GDN_HEAD_DIM
    gd = GDN_DIM
    const2 = lambda bi, ti: (0, 0)
    kern = functools.partial(_gdn_kernel, tb=tb, chunk=chunk, t_valid=t_valid)
    return pl.pallas_call(
        kern,
        grid=(b, t // tb),
        in_specs=[
            pl.BlockSpec((None, tb, gd), lambda bi, ti: (bi, ti, COL_QKV // gd)),
            pl.BlockSpec((None, tb, gd), lambda bi, ti: (bi, ti, COL_QKV // gd + 1)),
            pl.BlockSpec((None, tb, gd), lambda bi, ti: (bi, ti, COL_QKV // gd + 2)),
            pl.BlockSpec((None, tb, gd), lambda bi, ti: (bi, ti, COL_ZA // gd)),
            pl.BlockSpec((None, tb, LANES), lambda bi, ti: (bi, ti, COL_SMALL // LANES)),
            pl.BlockSpec((CONV_WIDTH, GDN_QKV_DIM), const2),
            pl.BlockSpec((None, SUBLANES, GDN_QKV_DIM), lambda bi, ti: (bi, 0, 0)),
            pl.BlockSpec((1, LANES), const2),
            pl.BlockSpec((1, LANES), const2),
            pl.BlockSpec((1, hd), const2),
            pl.BlockSpec((None, GDN_HEADS, hd, hd), lambda bi, ti: (bi, 0, 0, 0)),
        ],
        out_specs=[
            pl.BlockSpec((None, tb, gd), lambda bi, ti: (bi, ti, 0)),
            pl.BlockSpec((None, GDN_HEADS, hd, hd), lambda bi, ti: (bi, 0, 0, 0)),
        ],
        out_shape=[
            jax.ShapeDtypeStruct((b, t, gd), F32),
            jax.ShapeDtypeStruct((b, GDN_HEADS, hd, hd), F32),
        ],
        scratch_shapes=[
            pltpu.VMEM((GDN_HEADS, hd, hd), F32),
            pltpu.VMEM((SUBLANES, GDN_QKV_DIM), F32),
            pltpu.VMEM((tb, gd), F32),
            pltpu.VMEM((tb, gd), F32),
            pltpu.VMEM((tb, gd), F32),
            pltpu.VMEM((tb, gd), F32),
            pltpu.VMEM((tb, gd), F32),
            pltpu.VMEM((tb // chunk, LANES, chunk), F32),
        ],
        compiler_params=pltpu.CompilerParams(
            dimension_semantics=("parallel", "arbitrary"), vmem_limit_bytes=VMEM_LIMIT),
    )(p3, p3, p3, p3, p3, conv_w, prefix, alog, dtb, gain, s0)


def _masked_softmax2(s, mask):
    s = jnp.where(mask, s, NEG_INF)
    e = jnp.where(mask, jnp.exp2(s - jnp.max(s, axis=-1, keepdims=True)), 0.0)
    return e / jnp.maximum(jnp.sum(e, axis=-1, keepdims=True), 1e-30)


def _stack_heads(q, g):
    hd = NSA_HEAD_DIM
    parts = [q[:, (NSA_GROUP * g + h) * hd:(NSA_GROUP * g + h + 1) * hd] for h in range(NSA_GROUP)]
    return jnp.concatenate(parts, axis=0) * (NSA_SCALE * LOG2E)


def _tile_rows(x):
    return jnp.concatenate([x] * NSA_GROUP, axis=0)


def _group_importance(p_c, tq):
    imp = p_c[0:tq]
    for h in range(1, NSA_GROUP):
        imp = imp + p_c[h * tq:(h + 1) * tq]
    return imp


def _select_blocks(p_c, pos, tq, n_blocks, nbl):
    m = p_c.shape[1]
    imp = _group_importance(p_c, tq)
    pool = ((_iota((m, nbl), 0) >> int(math.log2(SEL_RATIO))) == _iota((m, nbl), 1)).astype(F32)
    imp_sel = _dot(imp, pool, HI)
    lane = _iota((tq, nbl), 1)
    valid = lane * SEL_BLOCK <= pos
    cur = pos >> SEL_SHIFT
    forced = (lane == 0) | (lane == cur) | (lane == cur - 1)
    score = jnp.where(valid, imp_sel + FORCE_BONUS * forced.astype(F32), -1.0)
    score = jnp.where(lane < n_blocks, score, -2.0)
    rank = jnp.zeros((tq, nbl), F32)
    for j in range(n_blocks):
        sj = score[:, j:j + 1]
        rank = rank + ((sj > score) | ((sj == score) & (lane > j))).astype(F32)
    return (rank < float(min(N_SELECT, n_blocks))).astype(F32)


def _select_blocks_t(p_c, q0, tq, n_blocks, nbl):
    m = p_c.shape[1]
    n_sel = min(N_SELECT, n_blocks)
    nbs = -(-n_blocks // SUBLANES) * SUBLANES
    imp = _group_importance(p_c, tq)
    pool_t = ((_iota((nbs, m), 1) >> int(math.log2(SEL_RATIO))) == _iota((nbs, m), 0)).astype(F32)
    imp_t = _dot_nt(pool_t, imp, HI)
    pos_r = q0 + _iota((1, tq), 1)
    blk = _iota((nbs, tq), 0)
    valid = blk * SEL_BLOCK <= pos_r

    def ranked():
        cur = pos_r >> SEL_SHIFT
        forced = (blk == 0) | (blk == cur) | (blk == cur - 1)
        score = jnp.where(valid, imp_t + FORCE_BONUS * forced.astype(F32), -1.0)
        score = jnp.where(blk < n_blocks, score, -2.0)
        rank = jnp.zeros((nbs, tq), F32)
        for j in range(n_blocks):
            sj = score[j:j + 1, :]
            rank = rank + ((sj > score) | ((sj == score) & (blk > j))).astype(F32)
        return (rank < float(n_sel)).astype(F32)

    sel_t = lax.cond(q0 + tq <= n_sel * SEL_BLOCK, lambda: valid.astype(F32), ranked)
    sel_t = jnp.concatenate([sel_t, jnp.zeros((nbl - nbs, tq), F32)], axis=0)
    return sel_t.T


def _compress_rows(x, pw):
    n = x.shape[0] // CMP_STRIDE
    x3 = x.reshape(n, CMP_STRIDE, LANES)
    lo = jnp.sum(x3 * pw[:CMP_STRIDE][None], axis=1)
    hi = jnp.sum(x3 * pw[CMP_STRIDE:][None], axis=1)
    return lo, hi


def _combine_heads(o_ref, g, gates, zb, o_c, o_s, o_w, tq):
    hd = NSA_HEAD_DIM
    for h in range(NSA_GROUP):
        hh = NSA_GROUP * g + h
        c0 = SMALL_GB + 3 * hh
        rows = slice(h * tq, (h + 1) * tq)
        o_h = (gates[:, c0:c0 + 1] * o_c[rows] + gates[:, c0 + 1:c0 + 2] * o_s[rows]
               + gates[:, c0 + 2:c0 + 3] * o_w[rows])
        o_ref[:, hh * hd:(hh + 1) * hd] = o_h * _silu(zb[:, hh * hd:(hh + 1) * hd])


def _nsa_prompt_kernel(q_ref, zb_ref, sm_ref, kcmp_ref, vcmp_ref, kslc_ref, vslc_ref, kwin_ref, vwin_ref,
                       pk_ref, pv_ref, wk_ref, wv_ref, o_ref, kc_sc, vc_sc, kv_sc, *, t_len, tq, tk, span, nbl):
    i = pl.program_id(1)
    m = t_len // CMP_STRIDE
    n_blocks = t_len // SEL_BLOCK
    hd = NSA_HEAD_DIM

    @pl.when(i == 0)
    def _():
        for src, pw_ref, w_ref, dst in ((kcmp_ref, pk_ref, wk_ref, kc_sc), (vcmp_ref, pv_ref, wv_ref, vc_sc)):
            lo, hi = _compress_rows(src[...], pw_ref[...])
            hi_next = pltpu.roll(hi, m - 1, 0)
            pre = jnp.where(_iota((m, LANES), 0) < m - 1, lo + hi_next, 0.0)
            dst[...] = _dot(pre, w_ref[...], HI)
        for idx, src in enumerate((kslc_ref, vslc_ref, kwin_ref, vwin_ref)):
            kv_sc[idx] = src[...].astype(BF16)

    q0 = i * tq
    pos = q0 + _iota((tq, 1), 0)
    pos4 = _tile_rows(pos)
    q = q_ref[...]
    zb = zb_ref[...]
    gates = jax.nn.sigmoid(sm_ref[...])
    c_end = _iota((1, m), 1) * CMP_STRIDE + (CMP_LEN - 1)
    n_kt = (q0 + tq + tk - 1) // tk
    w_start = pl.multiple_of(jnp.clip(q0 - WINDOW, 0, t_len - span), tq)
    w_dist = pos - (w_start + _iota((1, span), 1))
    w_bias = _tile_rows(jnp.where((w_dist >= 0) & (w_dist <= WINDOW), 0.0, NEG_INF))

    for g in range(NSA_KV_HEADS):
        gs = slice(g * hd, (g + 1) * hd)
        qs = _stack_heads(q, g)
        qs_bf = qs.astype(BF16)
        p_c = _masked_softmax2(_dot_nt(qs, kc_sc[:, gs], HI), c_end <= pos4)
        o_c = _dot(p_c, vc_sc[:, gs])
        sel = _select_blocks_t(p_c, q0, tq, n_blocks, nbl)
        sel_bias = ((sel - 1.0) * -NEG_INF).astype(BF16)

        def sel_body(kt, carry, qs_bf=qs_bf, sel_bias=sel_bias, gs=gs):
            m_run, l_run, acc = carry
            k0 = pl.multiple_of(kt * tk, tk)
            expand = ((k0 + _iota((nbl, tk), 1)) >> SEL_SHIFT) == _iota((nbl, tk), 0)
            bias = _dot(sel_bias, expand.astype(BF16)) + jnp.where(k0 + _iota((1, tk), 1) <= pos, 0.0, NEG_INF)
            s = _dot_nt(qs_bf, kv_sc[0, pl.ds(k0, tk), gs]) + _tile_rows(bias)
            m_new = jnp.maximum(m_run, jnp.max(s, axis=-1, keepdims=True))
            alpha = jnp.exp2(m_run - m_new)
            e = jnp.exp2(s - m_new)
            l_new = alpha * l_run + jnp.sum(e, axis=-1, keepdims=True)
            acc_new = alpha * acc + _dot(e.astype(BF16), kv_sc[1, pl.ds(k0, tk), gs])
            return m_new, l_new, acc_new

        init = (jnp.full((NSA_GROUP * tq, 1), NEG_INF, F32), jnp.zeros((NSA_GROUP * tq, 1), F32),
                jnp.zeros((NSA_GROUP * tq, hd), F32))
        _, l_s, acc_s = lax.fori_loop(0, n_kt, sel_body, init)
        o_s = acc_s / jnp.maximum(l_s, 1e-30)

        s_w = _dot_nt(qs_bf, kv_sc[2, pl.ds(w_start, span), gs]) + w_bias
        e_w = jnp.exp2(s_w - jnp.max(s_w, axis=-1, keepdims=True))
        o_w = (_dot(e_w.astype(BF16), kv_sc[3, pl.ds(w_start, span), gs])
               / jnp.maximum(jnp.sum(e_w, axis=-1, keepdims=True), 1e-30))
        _combine_heads(o_ref, g, gates, zb, o_c, o_s, o_w, tq)


def _nsa_prompt(p3, pos_k, pos_v, w_k, w_v, *, tq, tk):
    b, t, _ = p3.shape
    span = WINDOW + tq
    assert t % tq == 0 and t % tk == 0 and t >= span and t % SEL_BLOCK == 0 and tq % LANES == 0
    assert WINDOW % tq == 0
    m = t // CMP_STRIDE
    nbl = -(-(t // SEL_BLOCK) // LANES) * LANES
    kvc = COL_KV // LANES

    def full(c):
        return pl.BlockSpec((None, t, LANES), lambda bi, i: (bi, 0, c))

    const2 = lambda bi, i: (0, 0)
    kern = functools.partial(_nsa_prompt_kernel, t_len=t, tq=tq, tk=tk, span=span, nbl=nbl)
    return pl.pallas_call(
        kern,
        grid=(b, t // tq),
        in_specs=[
            pl.BlockSpec((None, tq, NSA_DIM), lambda bi, i: (bi, i, COL_QB // NSA_DIM)),
            pl.BlockSpec((None, tq, NSA_DIM), lambda bi, i: (bi, i, COL_ZB // NSA_DIM)),
            pl.BlockSpec((None, tq, LANES), lambda bi, i: (bi, i, COL_SMALL // LANES)),
            full(kvc), full(kvc + 1), full(kvc + 2), full(kvc + 3), full(kvc + 4), full(kvc + 5),
            pl.BlockSpec((CMP_LEN, LANES), const2),
            pl.BlockSpec((CMP_LEN, LANES), const2),
            pl.BlockSpec((LANES, LANES), const2),
            pl.BlockSpec((LANES, LANES), const2),
        ],
        out_specs=pl.BlockSpec((None, tq, NSA_DIM), lambda bi, i: (bi, i, 0)),
        out_shape=jax.ShapeDtypeStruct((b, t, NSA_DIM), F32),
        scratch_shapes=[pltpu.VMEM((m, LANES), F32), pltpu.VMEM((m, LANES), F32),
                        pltpu.VMEM((4, t, LANES), BF16)],
        compiler_params=pltpu.CompilerParams(
            dimension_semantics=("parallel", "arbitrary"), vmem_limit_bytes=VMEM_LIMIT),
    )(p3, p3, p3, p3, p3, p3, p3, p3, p3, pos_k, pos_v, w_k, w_v)


def _cmp_paged_kernel(pt_ref, *refs, k_pages):
    del pt_ref
    kp = refs[:k_pages]
    kn = refs[k_pages]
    vp = refs[k_pages + 1:2 * k_pages + 1]
    vn = refs[2 * k_pages + 1]
    plk_ref, phk_ref, plv_ref, phv_ref, seg_ref, wk_ref, wv_ref, kc_ref, vc_ref = refs[2 * k_pages + 2:]
    seg = seg_ref[...]
    lane = _iota((LANES, LANES), 1)

    def chunk_sums(ys):
        y = jnp.concatenate(ys, axis=1)
        hi = y.astype(BF16)
        mid = (y - hi.astype(F32)).astype(BF16)
        return _dot(hi, seg) + _dot(mid, seg)

    for pages, nxt, pl_ref, ph_ref, w_ref, out in ((kp, kn, plk_ref, phk_ref, wk_ref, kc_ref),
                                                   (vp, vn, plv_ref, phv_ref, wv_ref, vc_ref)):
        p_lo = pl_ref[...]
        p_hi = ph_ref[...]
        xs = [r[...].reshape(LANES, LANES) for r in pages]
        lo = chunk_sums([x * p_lo for x in xs])
        hi = chunk_sums([x * p_hi for x in xs])
        x_next = nxt[...].reshape(LANES, LANES)
        hi_head = jnp.sum(jnp.where(lane < CMP_STRIDE, x_next * p_hi, 0.0), axis=1, keepdims=True)
        hi_next = jnp.where(lane == LANES - 1, hi_head, pltpu.roll(hi, LANES - 1, 1))
        out[...] = _dot(w_ref[...], lo + hi_next, HI)


def _cmp_paged(pool_k, pool_v, page_table, pos_k, pos_v, w_k, w_v, *, k_pages):
    b, n_pages = page_table.shape
    page = pool_k.shape[3]
    per_step = k_pages * page // CMP_STRIDE
    assert page == LANES and n_pages % k_pages == 0 and per_step == LANES
    blk = (None, NSA_KV_HEADS, NSA_HEAD_DIM, page)

    def page_spec(i):
        return pl.BlockSpec(blk, lambda bi, j, pt: (pt[bi, j * k_pages + i], 0, 0, 0))

    next_spec = pl.BlockSpec(
        blk, lambda bi, j, pt: (pt[bi, jnp.minimum((j + 1) * k_pages, n_pages - 1)], 0, 0, 0))
    const2 = lambda bi, j, pt: (0, 0)

    def pos_tile(p):
        return jnp.tile(p.T, (NSA_KV_HEADS, page // CMP_STRIDE))

    keys = jnp.arange(k_pages * page) // CMP_STRIDE
    seg = (keys[:, None] == jnp.arange(per_step)[None, :]).astype(BF16)
    in_specs = ([page_spec(i) for i in range(k_pages)] + [next_spec]
                + [page_spec(i) for i in range(k_pages)] + [next_spec]
                + [pl.BlockSpec((LANES, LANES), const2)] * 4
                + [pl.BlockSpec((k_pages * page, per_step), const2)]
                + [pl.BlockSpec((LANES, LANES), const2)] * 2)
    out_spec = pl.BlockSpec((None, LANES, per_step), lambda bi, j, pt: (bi, 0, j))
    m = n_pages * page // CMP_STRIDE
    return pl.pallas_call(
        functools.partial(_cmp_paged_kernel, k_pages=k_pages),
        grid_spec=pltpu.PrefetchScalarGridSpec(
            num_scalar_prefetch=1, grid=(b, n_pages // k_pages), in_specs=in_specs,
            out_specs=[out_spec, out_spec]),
        out_shape=[jax.ShapeDtypeStruct((b, LANES, m), F32)] * 2,
        compiler_params=pltpu.CompilerParams(
            dimension_semantics=("parallel", "arbitrary"), vmem_limit_bytes=VMEM_LIMIT),
    )(page_table, *([pool_k] * (k_pages + 1)), *([pool_v] * (k_pages + 1)),
      pos_tile(pos_k[:CMP_STRIDE]), pos_tile(pos_k[CMP_STRIDE:]), pos_tile(pos_v[:CMP_STRIDE]),
      pos_tile(pos_v[CMP_STRIDE:]), seg, _block_diag2(w_k).T, _block_diag2(w_v).T)


def _new_key_scores(qs, new_k, r, gs):
    return jnp.sum(qs * new_k[r:r + 1, gs], axis=-1, keepdims=True)


def _nsa_sample_a_kernel(q_ref, kc_ref, vc_ref, bk_ref, bv_ref, nk_ref, nv_ref,
                         oc_ref, ow_ref, sel_ref, wk_out, wv_out, *, past, t_valid, tp, n_blocks, nbl):
    hd = NSA_HEAD_DIM
    m = kc_ref.shape[1]
    w_buf = bk_ref.shape[0]
    pos = past + _iota((tp, 1), 0)
    pos4 = _tile_rows(pos)
    q = q_ref[...]
    new_k = nk_ref[...]
    new_v = nv_ref[...]
    c_end = _iota((1, m), 1) * CMP_STRIDE + (CMP_LEN - 1)
    buf_pos = (past - w_buf) + _iota((1, w_buf), 1)
    buf_dist = pos4 - buf_pos
    buf_mask = (buf_dist >= 0) & (buf_dist <= WINDOW)
    for g in range(NSA_KV_HEADS):
        gs = slice(g * hd, (g + 1) * hd)
        qs = _stack_heads(q, g)
        p_c = _masked_softmax2(_dot(qs, kc_ref[gs, :], HI), c_end <= pos4)
        oc_ref[g] = _dot_nt(p_c, vc_ref[gs, :])
        sel_ref[g] = _select_blocks(p_c, pos, tp, n_blocks, nbl)
        s_buf = jnp.where(buf_mask, _dot_nt(qs, bk_ref[:, gs]), NEG_INF)
        s_new, new_mask = [], []
        for r in range(t_valid):
            dist = pos4 - (past + r)
            mk = (dist >= 0) & (dist <= WINDOW)
            new_mask.append(mk)
            s_new.append(jnp.where(mk, _new_key_scores(qs, new_k, r, gs), NEG_INF))
        m_all = jnp.max(s_buf, axis=-1, keepdims=True)
        for sn in s_new:
            m_all = jnp.maximum(m_all, sn)
        e_buf = jnp.where(buf_mask, jnp.exp2(s_buf - m_all), 0.0)
        l_all = jnp.sum(e_buf, axis=-1, keepdims=True)
        acc = _dot(e_buf, bv_ref[:, gs])
        for r in range(t_valid):
            e_r = jnp.where(new_mask[r], jnp.exp2(s_new[r] - m_all), 0.0)
            l_all = l_all + e_r
            acc = acc + e_r * new_v[r:r + 1, gs]
        ow_ref[g] = acc / jnp.maximum(l_all, 1e-30)
    wk_out[0:w_buf - t_valid, :] = bk_ref[t_valid:w_buf, :]
    wk_out[w_buf - t_valid:w_buf, :] = new_k[0:t_valid]
    wv_out[0:w_buf - t_valid, :] = bv_ref[t_valid:w_buf, :]
    wv_out[w_buf - t_valid:w_buf, :] = new_v[0:t_valid]


def _nsa_sample_a(p3, kc, vc, buf_k, buf_v, *, past, t_valid, n_blocks, nbl):
    b, tp, _ = p3.shape
    m = kc.shape[2]
    w_buf = buf_k.shape[1]
    assert w_buf == WINDOW and t_valid <= w_buf
    kvc = COL_KV // LANES
    rows = NSA_GROUP * tp
    kern = functools.partial(_nsa_sample_a_kernel, past=past, t_valid=t_valid, tp=tp,
                             n_blocks=n_blocks, nbl=nbl)
    per_b3 = lambda bi: (bi, 0, 0)
    per_b4 = lambda bi: (bi, 0, 0, 0)
    return pl.pallas_call(
        kern,
        grid=(b,),
        in_specs=[
            pl.BlockSpec((None, tp, NSA_DIM), lambda bi: (bi, 0, COL_QB // NSA_DIM)),
            pl.BlockSpec((None, LANES, m), per_b3),
            pl.BlockSpec((None, LANES, m), per_b3),
            pl.BlockSpec((None, w_buf, LANES), per_b3),
            pl.BlockSpec((None, w_buf, LANES), per_b3),
            pl.BlockSpec((None, tp, LANES), lambda bi: (bi, 0, kvc + 4)),
            pl.BlockSpec((None, tp, LANES), lambda bi: (bi, 0, kvc + 5)),
        ],
        out_specs=[
            pl.BlockSpec((None, NSA_KV_HEADS, rows, NSA_HEAD_DIM), per_b4),
            pl.BlockSpec((None, NSA_KV_HEADS, rows, NSA_HEAD_DIM), per_b4),
            pl.BlockSpec((None, NSA_KV_HEADS, tp, nbl), per_b4),
            pl.BlockSpec((None, w_buf, LANES), per_b3),
            pl.BlockSpec((None, w_buf, LANES), per_b3),
        ],
        out_shape=[
            jax.ShapeDtypeStruct((b, NSA_KV_HEADS, rows, NSA_HEAD_DIM), F32),
            jax.ShapeDtypeStruct((b, NSA_KV_HEADS, rows, NSA_HEAD_DIM), F32),
            jax.ShapeDtypeStruct((b, NSA_KV_HEADS, tp, nbl), F32),
            jax.ShapeDtypeStruct((b, w_buf, LANES), F32),
            jax.ShapeDtypeStruct((b, w_buf, LANES), F32),
        ],
        compiler_params=pltpu.CompilerParams(
            dimension_semantics=("parallel",), vmem_limit_bytes=VMEM_LIMIT),
    )(p3, kc, vc, buf_k, buf_v, p3, p3)


def _nsa_sample_b_kernel(pt_ref, q_ref, zb_ref, sm_ref, nk_ref, nv_ref, sel_ref, selj_ref, exp_ref,
                         oc_ref, ow_ref, *refs, k_pages, past, t_valid, tp, n_blocks, nbl):
    del pt_ref
    kp = refs[:k_pages]
    vp = refs[k_pages:2 * k_pages]
    o_ref, m_sc, l_sc, acc_sc = refs[2 * k_pages:]
    hd = NSA_HEAD_DIM
    j = pl.program_id(1)

    @pl.when(j == 0)
    def _():
        m_sc[...] = jnp.full(m_sc.shape, NEG_INF, F32)
        l_sc[...] = jnp.zeros(l_sc.shape, F32)
        acc_sc[...] = jnp.zeros(acc_sc.shape, F32)

    q = q_ref[...]
    expand = exp_ref[...]
    for g in range(NSA_KV_HEADS):
        qs = _stack_heads(q, g).astype(BF16)
        k_t = jnp.concatenate([r[g] for r in kp], axis=1).astype(BF16)
        v_t = jnp.concatenate([r[g] for r in vp], axis=1).astype(BF16)
        sel_bias = ((selj_ref[g] - 1.0) * -NEG_INF).astype(BF16)
        s = _dot(qs, k_t) + _tile_rows(_dot(sel_bias, expand))
        m_run = m_sc[g]
        m_new = jnp.maximum(m_run, jnp.max(s, axis=-1, keepdims=True))
        alpha = jnp.exp2(m_run - m_new)
        e = jnp.exp2(s - m_new)
        l_sc[g] = alpha * l_sc[g] + jnp.sum(e, axis=-1, keepdims=True)
        acc_sc[g] = alpha * acc_sc[g] + _dot_nt(e.astype(BF16), v_t)
        m_sc[g] = m_new

    @pl.when(j == pl.num_programs(1) - 1)
    def _():
        pos4 = _tile_rows(past + _iota((tp, 1), 0))
        gates = jax.nn.sigmoid(sm_ref[...])
        zb = zb_ref[...]
        new_k = nk_ref[...]
        new_v = nv_ref[...]
        for g in range(NSA_KV_HEADS):
            gs = slice(g * hd, (g + 1) * hd)
            qs = _stack_heads(q, g)
            picked = _tile_rows(sel_ref[g][:, n_blocks - 1:n_blocks] > 0.5)
            m_run, l_run, acc = m_sc[g], l_sc[g], acc_sc[g]
            s_new, new_mask = [], []
            for r in range(t_valid):
                mk = picked & (past + r <= pos4)
                new_mask.append(mk)
                s_new.append(jnp.where(mk, _new_key_scores(qs, new_k, r, gs), NEG_INF))
            m_all = m_run
            for sn in s_new:
                m_all = jnp.maximum(m_all, sn)
            alpha = jnp.exp2(m_run - m_all)
            l_all = alpha * l_run
            acc = alpha * acc
            for r in range(t_valid):
                e_r = jnp.where(new_mask[r], jnp.exp2(s_new[r] - m_all), 0.0)
                l_all = l_all + e_r
                acc = acc + e_r * new_v[r:r + 1, gs]
            o_s = acc / jnp.maximum(l_all, 1e-30)
            _combine_heads(o_ref, g, gates, zb, oc_ref[g], o_s, ow_ref[g], tp)


def _nsa_sample_b(p3, sel, o_c, o_w, pool_k, pool_v, page_table, *, k_pages, past, t_valid, n_blocks, nbl):
    b, tp, _ = p3.shape
    n_pages = page_table.shape[1]
    page = pool_k.shape[3]
    n_steps = n_pages // k_pages
    n_keys = k_pages * page
    per_step = n_keys // SEL_BLOCK
    assert n_pages % k_pages == 0 and page % SEL_BLOCK == 0 and past == n_pages * page
    assert n_blocks == past // SEL_BLOCK + 1 and t_valid <= SEL_BLOCK and per_step <= LANES
    kvc = COL_KV // LANES
    rows = NSA_GROUP * tp
    sel_steps = sel[..., :n_steps * per_step].reshape(b, NSA_KV_HEADS, tp, n_steps, per_step)
    sel_steps = jnp.pad(sel_steps.transpose(0, 3, 1, 2, 4), ((0, 0),) * 4 + ((0, LANES - per_step),))
    expand = (jnp.arange(LANES)[:, None] == (jnp.arange(n_keys) // SEL_BLOCK)[None, :]).astype(BF16)
    blk = (None, NSA_KV_HEADS, NSA_HEAD_DIM, page)

    def page_spec(i):
        return pl.BlockSpec(blk, lambda bi, j, pt: (pt[bi, j * k_pages + i], 0, 0, 0))

    per_b4 = lambda bi, j, pt: (bi, 0, 0, 0)
    in_specs = [
        pl.BlockSpec((None, tp, NSA_DIM), lambda bi, j, pt: (bi, 0, COL_QB // NSA_DIM)),
        pl.BlockSpec((None, tp, NSA_DIM), lambda bi, j, pt: (bi, 0, COL_ZB // NSA_DIM)),
        pl.BlockSpec((None, tp, LANES), lambda bi, j, pt: (bi, 0, COL_SMALL // LANES)),
        pl.BlockSpec((None, tp, LANES), lambda bi, j, pt: (bi, 0, kvc + 2)),
        pl.BlockSpec((None, tp, LANES), lambda bi, j, pt: (bi, 0, kvc + 3)),
        pl.BlockSpec((None, NSA_KV_HEADS, tp, nbl), per_b4),
        pl.BlockSpec((None, None, NSA_KV_HEADS, tp, LANES), lambda bi, j, pt: (bi, j, 0, 0, 0)),
        pl.BlockSpec((LANES, n_keys), lambda bi, j, pt: (0, 0)),
        pl.BlockSpec((None, NSA_KV_HEADS, rows, NSA_HEAD_DIM), per_b4),
        pl.BlockSpec((None, NSA_KV_HEADS, rows, NSA_HEAD_DIM), per_b4),
    ] + [page_spec(i) for i in range(k_pages)] * 2
    kern = functools.partial(_nsa_sample_b_kernel, k_pages=k_pages, past=past, t_valid=t_valid, tp=tp,
                             n_blocks=n_blocks, nbl=nbl)
    return pl.pallas_call(
        kern,
        grid_spec=pltpu.PrefetchScalarGridSpec(
            num_scalar_prefetch=1, grid=(b, n_steps), in_specs=in_specs,
            out_specs=pl.BlockSpec((None, tp, NSA_DIM), lambda bi, j, pt: (bi, 0, 0)),
            scratch_shapes=[pltpu.VMEM((NSA_KV_HEADS, rows, 1), F32), pltpu.VMEM((NSA_KV_HEADS, rows, 1), F32),
                            pltpu.VMEM((NSA_KV_HEADS, rows, NSA_HEAD_DIM), F32)]),
        out_shape=jax.ShapeDtypeStruct((b, tp, NSA_DIM), F32),
        compiler_params=pltpu.CompilerParams(
            dimension_semantics=("parallel", "arbitrary"), vmem_limit_bytes=VMEM_LIMIT),
    )(page_table, p3, p3, p3, p3, p3, sel, sel_steps, expand, o_c, o_w,
      *([pool_k] * k_pages), *([pool_v] * k_pages))


def _merge_kernel(x_ref, oa_ref, ob_ref, ga_ref, gb_ref, wa_ref, wb_ref, wo_ref, fg_ref, y_ref):
    mixed = (jax.nn.sigmoid(ga_ref[...]) * _dot(oa_ref[...].astype(BF16), wa_ref[...])
             + jax.nn.sigmoid(gb_ref[...]) * _dot(ob_ref[...].astype(BF16), wb_ref[...]))
    x_new = x_ref[...] + _dot(mixed.astype(BF16), wo_ref[...])
    y = x_new * lax.rsqrt(jnp.mean(x_new * x_new, axis=-1, keepdims=True) + RMS_EPS)
    y_ref[...] = y * fg_ref[...]


def _merge(x2d, o_a, o_b, p2d, w_a, w_b, w_o, final_gain, tm):
    rows, d = x2d.shape
    const2 = lambda i: (0, 0)
    return pl.pallas_call(
        _merge_kernel,
        grid=(rows // tm,),
        in_specs=[
            pl.BlockSpec((tm, d), lambda i: (i, 0)),
            pl.BlockSpec((tm, GDN_DIM), lambda i: (i, 0)),
            pl.BlockSpec((tm, NSA_DIM), lambda i: (i, 0)),
            pl.BlockSpec((tm, d), lambda i: (i, COL_GATE_A // D_MODEL)),
            pl.BlockSpec((tm, d), lambda i: (i, COL_GATE_B // D_MODEL)),
            pl.BlockSpec((GDN_DIM, d), const2),
            pl.BlockSpec((NSA_DIM, d), const2),
            pl.BlockSpec((d, d), const2),
            pl.BlockSpec((1, d), const2),
        ],
        out_specs=pl.BlockSpec((tm, d), lambda i: (i, 0)),
        out_shape=jax.ShapeDtypeStruct((rows, d), F32),
        compiler_params=pltpu.CompilerParams(
            dimension_semantics=("parallel",), vmem_limit_bytes=VMEM_LIMIT),
    )(x2d, o_a, o_b, p2d, p2d, w_a, w_b, w_o, final_gain)


def _pack_w_in(w_in):
    bounds = [0]
    for width in (GDN_QKV_DIM, GDN_HEADS, GDN_HEADS, GDN_DIM, NSA_DIM, 6 * NSA_KV_DIM, 3 * NSA_HEADS,
                  NSA_DIM, 2 * D_MODEL):
        bounds.append(bounds[-1] + width)
    qkv, a, bt, z_a, q_b, kv_b, g_b, z_b, merge = (w_in[:, bounds[i]:bounds[i + 1]] for i in range(9))
    small = jnp.concatenate([a, bt, g_b], axis=1)
    small = jnp.pad(small, ((0, 0), (0, LANES - small.shape[1])))
    packed = jnp.concatenate([qkv, z_a, merge, q_b, z_b, kv_b, small], axis=1)
    assert packed.shape[1] == N_PACKED
    return packed.astype(BF16)


def _lane_vec(v):
    return jnp.pad(v.astype(F32), (0, LANES - v.shape[0])).reshape(1, LANES)


def _block_diag2(w):
    z = jnp.zeros_like(w)
    return jnp.concatenate([jnp.concatenate([w, z], axis=1), jnp.concatenate([z, w], axis=1)], axis=0)


def _tile_for(n, candidates):
    for c in candidates:
        if n % c == 0:
            return c
    raise ValueError(f"no tile for {n}")


def kernel(x_prompt, x_sample, cache_k_cmp, cache_v_cmp, cache_k_slc, cache_v_slc, state_win_k, state_win_v,
           state_conv, state_gdn, page_table, norm_gain, w_in, conv_w, a_log, dt_bias, gdn_gain, cmp_pos_k,
           cmp_w_k, cmp_pos_v, cmp_w_v, w_branch_a, w_branch_b, w_out, final_gain):
    assert w_in.shape[0] == 1, "one layer"
    bp, tp_len, d = x_prompt.shape
    bs, ts, _ = x_sample.shape
    n_pool, page = cache_k_cmp.shape[1:3]
    n_pages = page_table.shape[1]
    past = n_pages * page
    assert past % CMP_STRIDE == 0 and ts < CMP_STRIDE and ts >= CONV_WIDTH - 1 and ts <= SUBLANES
    t_pad = SUBLANES

    w_packed = _pack_w_in(w_in[0])
    gain_in = norm_gain[0].reshape(1, d)
    fgain = final_gain.reshape(1, d)
    w_a = w_branch_a[0].astype(BF16)
    w_b = w_branch_b[0].astype(BF16)
    w_o = w_out[0].astype(BF16)
    cw = conv_w[0]
    alog = _lane_vec(a_log[0])
    dtb = _lane_vec(dt_bias[0])
    ggain = gdn_gain[0].reshape(1, GDN_HEAD_DIM)
    pos_k = jnp.tile(cmp_pos_k[0], (1, NSA_KV_HEADS))
    pos_v = jnp.tile(cmp_pos_v[0], (1, NSA_KV_HEADS))
    wk2 = _block_diag2(cmp_w_k[0])
    wv2 = _block_diag2(cmp_w_v[0])
    tn = _tile_for(N_PACKED, (1152, 896, 128))
    kvc = COL_KV

    rows_p = bp * tp_len
    x2p = x_prompt.reshape(rows_p, d)
    p2 = _proj(x2p, gain_in, w_packed, _tile_for(rows_p, (1024, 512, 256, 128)), tn)
    p3 = p2.reshape(bp, tp_len, N_PACKED)
    tb = _tile_for(tp_len, (256, 128, 64))
    o_a, gdn_p = _gdn(p3, cw, jnp.zeros((bp, SUBLANES, GDN_QKV_DIM), F32), alog, dtb, ggain,
                      jnp.zeros((bp, GDN_HEADS, GDN_HEAD_DIM, GDN_HEAD_DIM), F32),
                      tb=tb, chunk=GDN_CHUNK, t_valid=tp_len)
    o_b = _nsa_prompt(p3, pos_k, pos_v, wk2, wv2, tq=256, tk=_tile_for(tp_len, (512, 256, 128)))
    y_p = _merge(x2p, o_a.reshape(rows_p, GDN_DIM), o_b.reshape(rows_p, NSA_DIM), p2, w_a, w_b, w_o, fgain,
                 _tile_for(rows_p, (512, 256, 128)))
    y_prompt = y_p.reshape(bp, tp_len, d)

    def kv_out(p, i, t_keep_from, t_to):
        return p[:, t_keep_from:t_to, kvc + i * LANES:kvc + (i + 1) * LANES].reshape(
            1, p.shape[0], t_to - t_keep_from, NSA_KV_HEADS, NSA_HEAD_DIM)

    keep_p = min(WINDOW, tp_len)
    prompt_state = ([kv_out(p3, i, 0, tp_len) for i in range(4)]
                    + [kv_out(p3, i, tp_len - keep_p, tp_len) for i in (4, 5)]
                    + [p3[:, tp_len - (CONV_WIDTH - 1):, :GDN_QKV_DIM][None], gdn_p[None]])

    xs_pad = jnp.pad(x_sample, ((0, 0), (0, t_pad - ts), (0, 0)))
    rows_s = bs * t_pad
    x2s = xs_pad.reshape(rows_s, d)
    ps2 = _proj(x2s, gain_in, w_packed, _tile_for(rows_s, (512, 256, 128, 8)), tn)
    ps3 = ps2.reshape(bs, t_pad, N_PACKED)
    prefix_s = jnp.pad(state_conv[0], ((0, 0), (SUBLANES - (CONV_WIDTH - 1), 0), (0, 0)))
    o_a_s, gdn_s = _gdn(ps3, cw, prefix_s, alog, dtb, ggain, state_gdn[0], tb=t_pad, chunk=t_pad, t_valid=ts)

    pools = [c[0].transpose(0, 2, 3, 1) for c in (cache_k_cmp, cache_v_cmp, cache_k_slc, cache_v_slc)]
    kc_s, vc_s = _cmp_paged(pools[0], pools[1], page_table, cmp_pos_k[0], cmp_pos_v[0], cmp_w_k[0], cmp_w_v[0],
                            k_pages=LANES * CMP_STRIDE // page)
    n_blocks = -(-(past + ts) // SEL_BLOCK)
    nbl = -(-n_blocks // LANES) * LANES
    w_buf = state_win_k.shape[2]
    o_c_s, o_w_s, sel_s, win_k_s, win_v_s = _nsa_sample_a(
        ps3, kc_s, vc_s, state_win_k[0].reshape(bs, w_buf, NSA_KV_DIM), state_win_v[0].reshape(bs, w_buf, NSA_KV_DIM),
        past=past, t_valid=ts, n_blocks=n_blocks, nbl=nbl)
    o_b_s = _nsa_sample_b(ps3, sel_s, o_c_s, o_w_s, pools[2], pools[3], page_table,
                          k_pages=_tile_for(n_pages, (16, 8, 4, 2, 1)),
                          past=past, t_valid=ts, n_blocks=n_blocks, nbl=nbl)
    y_s = _merge(x2s, o_a_s.reshape(rows_s, GDN_DIM), o_b_s.reshape(rows_s, NSA_DIM), ps2, w_a, w_b, w_o, fgain,
                 _tile_for(rows_s, (512, 256, 128, 8)))
    y_sample = y_s.reshape(bs, t_pad, d)[:, :ts]

    keep_s = min(WINDOW, w_buf + ts)
    assert keep_s == w_buf
    sample_state = ([kv_out(ps3, i, 0, ts) for i in range(4)]
                    + [win_k_s.reshape(1, bs, keep_s, NSA_KV_HEADS, NSA_HEAD_DIM),
                       win_v_s.reshape(1, bs, keep_s, NSA_KV_HEADS, NSA_HEAD_DIM),
                       ps3[:, ts - (CONV_WIDTH - 1):ts, :GDN_QKV_DIM][None], gdn_s[None]])
    return (y_prompt, y_sample, *prompt_state, *sample_state)
```

```python
import functools
import math

import jax
import jax.numpy as jnp
from jax import lax
from jax.experimental import pallas as pl
from jax.experimental.pallas import tpu as pltpu

F32 = jnp.float32
BF16 = jnp.bfloat16
HI = lax.Precision.HIGHEST

LANES = 128
SUBLANES = 8
VMEM_LIMIT = 56 * 1024 * 1024

D_MODEL = 1024
GDN_HEADS = 8
GDN_HEAD_DIM = 128
GDN_DIM = GDN_HEADS * GDN_HEAD_DIM
GDN_QKV_DIM = 3 * GDN_DIM
CONV_WIDTH = 4
GDN_CHUNK = 64
NSA_HEADS = 8
NSA_KV_HEADS = 2
NSA_GROUP = NSA_HEADS // NSA_KV_HEADS
NSA_HEAD_DIM = 64
NSA_DIM = NSA_HEADS * NSA_HEAD_DIM
NSA_KV_DIM = NSA_KV_HEADS * NSA_HEAD_DIM
CMP_STRIDE = 16
CMP_LEN = 2 * CMP_STRIDE
SEL_BLOCK = 64
SEL_SHIFT = 6
SEL_RATIO = SEL_BLOCK // CMP_STRIDE
N_SELECT = 16
WINDOW = 512
FORCE_BONUS = 100.0
RMS_EPS = 1e-6
NEG_INF = -1e30
NSA_SCALE = NSA_HEAD_DIM ** -0.5
LOG2E = math.log2(math.e)

COL_QKV = 0
COL_ZA = 3072
COL_GATE_A = 4096
COL_GATE_B = 5120
COL_QB = 6144
COL_ZB = 6656
COL_KV = 7168
COL_SMALL = 7936
N_PACKED = 8064
SMALL_A = 0
SMALL_BT = GDN_HEADS
SMALL_GB = 2 * GDN_HEADS


def _dot(a, b, precision=None):
    return jnp.dot(a, b, precision=precision, preferred_element_type=F32)


def _dot_nt(a, b, precision=None):
    return lax.dot_general(a, b, (((1,), (1,)), ((), ())), precision=precision,
                           preferred_element_type=F32)


def _dot_tn(a, b, precision=None):
    return lax.dot_general(a, b, (((0,), (0,)), ((), ())), precision=precision,
                           preferred_element_type=F32)


def _mm(a, b):
    return jnp.dot(a.astype(BF16), b.astype(BF16), preferred_element_type=F32)


def _silu(x):
    return x * jax.nn.sigmoid(x)


def _iota(shape, dim):
    return lax.broadcasted_iota(jnp.int32, shape, dim)


def _proj_kernel(x_ref, g_ref, w_ref, o_ref, h_ref):
    @pl.when(pl.program_id(1) == 0)
    def _():
        x = x_ref[...]
        y = x * lax.rsqrt(jnp.mean(x * x, axis=-1, keepdims=True) + RMS_EPS)
        h_ref[...] = (y * g_ref[...]).astype(BF16)

    o_ref[...] = jnp.dot(h_ref[...], w_ref[...], preferred_element_type=F32)


def _proj(x2d, gain, w_packed, tm, tn):
    rows, d = x2d.shape
    n = w_packed.shape[1]
    return pl.pallas_call(
        _proj_kernel,
        grid=(rows // tm, n // tn),
        in_specs=[
            pl.BlockSpec((tm, d), lambda i, j: (i, 0)),
            pl.BlockSpec((1, d), lambda i, j: (0, 0)),
            pl.BlockSpec((d, tn), lambda i, j: (0, j)),
        ],
        out_specs=pl.BlockSpec((tm, tn), lambda i, j: (i, j)),
        out_shape=jax.ShapeDtypeStruct((rows, n), F32),
        scratch_shapes=[pltpu.VMEM((tm, d), BF16)],
        compiler_params=pltpu.CompilerParams(
            dimension_semantics=("parallel", "arbitrary"), vmem_limit_bytes=VMEM_LIMIT),
    )(x2d, gain, w_packed)


def _gdn_kernel(q_ref, k_ref, v_ref, z_ref, sm_ref, cw_ref, pre_ref, alog_ref, dtb_ref, gain_ref, s0_ref,
                o_ref, sfin_ref,
                s_sc, tail_sc, q_sc, k_sc, v_sc, gc_sc, b_sc, gct_sc, *, tb, chunk, t_valid):
    t = pl.program_id(1)
    c = chunk
    hd = GDN_HEAD_DIM
    nh = GDN_HEADS

    @pl.when(t == 0)
    def _():
        s_sc[...] = s0_ref[...]
        tail_sc[...] = pre_ref[...]

    for idx, (x_ref, dst) in enumerate(((q_ref, q_sc), (k_ref, k_sc), (v_ref, v_sc))):
        cols = slice(idx * GDN_DIM, (idx + 1) * GDN_DIM)
        x = x_ref[...]
        cw = cw_ref[:, cols]
        last = CONV_WIDTH - 1
        head = jnp.concatenate([tail_sc[:, cols], x[0:SUBLANES]], axis=0)
        tail_sc[:, cols] = x[tb - SUBLANES:tb]
        y = cw[last:last + 1] * head[SUBLANES:2 * SUBLANES]
        for j in range(1, CONV_WIDTH):
            y = y + cw[last - j:last - j + 1] * head[SUBLANES - j:2 * SUBLANES - j]
        if tb > SUBLANES:
            rest = cw[last:last + 1] * x
            for j in range(1, CONV_WIDTH):
                rest = rest + cw[last - j:last - j + 1] * pltpu.roll(x, j, 0)
            y = jnp.concatenate([y, rest[SUBLANES:]], axis=0)
        y = _silu(y)
        if idx == 2:
            dst[...] = y
        else:
            scale = GDN_HEAD_DIM ** -0.5 if idx == 0 else 1.0
            for h in range(nh):
                yh = y[:, h * hd:(h + 1) * hd]
                dst[:, h * hd:(h + 1) * hd] = yh * (lax.rsqrt(jnp.sum(yh * yh, axis=-1, keepdims=True) + 1e-6) * scale)

    sm = sm_ref[...]
    xx = sm + dtb_ref[...]
    softplus = jnp.maximum(xx, 0.0) + jnp.log(1.0 + jnp.exp(-jnp.abs(xx)))
    row = _iota((tb, 1), 0)
    live = t * tb + row < t_valid
    g_small = jnp.where(live, -jnp.exp(alog_ref[...]) * softplus, 0.0)
    beta_small = jnp.where(live, jax.nn.sigmoid(sm), 0.0)
    row_in_chunk = row & (c - 1)
    gc_small = g_small
    shift = 1
    while shift < c:
        gc_small = gc_small + jnp.where(row_in_chunk >= shift, pltpu.roll(gc_small, shift, 0), 0.0)
        shift *= 2
    li = _iota((LANES, GDN_DIM), 0)
    head_of_lane = _iota((LANES, GDN_DIM), 1) >> int(math.log2(hd))

    def spread(x, first_lane):
        onehot = (li == head_of_lane + first_lane).astype(BF16)
        hi = x.astype(BF16)
        r1 = x - hi.astype(F32)
        mid = r1.astype(BF16)
        lo = (r1 - mid.astype(F32)).astype(BF16)
        return _dot(hi, onehot) + _dot(mid, onehot) + _dot(lo, onehot)

    gc_sc[...] = spread(gc_small, SMALL_A)
    b_sc[...] = spread(beta_small, SMALL_BT)
    for r0 in range(0, tb, LANES):
        n = min(LANES, tb - r0)
        part = gc_small[r0:r0 + n]
        if n < LANES:
            part = jnp.concatenate([part, jnp.zeros((LANES - n, LANES), F32)], axis=0)
        part_t = part.T
        for off in range(0, n, c):
            gct_sc[(r0 + off) // c] = part_t[:, off:off + c]

    ri = _iota((c, c), 0)
    ci = _iota((c, c), 1)
    causal = ri >= ci
    strict = ri > ci
    eye_f = (ri == ci).astype(F32)
    n_doubling = int(math.log2(c)) - 1
    gain = gain_ref[...]

    heads = range(nh)
    hs = [slice(h * hd, (h + 1) * hd) for h in heads]
    n_chunks = tb // c
    rw = [slice(ck * c, (ck + 1) * c) for ck in range(n_chunks)]
    pairs = [(ck, h) for ck in range(n_chunks) for h in heads]

    kq = {}
    for ck, h in pairs:
        kk = k_sc[rw[ck], hs[h]].astype(BF16)
        kq[ck, h] = _dot_nt(jnp.concatenate([kk, q_sc[rw[ck], hs[h]].astype(BF16)], axis=0), kk)
    decay = {(ck, h): jnp.where(causal, jnp.exp(gc_sc[rw[ck], hs[h]][:, :c] - gct_sc[ck, h:h + 1, :]), 0.0)
             for ck, h in pairs}
    attn = {p: kq[p][c:] * decay[p] for p in pairs}
    nk = {(ck, h): jnp.where(strict, -(b_sc[rw[ck], hs[h]][:, :c] * kq[ck, h][:c] * decay[ck, h]), 0.0)
          for ck, h in pairs}
    tinv = {p: eye_f + nk[p] for p in pairs}
    for _ in range(n_doubling):
        nk = {p: _mm(nk[p], nk[p]) for p in pairs}
        tinv = {p: tinv[p] + _mm(tinv[p], nk[p]) for p in pairs}
    uw = {}
    for ck, h in pairs:
        bb = b_sc[rw[ck], hs[h]]
        rhs = jnp.concatenate([v_sc[rw[ck], hs[h]] * bb,
                               k_sc[rw[ck], hs[h]] * (bb * jnp.exp(gc_sc[rw[ck], hs[h]]))], axis=1)
        uw[ck, h] = _mm(tinv[ck, h], rhs)

    state = [s_sc[h] for h in heads]
    for ck in range(n_chunks):
        rows = rw[ck]
        ws_qs = []
        for h in heads:
            q_dec = q_sc[rows, hs[h]] * jnp.exp(gc_sc[rows, hs[h]])
            ws_qs.append(_mm(jnp.concatenate([uw[ck, h][:, hd:], q_dec], axis=0), state[h]))
        v_new = [uw[ck, h][:, :hd] - ws_qs[h][:c] for h in heads]
        o = [ws_qs[h][c:] + _mm(attn[ck, h], v_new[h]) for h in heads]
        new_state = []
        for h in heads:
            gc = gc_sc[rows, hs[h]]
            g_last = gc[c - 1:c, :]
            k_dec = k_sc[rows, hs[h]] * jnp.exp(g_last - gc)
            new_state.append(state[h] * jnp.exp(g_last) + _dot_tn(k_dec, v_new[h]))
        state = new_state
        for h in heads:
            on = o[h] * lax.rsqrt(jnp.mean(o[h] * o[h], axis=-1, keepdims=True) + RMS_EPS) * gain
            o_ref[rows, hs[h]] = on * _silu(z_ref[rows, hs[h]])
    for h in heads:
        s_sc[h] = state[h]

    @pl.when(t == pl.num_programs(1) - 1)
    def _():
        sfin_ref[...] = s_sc[...]


def _gdn(p3, conv_w, prefix, alog, dtb, gain, s0, *, tb, chunk, t_valid):
    b, t, _ = p3.shape
    hd = GDN_HEAD_DIM
    gd = GDN_DIM
    const2 = lambda bi, ti: (0, 0)
    kern = functools.partial(_gdn_kernel, tb=tb, chunk=chunk, t_valid=t_valid)
    return pl.pallas_call(
        kern,
        grid=(b, t // tb),
        in_specs=[
            pl.BlockSpec((None, tb, gd), lambda bi, ti: (bi, ti, COL_QKV // gd)),
            pl.BlockSpec((None, tb, gd), lambda bi, ti: (bi, ti, COL_QKV // gd + 1)),
            pl.BlockSpec((None, tb, gd), lambda bi, ti: (bi, ti, COL_QKV // gd + 2)),
            pl.BlockSpec((None, tb, gd), lambda bi, ti: (bi, ti, COL_ZA // gd)),
            pl.BlockSpec((None, tb, LANES), lambda bi, ti: (bi, ti, COL_SMALL // LANES)),
            pl.BlockSpec((CONV_WIDTH, GDN_QKV_DIM), const2),
            pl.BlockSpec((None, SUBLANES, GDN_QKV_DIM), lambda bi, ti: (bi, 0, 0)),
            pl.BlockSpec((1, LANES), const2),
            pl.BlockSpec((1, LANES), const2),
            pl.BlockSpec((1, hd), const2),
            pl.BlockSpec((None, GDN_HEADS, hd, hd), lambda bi, ti: (bi, 0, 0, 0)),
        ],
        out_specs=[
            pl.BlockSpec((None, tb, gd), lambda bi, ti: (bi, ti, 0)),
            pl.BlockSpec((None, GDN_HEADS, hd, hd), lambda bi, ti: (bi, 0, 0, 0)),
        ],
        out_shape=[
            jax.ShapeDtypeStruct((b, t, gd), F32),
            jax.ShapeDtypeStruct((b, GDN_HEADS, hd, hd), F32),
        ],
        scratch_shapes=[
            pltpu.VMEM((GDN_HEADS, hd, hd), F32),
            pltpu.VMEM((SUBLANES, GDN_QKV_DIM), F32),
            pltpu.VMEM((tb, gd), F32),
            pltpu.VMEM((tb, gd), F32),
            pltpu.VMEM((tb, gd), F32),
            pltpu.VMEM((tb, gd), F32),
            pltpu.VMEM((tb, gd), F32),
            pltpu.VMEM((tb // chunk, LANES, chunk), F32),
        ],
        compiler_params=pltpu.CompilerParams(
            dimension_semantics=("parallel", "arbitrary"), vmem_limit_bytes=VMEM_LIMIT),
    )(p3, p3, p3, p3, p3, conv_w, prefix, alog, dtb, gain, s0)


def _masked_softmax2(s, mask):
    s = jnp.where(mask, s, NEG_INF)
    e = jnp.where(mask, jnp.exp2(s - jnp.max(s, axis=-1, keepdims=True)), 0.0)
    return e / jnp.maximum(jnp.sum(e, axis=-1, keepdims=True), 1e-30)


def _stack_heads(q, g):
    hd = NSA_HEAD_DIM
    parts = [q[:, (NSA_GROUP * g + h) * hd:(NSA_GROUP * g + h + 1) * hd] for h in range(NSA_GROUP)]
    return jnp.concatenate(parts, axis=0) * (NSA_SCALE * LOG2E)


def _tile_rows(x):
    return jnp.concatenate([x] * NSA_GROUP, axis=0)


def _group_importance(p_c, tq):
    imp = p_c[0:tq]
    for h in range(1, NSA_GROUP):
        imp = imp + p_c[h * tq:(h + 1) * tq]
    return imp


def _select_blocks(p_c, pos, tq, n_blocks, nbl):
    m = p_c.shape[1]
    imp = _group_importance(p_c, tq)
    pool = ((_iota((m, nbl), 0) >> int(math.log2(SEL_RATIO))) == _iota((m, nbl), 1)).astype(F32)
    imp_sel = _dot(imp, pool, HI)
    lane = _iota((tq, nbl), 1)
    valid = lane * SEL_BLOCK <= pos
    cur = pos >> SEL_SHIFT
    forced = (lane == 0) | (lane == cur) | (lane == cur - 1)
    score = jnp.where(valid, imp_sel + FORCE_BONUS * forced.astype(F32), -1.0)
    score = jnp.where(lane < n_blocks, score, -2.0)
    rank = jnp.zeros((tq, nbl), F32)
    for j in range(n_blocks):
        sj = score[:, j:j + 1]
        rank = rank + ((sj > score) | ((sj == score) & (lane > j))).astype(F32)
    return (rank < float(min(N_SELECT, n_blocks))).astype(F32)


def _select_blocks_t(p_c, q0, tq, n_blocks, nbl):
    m = p_c.shape[1]
    n_sel = min(N_SELECT, n_blocks)
    nbs = -(-n_blocks // SUBLANES) * SUBLANES
    imp = _group_importance(p_c, tq)
    pool_t = ((_iota((nbs, m), 1) >> int(math.log2(SEL_RATIO))) == _iota((nbs, m), 0)).astype(F32)
    imp_t = _dot_nt(pool_t, imp, HI)
    pos_r = q0 + _iota((1, tq), 1)
    blk = _iota((nbs, tq), 0)
    valid = blk * SEL_BLOCK <= pos_r

    def ranked():
        cur = pos_r >> SEL_SHIFT
        forced = (blk == 0) | (blk == cur) | (blk == cur - 1)
        score = jnp.where(valid, imp_t + FORCE_BONUS * forced.astype(F32), -1.0)
        score = jnp.where(blk < n_blocks, score, -2.0)
        rank = jnp.zeros((nbs, tq), F32)
        for j in range(n_blocks):
            sj = score[j:j + 1, :]
            rank = rank + ((sj > score) | ((sj == score) & (blk > j))).astype(F32)
        return (rank < float(n_sel)).astype(F32)

    sel_t = lax.cond(q0 + tq <= n_sel * SEL_BLOCK, lambda: valid.astype(F32), ranked)
    sel_t = jnp.concatenate([sel_t, jnp.zeros((nbl - nbs, tq), F32)], axis=0)
    return sel_t.T


def _compress_rows(x, pw):
    n = x.shape[0] // CMP_STRIDE
    x3 = x.reshape(n, CMP_STRIDE, LANES)
    lo = jnp.sum(x3 * pw[:CMP_STRIDE][None], axis=1)
    hi = jnp.sum(x3 * pw[CMP_STRIDE:][None], axis=1)
    return lo, hi


def _combine_heads(o_ref, g, gates, zb, o_c, o_s, o_w, tq):
    hd = NSA_HEAD_DIM
    for h in range(NSA_GROUP):
        hh = NSA_GROUP * g + h
        c0 = SMALL_GB + 3 * hh
        rows = slice(h * tq, (h + 1) * tq)
        o_h = (gates[:, c0:c0 + 1] * o_c[rows] + gates[:, c0 + 1:c0 + 2] * o_s[rows]
               + gates[:, c0 + 2:c0 + 3] * o_w[rows])
        o_ref[:, hh * hd:(hh + 1) * hd] = o_h * _silu(zb[:, hh * hd:(hh + 1) * hd])


def _nsa_prompt_kernel(q_ref, zb_ref, sm_ref, kcmp_ref, vcmp_ref, kslc_ref, vslc_ref, kwin_ref, vwin_ref,
                       pk_ref, pv_ref, wk_ref, wv_ref, o_ref, kc_sc, vc_sc, k_sc2, vx_sc, *, t_len, tq, tk, span, nbl):
    i = pl.program_id(1)
    m = t_len // CMP_STRIDE
    n_blocks = t_len // SEL_BLOCK
    hd = NSA_HEAD_DIM

    @pl.when(i == 0)
    def _():
        for src, pw_ref, w_ref, dst in ((kcmp_ref, pk_ref, wk_ref, kc_sc), (vcmp_ref, pv_ref, wv_ref, vc_sc)):
            lo, hi = _compress_rows(src[...], pw_ref[...])
            hi_next = pltpu.roll(hi, m - 1, 0)
            pre = jnp.where(_iota((m, LANES), 0) < m - 1, lo + hi_next, 0.0)
            dst[...] = _dot(pre, w_ref[...], HI)
        k_sc2[0] = kslc_ref[...].astype(BF16)
        k_sc2[1] = kwin_ref[...].astype(BF16)
        ones = jnp.ones((t_len, hd), BF16)
        for idx, src in enumerate((vslc_ref, vwin_ref)):
            v = src[...].astype(BF16)
            for g in range(NSA_KV_HEADS):
                vx_sc[idx, g] = jnp.concatenate([v[:, g * hd:(g + 1) * hd], ones], axis=1)

    q0 = i * tq
    pos = q0 + _iota((tq, 1), 0)
    pos4 = _tile_rows(pos)
    q = q_ref[...]
    zb = zb_ref[...]
    gates = jax.nn.sigmoid(sm_ref[...])
    c_end = _iota((1, m), 1) * CMP_STRIDE + (CMP_LEN - 1)
    n_kt = (q0 + tq + tk - 1) // tk
    w_start = pl.multiple_of(jnp.clip(q0 - WINDOW, 0, t_len - span), tq)
    w_dist = pos - (w_start + _iota((1, span), 1))
    w_bias = _tile_rows(jnp.where((w_dist >= 0) & (w_dist <= WINDOW), 0.0, NEG_INF))

    for g in range(NSA_KV_HEADS):
        gs = slice(g * hd, (g + 1) * hd)
        qs = _stack_heads(q, g)
        qs_bf = qs.astype(BF16)
        p_c = _masked_softmax2(_dot_nt(qs, kc_sc[:, gs], HI), c_end <= pos4)
        o_c = _dot(p_c, vc_sc[:, gs])
        sel = _select_blocks_t(p_c, q0, tq, n_blocks, nbl)
        sel_bias = ((sel - 1.0) * -NEG_INF).astype(BF16)

        def sel_body(kt, carry, qs_bf=qs_bf, sel_bias=sel_bias, gs=gs, g=g):
            m_run, acc = carry
            k0 = pl.multiple_of(kt * tk, tk)
            expand = ((k0 + _iota((nbl, tk), 1)) >> SEL_SHIFT) == _iota((nbl, tk), 0)
            bias = _dot(sel_bias, expand.astype(BF16)) + jnp.where(k0 + _iota((1, tk), 1) <= pos, 0.0, NEG_INF)
            s = _dot_nt(qs_bf, k_sc2[0, pl.ds(k0, tk), gs]) + _tile_rows(bias)
            m_new = jnp.maximum(m_run, jnp.max(s, axis=-1, keepdims=True))
            e = jnp.exp2(s - m_new).astype(BF16)
            acc_new = jnp.exp2(m_run - m_new) * acc + _dot(e, vx_sc[0, g, pl.ds(k0, tk), :])
            return m_new, acc_new

        init = (jnp.full((NSA_GROUP * tq, 1), NEG_INF, F32), jnp.zeros((NSA_GROUP * tq, 2 * hd), F32))
        _, acc_s = lax.fori_loop(0, n_kt, sel_body, init)
        o_s = acc_s[:, :hd] / jnp.maximum(acc_s[:, hd:hd + 1], 1e-30)

        s_w = _dot_nt(qs_bf, k_sc2[1, pl.ds(w_start, span), gs]) + w_bias
        e_w = jnp.exp2(s_w - jnp.max(s_w, axis=-1, keepdims=True)).astype(BF16)
        acc_w = _dot(e_w, vx_sc[1, g, pl.ds(w_start, span), :])
        o_w = acc_w[:, :hd] / jnp.maximum(acc_w[:, hd:hd + 1], 1e-30)
        _combine_heads(o_ref, g, gates, zb, o_c, o_s, o_w, tq)


def _nsa_prompt(p3, pos_k, pos_v, w_k, w_v, *, tq, tk):
    b, t, _ = p3.shape
    span = WINDOW + tq
    assert t % tq == 0 and t % tk == 0 and t >= span and t % SEL_BLOCK == 0 and tq % LANES == 0
    assert WINDOW % tq == 0
    m = t // CMP_STRIDE
    nbl = -(-(t // SEL_BLOCK) // LANES) * LANES
    kvc = COL_KV // LANES

    def full(c):
        return pl.BlockSpec((None, t, LANES), lambda bi, i: (bi, 0, c))

    const2 = lambda bi, i: (0, 0)
    kern = functools.partial(_nsa_prompt_kernel, t_len=t, tq=tq, tk=tk, span=span, nbl=nbl)
    return pl.pallas_call(
        kern,
        grid=(b, t // tq),
        in_specs=[
            pl.BlockSpec((None, tq, NSA_DIM), lambda bi, i: (bi, i, COL_QB // NSA_DIM)),
            pl.BlockSpec((None, tq, NSA_DIM), lambda bi, i: (bi, i, COL_ZB // NSA_DIM)),
            pl.BlockSpec((None, tq, LANES), lambda bi, i: (bi, i, COL_SMALL // LANES)),
            full(kvc), full(kvc + 1), full(kvc + 2), full(kvc + 3), full(kvc + 4), full(kvc + 5),
            pl.BlockSpec((CMP_LEN, LANES), const2),
            pl.BlockSpec((CMP_LEN, LANES), const2),
            pl.BlockSpec((LANES, LANES), const2),
            pl.BlockSpec((LANES, LANES), const2),
        ],
        out_specs=pl.BlockSpec((None, tq, NSA_DIM), lambda bi, i: (bi, i, 0)),
        out_shape=jax.ShapeDtypeStruct((b, t, NSA_DIM), F32),
        scratch_shapes=[pltpu.VMEM((m, LANES), F32), pltpu.VMEM((m, LANES), F32),
                        pltpu.VMEM((2, t, LANES), BF16),
                        pltpu.VMEM((2, NSA_KV_HEADS, t, 2 * NSA_HEAD_DIM), BF16)],
        compiler_params=pltpu.CompilerParams(
            dimension_semantics=("parallel", "arbitrary"), vmem_limit_bytes=VMEM_LIMIT),
    )(p3, p3, p3, p3, p3, p3, p3, p3, p3, pos_k, pos_v, w_k, w_v)


def _cmp_paged_kernel(pt_ref, *refs, k_pages):
    del pt_ref
    kp = refs[:k_pages]
    kn = refs[k_pages]
    vp = refs[k_pages + 1:2 * k_pages + 1]
    vn = refs[2 * k_pages + 1]
    plk_ref, phk_ref, plv_ref, phv_ref, seg_ref, wk_ref, wv_ref, kc_ref, vc_ref = refs[2 * k_pages + 2:]
    seg = seg_ref[...]
    lane = _iota((LANES, LANES), 1)

    def chunk_sums(ys):
        y = jnp.concatenate(ys, axis=1)
        hi = y.astype(BF16)
        mid = (y - hi.astype(F32)).astype(BF16)
        return _dot(hi, seg) + _dot(mid, seg)

    for pages, nxt, pl_ref, ph_ref, w_ref, out in ((kp, kn, plk_ref, phk_ref, wk_ref, kc_ref),
                                                   (vp, vn, plv_ref, phv_ref, wv_ref, vc_ref)):
        p_lo = pl_ref[...]
        p_hi = ph_ref[...]
        xs = [r[...].reshape(LANES, LANES) for r in pages]
        lo = chunk_sums([x * p_lo for x in xs])
        hi = chunk_sums([x * p_hi for x in xs])
        x_next = nxt[...].reshape(LANES, LANES)
        hi_head = jnp.sum(jnp.where(lane < CMP_STRIDE, x_next * p_hi, 0.0), axis=1, keepdims=True)
        hi_next = jnp.where(lane == LANES - 1, hi_head, pltpu.roll(hi, LANES - 1, 1))
        out[...] = _dot(w_ref[...], lo + hi_next, HI)


def _cmp_paged(pool_k, pool_v, page_table, pos_k, pos_v, w_k, w_v, *, k_pages):
    b, n_pages = page_table.shape
    page = pool_k.shape[3]
    per_step = k_pages * page // CMP_STRIDE
    assert page == LANES and n_pages % k_pages == 0 and per_step == LANES
    blk = (None, NSA_KV_HEADS, NSA_HEAD_DIM, page)

    def page_spec(i):
        return pl.BlockSpec(blk, lambda bi, j, pt: (pt[bi, j * k_pages + i], 0, 0, 0))

    next_spec = pl.BlockSpec(
        blk, lambda bi, j, pt: (pt[bi, jnp.minimum((j + 1) * k_pages, n_pages - 1)], 0, 0, 0))
    const2 = lambda bi, j, pt: (0, 0)

    def pos_tile(p):
        return jnp.tile(p.T, (NSA_KV_HEADS, page // CMP_STRIDE))

    keys = jnp.arange(k_pages * page) // CMP_STRIDE
    seg = (keys[:, None] == jnp.arange(per_step)[None, :]).astype(BF16)
    in_specs = ([page_spec(i) for i in range(k_pages)] + [next_spec]
                + [page_spec(i) for i in range(k_pages)] + [next_spec]
                + [pl.BlockSpec((LANES, LANES), const2)] * 4
                + [pl.BlockSpec((k_pages * page, per_step), const2)]
                + [pl.BlockSpec((LANES, LANES), const2)] * 2)
    out_spec = pl.BlockSpec((None, LANES, per_step), lambda bi, j, pt: (bi, 0, j))
    m = n_pages * page // CMP_STRIDE
    return pl.pallas_call(
        functools.partial(_cmp_paged_kernel, k_pages=k_pages),
        grid_spec=pltpu.PrefetchScalarGridSpec(
            num_scalar_prefetch=1, grid=(b, n_pages // k_pages), in_specs=in_specs,
            out_specs=[out_spec, out_spec]),
        out_shape=[jax.ShapeDtypeStruct((b, LANES, m), F32)] * 2,
        compiler_params=pltpu.CompilerParams(
            dimension_semantics=("parallel", "arbitrary"), vmem_limit_bytes=VMEM_LIMIT),
    )(page_table, *([pool_k] * (k_pages + 1)), *([pool_v] * (k_pages + 1)),
      pos_tile(pos_k[:CMP_STRIDE]), pos_tile(pos_k[CMP_STRIDE:]), pos_tile(pos_v[:CMP_STRIDE]),
      pos_tile(pos_v[CMP_STRIDE:]), seg, _block_diag2(w_k).T, _block_diag2(w_v).T)


def _new_key_scores(qs, new_k, r, gs):
    return jnp.sum(qs * new_k[r:r + 1, gs], axis=-1, keepdims=True)


def _nsa_sample_a_kernel(q_ref, kc_ref, vc_ref, bk_ref, bv_ref, nk_ref, nv_ref,
                         oc_ref, ow_ref, sel_ref, wk_out, wv_out, *, past, t_valid, tp, n_blocks, nbl):
    hd = NSA_HEAD_DIM
    m = kc_ref.shape[1]
    w_buf = bk_ref.shape[0]
    pos = past + _iota((tp, 1), 0)
    pos4 = _tile_rows(pos)
    q = q_ref[...]
    new_k = nk_ref[...]
    new_v = nv_ref[...]
    c_end = _iota((1, m), 1) * CMP_STRIDE + (CMP_LEN - 1)
    buf_pos = (past - w_buf) + _iota((1, w_buf), 1)
    buf_dist = pos4 - buf_pos
    buf_mask = (buf_dist >= 0) & (buf_dist <= WINDOW)
    for g in range(NSA_KV_HEADS):
        gs = slice(g * hd, (g + 1) * hd)
        qs = _stack_heads(q, g)
        p_c = _masked_softmax2(_dot(qs, kc_ref[gs, :], HI), c_end <= pos4)
        oc_ref[g] = _dot_nt(p_c, vc_ref[gs, :])
        sel_ref[g] = _select_blocks(p_c, pos, tp, n_blocks, nbl)
        s_buf = jnp.where(buf_mask, _dot_nt(qs, bk_ref[:, gs]), NEG_INF)
        s_new, new_mask = [], []
        for r in range(t_valid):
            dist = pos4 - (past + r)
            mk = (dist >= 0) & (dist <= WINDOW)
            new_mask.append(mk)
            s_new.append(jnp.where(mk, _new_key_scores(qs, new_k, r, gs), NEG_INF))
        m_all = jnp.max(s_buf, axis=-1, keepdims=True)
        for sn in s_new:
            m_all = jnp.maximum(m_all, sn)
        e_buf = jnp.where(buf_mask, jnp.exp2(s_buf - m_all), 0.0)
        l_all = jnp.sum(e_buf, axis=-1, keepdims=True)
        acc = _dot(e_buf, bv_ref[:, gs])
        for r in range(t_valid):
            e_r = jnp.where(new_mask[r], jnp.exp2(s_new[r] - m_all), 0.0)
            l_all = l_all + e_r
            acc = acc + e_r * new_v[r:r + 1, gs]
        ow_ref[g] = acc / jnp.maximum(l_all, 1e-30)
    wk_out[0:w_buf - t_valid, :] = bk_ref[t_valid:w_buf, :]
    wk_out[w_buf - t_valid:w_buf, :] = new_k[0:t_valid]
    wv_out[0:w_buf - t_valid, :] = bv_ref[t_valid:w_buf, :]
    wv_out[w_buf - t_valid:w_buf, :] = new_v[0:t_valid]


def _nsa_sample_a(p3, kc, vc, buf_k, buf_v, *, past, t_valid, n_blocks, nbl):
    b, tp, _ = p3.shape
    m = kc.shape[2]
    w_buf = buf_k.shape[1]
    assert w_buf == WINDOW and t_valid <= w_buf
    kvc = COL_KV // LANES
    rows = NSA_GROUP * tp
    kern = functools.partial(_nsa_sample_a_kernel, past=past, t_valid=t_valid, tp=tp,
                             n_blocks=n_blocks, nbl=nbl)
    per_b3 = lambda bi: (bi, 0, 0)
    per_b4 = lambda bi: (bi, 0, 0, 0)
    return pl.pallas_call(
        kern,
        grid=(b,),
        in_specs=[
            pl.BlockSpec((None, tp, NSA_DIM), lambda bi: (bi, 0, COL_QB // NSA_DIM)),
            pl.BlockSpec((None, LANES, m), per_b3),
            pl.BlockSpec((None, LANES, m), per_b3),
            pl.BlockSpec((None, w_buf, LANES), per_b3),
            pl.BlockSpec((None, w_buf, LANES), per_b3),
            pl.BlockSpec((None, tp, LANES), lambda bi: (bi, 0, kvc + 4)),
            pl.BlockSpec((None, tp, LANES), lambda bi: (bi, 0, kvc + 5)),
        ],
        out_specs=[
            pl.BlockSpec((None, NSA_KV_HEADS, rows, NSA_HEAD_DIM), per_b4),
            pl.BlockSpec((None, NSA_KV_HEADS, rows, NSA_HEAD_DIM), per_b4),
            pl.BlockSpec((None, NSA_KV_HEADS, tp, nbl), per_b4),
            pl.BlockSpec((None, w_buf, LANES), per_b3),
            pl.BlockSpec((None, w_buf, LANES), per_b3),
        ],
        out_shape=[
            jax.ShapeDtypeStruct((b, NSA_KV_HEADS, rows, NSA_HEAD_DIM), F32),
            jax.ShapeDtypeStruct((b, NSA_KV_HEADS, rows, NSA_HEAD_DIM), F32),
            jax.ShapeDtypeStruct((b, NSA_KV_HEADS, tp, nbl), F32),
            jax.ShapeDtypeStruct((b, w_buf, LANES), F32),
            jax.ShapeDtypeStruct((b, w_buf, LANES), F32),
        ],
        compiler_params=pltpu.CompilerParams(
            dimension_semantics=("parallel",), vmem_limit_bytes=VMEM_LIMIT),
    )(p3, kc, vc, buf_k, buf_v, p3, p3)


def _nsa_sample_b_kernel(pt_ref, q_ref, zb_ref, sm_ref, nk_ref, nv_ref, sel_ref, selj_ref, exp_ref,
                         oc_ref, ow_ref, *refs, k_pages, past, t_valid, tp, n_blocks, nbl):
    del pt_ref
    kp = refs[:k_pages]
    vp = refs[k_pages:2 * k_pages]
    o_ref, m_sc, l_sc, acc_sc = refs[2 * k_pages:]
    hd = NSA_HEAD_DIM
    j = pl.program_id(1)

    @pl.when(j == 0)
    def _():
        m_sc[...] = jnp.full(m_sc.shape, NEG_INF, F32)
        l_sc[...] = jnp.zeros(l_sc.shape, F32)
        acc_sc[...] = jnp.zeros(acc_sc.shape, F32)

    q = q_ref[...]
    expand = exp_ref[...]
    for g in range(NSA_KV_HEADS):
        qs = _stack_heads(q, g).astype(BF16)
        k_t = jnp.concatenate([r[g] for r in kp], axis=1).astype(BF16)
        v_t = jnp.concatenate([r[g] for r in vp], axis=1).astype(BF16)
        sel_bias = ((selj_ref[g] - 1.0) * -NEG_INF).astype(BF16)
        s = _dot(qs, k_t) + _tile_rows(_dot(sel_bias, expand))
        m_run = m_sc[g]
        m_new = jnp.maximum(m_run, jnp.max(s, axis=-1, keepdims=True))
        alpha = jnp.exp2(m_run - m_new)
        e = jnp.exp2(s - m_new)
        l_sc[g] = alpha * l_sc[g] + jnp.sum(e, axis=-1, keepdims=True)
        acc_sc[g] = alpha * acc_sc[g] + _dot_nt(e.astype(BF16), v_t)
        m_sc[g] = m_new

    @pl.when(j == pl.num_programs(1) - 1)
    def _():
        pos4 = _tile_rows(past + _iota((tp, 1), 0))
        gates = jax.nn.sigmoid(sm_ref[...])
        zb = zb_ref[...]
        new_k = nk_ref[...]
        new_v = nv_ref[...]
        for g in range(NSA_KV_HEADS):
            gs = slice(g * hd, (g + 1) * hd)
            qs = _stack_heads(q, g)
            picked = _tile_rows(sel_ref[g][:, n_blocks - 1:n_blocks] > 0.5)
            m_run, l_run, acc = m_sc[g], l_sc[g], acc_sc[g]
            s_new, new_mask = [], []
            for r in range(t_valid):
                mk = picked & (past + r <= pos4)
                new_mask.append(mk)
                s_new.append(jnp.where(mk, _new_key_scores(qs, new_k, r, gs), NEG_INF))
            m_all = m_run
            for sn in s_new:
                m_all = jnp.maximum(m_all, sn)
            alpha = jnp.exp2(m_run - m_all)
            l_all = alpha * l_run
            acc = alpha * acc
            for r in range(t_valid):
                e_r = jnp.where(new_mask[r], jnp.exp2(s_new[r] - m_all), 0.0)
                l_all = l_all + e_r
                acc = acc + e_r * new_v[r:r + 1, gs]
            o_s = acc / jnp.maximum(l_all, 1e-30)
            _combine_heads(o_ref, g, gates, zb, oc_ref[g], o_s, ow_ref[g], tp)


def _nsa_sample_b(p3, sel, o_c, o_w, pool_k, pool_v, page_table, *, k_pages, past, t_valid, n_blocks, nbl):
    b, tp, _ = p3.shape
    n_pages = page_table.shape[1]
    page = pool_k.shape[3]
    n_steps = n_pages // k_pages
    n_keys = k_pages * page
    per_step = n_keys // SEL_BLOCK
    assert n_pages % k_pages == 0 and page % SEL_BLOCK == 0 and past == n_pages * page
    assert n_blocks == past // SEL_BLOCK + 1 and t_valid <= SEL_BLOCK and per_step <= LANES
    kvc = COL_KV // LANES
    rows = NSA_GROUP * tp
    sel_steps = sel[..., :n_steps * per_step].reshape(b, NSA_KV_HEADS, tp, n_steps, per_step)
    sel_steps = jnp.pad(sel_steps.transpose(0, 3, 1, 2, 4), ((0, 0),) * 4 + ((0, LANES - per_step),))
    expand = (jnp.arange(LANES)[:, None] == (jnp.arange(n_keys) // SEL_BLOCK)[None, :]).astype(BF16)
    blk = (None, NSA_KV_HEADS, NSA_HEAD_DIM, page)

    def page_spec(i):
        return pl.BlockSpec(blk, lambda bi, j, pt: (pt[bi, j * k_pages + i], 0, 0, 0))

    per_b4 = lambda bi, j, pt: (bi, 0, 0, 0)
    in_specs = [
        pl.BlockSpec((None, tp, NSA_DIM), lambda bi, j, pt: (bi, 0, COL_QB // NSA_DIM)),
        pl.BlockSpec((None, tp, NSA_DIM), lambda bi, j, pt: (bi, 0, COL_ZB // NSA_DIM)),
        pl.BlockSpec((None, tp, LANES), lambda bi, j, pt: (bi, 0, COL_SMALL // LANES)),
        pl.BlockSpec((None, tp, LANES), lambda bi, j, pt: (bi, 0, kvc + 2)),
        pl.BlockSpec((None, tp, LANES), lambda bi, j, pt: (bi, 0, kvc + 3)),
        pl.BlockSpec((None, NSA_KV_HEADS, tp, nbl), per_b4),
        pl.BlockSpec((None, None, NSA_KV_HEADS, tp, LANES), lambda bi, j, pt: (bi, j, 0, 0, 0)),
        pl.BlockSpec((LANES, n_keys), lambda bi, j, pt: (0, 0)),
        pl.BlockSpec((None, NSA_KV_HEADS, rows, NSA_HEAD_DIM), per_b4),
        pl.BlockSpec((None, NSA_KV_HEADS, rows, NSA_HEAD_DIM), per_b4),
    ] + [page_spec(i) for i in range(k_pages)] * 2
    kern = functools.partial(_nsa_sample_b_kernel, k_pages=k_pages, past=past, t_valid=t_valid, tp=tp,
                             n_blocks=n_blocks, nbl=nbl)
    return pl.pallas_call(
        kern,
        grid_spec=pltpu.PrefetchScalarGridSpec(
            num_scalar_prefetch=1, grid=(b, n_steps), in_specs=in_specs,
            out_specs=pl.BlockSpec((None, tp, NSA_DIM), lambda bi, j, pt: (bi, 0, 0)),
            scratch_shapes=[pltpu.VMEM((NSA_KV_HEADS, rows, 1), F32), pltpu.VMEM((NSA_KV_HEADS, rows, 1), F32),
                            pltpu.VMEM((NSA_KV_HEADS, rows, NSA_HEAD_DIM), F32)]),
        out_shape=jax.ShapeDtypeStruct((b, tp, NSA_DIM), F32),
        compiler_params=pltpu.CompilerParams(
            dimension_semantics=("parallel", "arbitrary"), vmem_limit_bytes=VMEM_LIMIT),
    )(page_table, p3, p3, p3, p3, p3, sel, sel_steps, expand, o_c, o_w,
      *([pool_k] * k_pages), *([pool_v] * k_pages))


def _merge_kernel(x_ref, oa_ref, ob_ref, ga_ref, gb_ref, wa_ref, wb_ref, wo_ref, fg_ref, y_ref):
    mixed = (jax.nn.sigmoid(ga_ref[...]) * _dot(oa_ref[...].astype(BF16), wa_ref[...])
             + jax.nn.sigmoid(gb_ref[...]) * _dot(ob_ref[...].astype(BF16), wb_ref[...]))
    x_new = x_ref[...] + _dot(mixed.astype(BF16), wo_ref[...])
    y = x_new * lax.rsqrt(jnp.mean(x_new * x_new, axis=-1, keepdims=True) + RMS_EPS)
    y_ref[...] = y * fg_ref[...]


def _merge(x2d, o_a, o_b, p2d, w_a, w_b, w_o, final_gain, tm):
    rows, d = x2d.shape
    const2 = lambda i: (0, 0)
    return pl.pallas_call(
        _merge_kernel,
        grid=(rows // tm,),
        in_specs=[
            pl.BlockSpec((tm, d), lambda i: (i, 0)),
            pl.BlockSpec((tm, GDN_DIM), lambda i: (i, 0)),
            pl.BlockSpec((tm, NSA_DIM), lambda i: (i, 0)),
            pl.BlockSpec((tm, d), lambda i: (i, COL_GATE_A // D_MODEL)),
            pl.BlockSpec((tm, d), lambda i: (i, COL_GATE_B // D_MODEL)),
            pl.BlockSpec((GDN_DIM, d), const2),
            pl.BlockSpec((NSA_DIM, d), const2),
            pl.BlockSpec((d, d), const2),
            pl.BlockSpec((1, d), const2),
        ],
        out_specs=pl.BlockSpec((tm, d), lambda i: (i, 0)),
        out_shape=jax.ShapeDtypeStruct((rows, d), F32),
        compiler_params=pltpu.CompilerParams(
            dimension_semantics=("parallel",), vmem_limit_bytes=VMEM_LIMIT),
    )(x2d, o_a, o_b, p2d, p2d, w_a, w_b, w_o, final_gain)


def _pack_w_in(w_in):
    bounds = [0]
    for width in (GDN_QKV_DIM, GDN_HEADS, GDN_HEADS, GDN_DIM, NSA_DIM, 6 * NSA_KV_DIM, 3 * NSA_HEADS,
                  NSA_DIM, 2 * D_MODEL):
        bounds.append(bounds[-1] + width)
    qkv, a, bt, z_a, q_b, kv_b, g_b, z_b, merge = (w_in[:, bounds[i]:bounds[i + 1]] for i in range(9))
    small = jnp.concatenate([a, bt, g_b], axis=1)
    small = jnp.pad(small, ((0, 0), (0, LANES - small.shape[1])))
    packed = jnp.concatenate([qkv, z_a, merge, q_b, z_b, kv_b, small], axis=1)
    assert packed.shape[1] == N_PACKED
    return packed.astype(BF16)


def _lane_vec(v):
    return jnp.pad(v.astype(F32), (0, LANES - v.shape[0])).reshape(1, LANES)


def _block_diag2(w):
    z = jnp.zeros_like(w)
    return jnp.concatenate([jnp.concatenate([w, z], axis=1), jnp.concatenate([z, w], axis=1)], axis=0)


def _tile_for(n, candidates):
    for c in candidates:
        if n % c == 0:
            return c
    raise ValueError(f"no tile for {n}")


def kernel(x_prompt, x_sample, cache_k_cmp, cache_v_cmp, cache_k_slc, cache_v_slc, state_win_k, state_win_v,
           state_conv, state_gdn, page_table, norm_gain, w_in, conv_w, a_log, dt_bias, gdn_gain, cmp_pos_k,
           cmp_w_k, cmp_pos_v, cmp_w_v, w_branch_a, w_branch_b, w_out, final_gain):
    assert w_in.shape[0] == 1, "one layer"
    bp, tp_len, d = x_prompt.shape
    bs, ts, _ = x_sample.shape
    n_pool, page = cache_k_cmp.shape[1:3]
    n_pages = page_table.shape[1]
    past = n_pages * page
    assert past % CMP_STRIDE == 0 and ts < CMP_STRIDE and ts >= CONV_WIDTH - 1 and ts <= SUBLANES
    t_pad = SUBLANES

    w_packed = _pack_w_in(w_in[0])
    gain_in = norm_gain[0].reshape(1, d)
    fgain = final_gain.reshape(1, d)
    w_a = w_branch_a[0].astype(BF16)
    w_b = w_branch_b[0].astype(BF16)
    w_o = w_out[0].astype(BF16)
    cw = conv_w[0]
    alog = _lane_vec(a_log[0])
    dtb = _lane_vec(dt_bias[0])
    ggain = gdn_gain[0].reshape(1, GDN_HEAD_DIM)
    pos_k = jnp.tile(cmp_pos_k[0], (1, NSA_KV_HEADS))
    pos_v = jnp.tile(cmp_pos_v[0], (1, NSA_KV_HEADS))
    wk2 = _block_diag2(cmp_w_k[0])
    wv2 = _block_diag2(cmp_w_v[0])
    tn = _tile_for(N_PACKED, (1152, 896, 128))
    kvc = COL_KV

    rows_p = bp * tp_len
    x2p = x_prompt.reshape(rows_p, d)
    p2 = _proj(x2p, gain_in, w_packed, _tile_for(rows_p, (1024, 512, 256, 128)), tn)
    p3 = p2.reshape(bp, tp_len, N_PACKED)
    tb = _tile_for(tp_len, (256, 128, 64))
    o_a, gdn_p = _gdn(p3, cw, jnp.zeros((bp, SUBLANES, GDN_QKV_DIM), F32), alog, dtb, ggain,
                      jnp.zeros((bp, GDN_HEADS, GDN_HEAD_DIM, GDN_HEAD_DIM), F32),
                      tb=tb, chunk=GDN_CHUNK, t_valid=tp_len)
    o_b = _nsa_prompt(p3, pos_k, pos_v, wk2, wv2, tq=256, tk=_tile_for(tp_len, (512, 256, 128)))
    y_p = _merge(x2p, o_a.reshape(rows_p, GDN_DIM), o_b.reshape(rows_p, NSA_DIM), p2, w_a, w_b, w_o, fgain,
                 _tile_for(rows_p, (512, 256, 128)))
    y_prompt = y_p.reshape(bp, tp_len, d)

    def kv_out(p, i, t_keep_from, t_to):
        return p[:, t_keep_from:t_to, kvc + i * LANES:kvc + (i + 1) * LANES].reshape(
            1, p.shape[0], t_to - t_keep_from, NSA_KV_HEADS, NSA_HEAD_DIM)

    keep_p = min(WINDOW, tp_len)
    prompt_state = ([kv_out(p3, i, 0, tp_len) for i in range(4)]
                    + [kv_out(p3, i, tp_len - keep_p, tp_len) for i in (4, 5)]
                    + [p3[:, tp_len - (CONV_WIDTH - 1):, :GDN_QKV_DIM][None], gdn_p[None]])

    xs_pad = jnp.pad(x_sample, ((0, 0), (0, t_pad - ts), (0, 0)))
    rows_s = bs * t_pad
    x2s = xs_pad.reshape(rows_s, d)
    ps2 = _proj(x2s, gain_in, w_packed, _tile_for(rows_s, (512, 256, 128, 8)), tn)
    ps3 = ps2.reshape(bs, t_pad, N_PACKED)
    prefix_s = jnp.pad(state_conv[0], ((0, 0), (SUBLANES - (CONV_WIDTH - 1), 0), (0, 0)))
    o_a_s, gdn_s = _gdn(ps3, cw, prefix_s, alog, dtb, ggain, state_gdn[0], tb=t_pad, chunk=t_pad, t_valid=ts)

    pools = [c[0].transpose(0, 2, 3, 1) for c in (cache_k_cmp, cache_v_cmp, cache_k_slc, cache_v_slc)]
    kc_s, vc_s = _cmp_paged(pools[0], pools[1], page_table, cmp_pos_k[0], cmp_pos_v[0], cmp_w_k[0], cmp_w_v[0],
                            k_pages=LANES * CMP_STRIDE // page)
    n_blocks = -(-(past + ts) // SEL_BLOCK)
    nbl = -(-n_blocks // LANES) * LANES
    w_buf = state_win_k.shape[2]
    o_c_s, o_w_s, sel_s, win_k_s, win_v_s = _nsa_sample_a(
        ps3, kc_s, vc_s, state_win_k[0].reshape(bs, w_buf, NSA_KV_DIM), state_win_v[0].reshape(bs, w_buf, NSA_KV_DIM),
        past=past, t_valid=ts, n_blocks=n_blocks, nbl=nbl)
    o_b_s = _nsa_sample_b(ps3, sel_s, o_c_s, o_w_s, pools[2], pools[3], page_table,
                          k_pages=_tile_for(n_pages, (16, 8, 4, 2, 1)),
                          past=past, t_valid=ts, n_blocks=n_blocks, nbl=nbl)
    y_s = _merge(x2s, o_a_s.reshape(rows_s, GDN_DIM), o_b_s.reshape(rows_s, NSA_DIM), ps2, w_a, w_b, w_o, fgain,
                 _tile_for(rows_s, (512, 256, 128, 8)))
    y_sample = y_s.reshape(bs, t_pad, d)[:, :ts]

    keep_s = min(WINDOW, w_buf + ts)
    assert keep_s == w_buf
    sample_state = ([kv_out(ps3, i, 0, ts) for i in range(4)]
                    + [win_k_s.reshape(1, bs, keep_s, NSA_KV_HEADS, NSA_HEAD_DIM),
                       win_v_s.reshape(1, bs, keep_s, NSA_KV_HEADS, NSA_HEAD_DIM),
                       ps3[:, ts - (CONV_WIDTH - 1):ts, :GDN_QKV_DIM][None], gdn_s[None]])
    return (y_prompt, y_sample, *prompt_state, *sample_state)
```

```python
import functools
import math

import jax
import jax.numpy as jnp
from jax import lax
from jax.experimental import pallas as pl
from jax.experimental.pallas import tpu as pltpu

F32 = jnp.float32
BF16 = jnp.bfloat16
HI = lax.Precision.HIGHEST

LANES = 128
SUBLANES = 8
VMEM_LIMIT = 56 * 1024 * 1024

D_MODEL = 1024
GDN_HEADS = 8
GDN_HEAD_DIM = 128
GDN_DIM = GDN_HEADS * GDN_HEAD_DIM
GDN_QKV_DIM = 3 * GDN_DIM
CONV_WIDTH = 4
GDN_CHUNK = 64
NSA_HEADS = 8
NSA_KV_HEADS = 2
NSA_GROUP = NSA_HEADS // NSA_KV_HEADS
NSA_HEAD_DIM = 64
NSA_DIM = NSA_HEADS * NSA_HEAD_DIM
NSA_KV_DIM = NSA_KV_HEADS * NSA_HEAD_DIM
CMP_STRIDE = 16
CMP_LEN = 2 * CMP_STRIDE
SEL_BLOCK = 64
SEL_SHIFT = 6
SEL_RATIO = SEL_BLOCK // CMP_STRIDE
N_SELECT = 16
WINDOW = 512
FORCE_BONUS = 100.0
RMS_EPS = 1e-6
NEG_INF = -1e30
NSA_SCALE = NSA_HEAD_DIM ** -0.5
LOG2E = math.log2(math.e)

COL_QKV = 0
COL_ZA = 3072
COL_GATE_A = 4096
COL_GATE_B = 5120
COL_QB = 6144
COL_ZB = 6656
COL_KV = 7168
COL_SMALL = 7936
N_PACKED = 8192
SMALL_A = 0
SMALL_BT = GDN_HEADS
SMALL_GB = 2 * GDN_HEADS


def _dot(a, b, precision=None):
    return jnp.dot(a, b, precision=precision, preferred_element_type=F32)


def _dot_nt(a, b, precision=None):
    return lax.dot_general(a, b, (((1,), (1,)), ((), ())), precision=precision,
                           preferred_element_type=F32)


def _dot_tn(a, b, precision=None):
    return lax.dot_general(a, b, (((0,), (0,)), ((), ())), precision=precision,
                           preferred_element_type=F32)


def _mm(a, b):
    return jnp.dot(a.astype(BF16), b.astype(BF16), preferred_element_type=F32)


def _silu(x):
    return x * jax.nn.sigmoid(x)


def _iota(shape, dim):
    return lax.broadcasted_iota(jnp.int32, shape, dim)


def _proj_kernel(x_ref, g_ref, w_ref, o_ref, h_ref):
    @pl.when(pl.program_id(1) == 0)
    def _():
        x = x_ref[...]
        y = x * lax.rsqrt(jnp.mean(x * x, axis=-1, keepdims=True) + RMS_EPS)
        h_ref[...] = (y * g_ref[...]).astype(BF16)

    o_ref[...] = jnp.dot(h_ref[...], w_ref[...], preferred_element_type=F32)


def _proj(x2d, gain, w_packed, tm, tn):
    rows, d = x2d.shape
    n = w_packed.shape[1]
    return pl.pallas_call(
        _proj_kernel,
        grid=(rows // tm, n // tn),
        in_specs=[
            pl.BlockSpec((tm, d), lambda i, j: (i, 0)),
            pl.BlockSpec((1, d), lambda i, j: (0, 0)),
            pl.BlockSpec((d, tn), lambda i, j: (0, j)),
        ],
        out_specs=pl.BlockSpec((tm, tn), lambda i, j: (i, j)),
        out_shape=jax.ShapeDtypeStruct((rows, n), F32),
        scratch_shapes=[pltpu.VMEM((tm, d), BF16)],
        compiler_params=pltpu.CompilerParams(
            dimension_semantics=("parallel", "arbitrary"), vmem_limit_bytes=VMEM_LIMIT),
    )(x2d, gain, w_packed)


def _gdn_kernel(q_ref, k_ref, v_ref, z_ref, sm_ref, cw_ref, pre_ref, alog_ref, dtb_ref, gain_ref, s0_ref,
                o_ref, sfin_ref,
                s_sc, tail_sc, q_sc, k_sc, v_sc, gc_sc, b_sc, gct_sc, *, tb, chunk, t_valid):
    t = pl.program_id(1)
    c = chunk
    hd = GDN_HEAD_DIM
    nh = GDN_HEADS

    @pl.when(t == 0)
    def _():
        s_sc[...] = s0_ref[...]
        tail_sc[...] = pre_ref[...]

    for idx, (x_ref, dst) in enumerate(((q_ref, q_sc), (k_ref, k_sc), (v_ref, v_sc))):
        cols = slice(idx * GDN_DIM, (idx + 1) * GDN_DIM)
        x = x_ref[...]
        cw = cw_ref[:, cols]
        last = CONV_WIDTH - 1
        head = jnp.concatenate([tail_sc[:, cols], x[0:SUBLANES]], axis=0)
        tail_sc[:, cols] = x[tb - SUBLANES:tb]
        y = cw[last:last + 1] * head[SUBLANES:2 * SUBLANES]
        for j in range(1, CONV_WIDTH):
            y = y + cw[last - j:last - j + 1] * head[SUBLANES - j:2 * SUBLANES - j]
        if tb > SUBLANES:
            rest = cw[last:last + 1] * x
            for j in range(1, CONV_WIDTH):
                rest = rest + cw[last - j:last - j + 1] * pltpu.roll(x, j, 0)
            y = jnp.concatenate([y, rest[SUBLANES:]], axis=0)
        y = _silu(y)
        if idx == 2:
            dst[...] = y
        else:
            scale = GDN_HEAD_DIM ** -0.5 if idx == 0 else 1.0
            for h in range(nh):
                yh = y[:, h * hd:(h + 1) * hd]
                dst[:, h * hd:(h + 1) * hd] = yh * (lax.rsqrt(jnp.sum(yh * yh, axis=-1, keepdims=True) + 1e-6) * scale)

    sm = sm_ref[...]
    xx = sm + dtb_ref[...]
    softplus = jnp.maximum(xx, 0.0) + jnp.log(1.0 + jnp.exp(-jnp.abs(xx)))
    row = _iota((tb, 1), 0)
    live = t * tb + row < t_valid
    g_small = jnp.where(live, -jnp.exp(alog_ref[...]) * softplus, 0.0)
    beta_small = jnp.where(live, jax.nn.sigmoid(sm), 0.0)
    row_in_chunk = row & (c - 1)
    gc_small = g_small
    shift = 1
    while shift < c:
        gc_small = gc_small + jnp.where(row_in_chunk >= shift, pltpu.roll(gc_small, shift, 0), 0.0)
        shift *= 2
    li = _iota((LANES, GDN_DIM), 0)
    head_of_lane = _iota((LANES, GDN_DIM), 1) >> int(math.log2(hd))

    def spread(x, first_lane):
        onehot = (li == head_of_lane + first_lane).astype(BF16)
        hi = x.astype(BF16)
        r1 = x - hi.astype(F32)
        mid = r1.astype(BF16)
        lo = (r1 - mid.astype(F32)).astype(BF16)
        return _dot(hi, onehot) + _dot(mid, onehot) + _dot(lo, onehot)

    gc_sc[...] = spread(gc_small, SMALL_A)
    b_sc[...] = spread(beta_small, SMALL_BT)
    for r0 in range(0, tb, LANES):
        n = min(LANES, tb - r0)
        part = gc_small[r0:r0 + n]
        if n < LANES:
            part = jnp.concatenate([part, jnp.zeros((LANES - n, LANES), F32)], axis=0)
        part_t = part.T
        for off in range(0, n, c):
            gct_sc[(r0 + off) // c] = part_t[:, off:off + c]

    ri = _iota((c, c), 0)
    ci = _iota((c, c), 1)
    causal = ri >= ci
    strict = ri > ci
    eye_f = (ri == ci).astype(F32)
    n_doubling = int(math.log2(c)) - 1
    gain = gain_ref[...]

    heads = range(nh)
    hs = [slice(h * hd, (h + 1) * hd) for h in heads]
    n_chunks = tb // c
    rw = [slice(ck * c, (ck + 1) * c) for ck in range(n_chunks)]
    pairs = [(ck, h) for ck in range(n_chunks) for h in heads]

    kq = {}
    for ck, h in pairs:
        kk = k_sc[rw[ck], hs[h]].astype(BF16)
        kq[ck, h] = _dot_nt(jnp.concatenate([kk, q_sc[rw[ck], hs[h]].astype(BF16)], axis=0), kk)
    decay = {(ck, h): jnp.where(causal, jnp.exp(gc_sc[rw[ck], hs[h]][:, :c] - gct_sc[ck, h:h + 1, :]), 0.0)
             for ck, h in pairs}
    attn = {p: kq[p][c:] * decay[p] for p in pairs}
    nk = {(ck, h): jnp.where(strict, -(b_sc[rw[ck], hs[h]][:, :c] * kq[ck, h][:c] * decay[ck, h]), 0.0)
          for ck, h in pairs}
    tinv = {p: eye_f + nk[p] for p in pairs}
    for _ in range(n_doubling):
        nk = {p: _mm(nk[p], nk[p]) for p in pairs}
        tinv = {p: tinv[p] + _mm(tinv[p], nk[p]) for p in pairs}
    uw = {}
    for ck, h in pairs:
        bb = b_sc[rw[ck], hs[h]]
        rhs = jnp.concatenate([v_sc[rw[ck], hs[h]] * bb,
                               k_sc[rw[ck], hs[h]] * (bb * jnp.exp(gc_sc[rw[ck], hs[h]]))], axis=1)
        uw[ck, h] = _mm(tinv[ck, h], rhs)

    state = [s_sc[h] for h in heads]
    for ck in range(n_chunks):
        rows = rw[ck]
        ws_qs = []
        for h in heads:
            q_dec = q_sc[rows, hs[h]] * jnp.exp(gc_sc[rows, hs[h]])
            ws_qs.append(_mm(jnp.concatenate([uw[ck, h][:, hd:], q_dec], axis=0), state[h]))
        v_new = [uw[ck, h][:, :hd] - ws_qs[h][:c] for h in heads]
        o = [ws_qs[h][c:] + _mm(attn[ck, h], v_new[h]) for h in heads]
        new_state = []
        for h in heads:
            gc = gc_sc[rows, hs[h]]
            g_last = gc[c - 1:c, :]
            k_dec = k_sc[rows, hs[h]] * jnp.exp(g_last - gc)
            new_state.append(state[h] * jnp.exp(g_last) + _dot_tn(k_dec, v_new[h]))
        state = new_state
        for h in heads:
            on = o[h] * lax.rsqrt(jnp.mean(o[h] * o[h], axis=-1, keepdims=True) + RMS_EPS) * gain
            o_ref[rows, hs[h]] = on * _silu(z_ref[rows, hs[h]])
    for h in heads:
        s_sc[h] = state[h]

    @pl.when(t == pl.num_programs(1) - 1)
    def _():
        sfin_ref[...] = s_sc[...]


def _gdn(p3, conv_w, prefix, alog, dtb, gain, s0, *, tb, chunk, t_valid):
    b, t, _ = p3.shape
    hd = GDN_HEAD_DIM
    gd = GDN_DIM
    const2 = lambda bi, ti: (0, 0)
    kern = functools.partial(_gdn_kernel, tb=tb, chunk=chunk, t_valid=t_valid)
    return pl.pallas_call(
        kern,
        grid=(b, t // tb),
        in_specs=[
            pl.BlockSpec((None, tb, gd), lambda bi, ti: (bi, ti, COL_QKV // gd)),
            pl.BlockSpec((None, tb, gd), lambda bi, ti: (bi, ti, COL_QKV // gd + 1)),
            pl.BlockSpec((None, tb, gd), lambda bi, ti: (bi, ti, COL_QKV // gd + 2)),
            pl.BlockSpec((None, tb, gd), lambda bi, ti: (bi, ti, COL_ZA // gd)),
            pl.BlockSpec((None, tb, LANES), lambda bi, ti: (bi, ti, COL_SMALL // LANES)),
            pl.BlockSpec((CONV_WIDTH, GDN_QKV_DIM), const2),
            pl.BlockSpec((None, SUBLANES, GDN_QKV_DIM), lambda bi, ti: (bi, 0, 0)),
            pl.BlockSpec((1, LANES), const2),
            pl.BlockSpec((1, LANES), const2),
            pl.BlockSpec((1, hd), const2),
            pl.BlockSpec((None, GDN_HEADS, hd, hd), lambda bi, ti: (bi, 0, 0, 0)),
        ],
        out_specs=[
            pl.BlockSpec((None, tb, gd), lambda bi, ti: (bi, ti, 0)),
            pl.BlockSpec((None, GDN_HEADS, hd, hd), lambda bi, ti: (bi, 0, 0, 0)),
        ],
        out_shape=[
            jax.ShapeDtypeStruct((b, t, gd), F32),
            jax.ShapeDtypeStruct((b, GDN_HEADS, hd, hd), F32),
        ],
        scratch_shapes=[
            pltpu.VMEM((GDN_HEADS, hd, hd), F32),
            pltpu.VMEM((SUBLANES, GDN_QKV_DIM), F32),
            pltpu.VMEM((tb, gd), F32),
            pltpu.VMEM((tb, gd), F32),
            pltpu.VMEM((tb, gd), F32),
            pltpu.VMEM((tb, gd), F32),
            pltpu.VMEM((tb, gd), F32),
            pltpu.VMEM((tb // chunk, LANES, chunk), F32),
        ],
        compiler_params=pltpu.CompilerParams(
            dimension_semantics=("parallel", "arbitrary"), vmem_limit_bytes=VMEM_LIMIT),
    )(p3, p3, p3, p3, p3, conv_w, prefix, alog, dtb, gain, s0)


def _masked_softmax2(s, mask):
    s = jnp.where(mask, s, NEG_INF)
    e = jnp.where(mask, jnp.exp2(s - jnp.max(s, axis=-1, keepdims=True)), 0.0)
    return e / jnp.maximum(jnp.sum(e, axis=-1, keepdims=True), 1e-30)


def _stack_heads(q, g):
    hd = NSA_HEAD_DIM
    parts = [q[:, (NSA_GROUP * g + h) * hd:(NSA_GROUP * g + h + 1) * hd] for h in range(NSA_GROUP)]
    return jnp.concatenate(parts, axis=0) * (NSA_SCALE * LOG2E)


def _tile_rows(x):
    return jnp.concatenate([x] * NSA_GROUP, axis=0)


def _group_importance(p_c, tq):
    imp = p_c[0:tq]
    for h in range(1, NSA_GROUP):
        imp = imp + p_c[h * tq:(h + 1) * tq]
    return imp


def _select_blocks(p_c, pos, tq, n_blocks, nbl):
    m = p_c.shape[1]
    imp = _group_importance(p_c, tq)
    pool = ((_iota((m, nbl), 0) >> int(math.log2(SEL_RATIO))) == _iota((m, nbl), 1)).astype(F32)
    imp_sel = _dot(imp, pool, HI)
    lane = _iota((tq, nbl), 1)
    valid = lane * SEL_BLOCK <= pos
    cur = pos >> SEL_SHIFT
    forced = (lane == 0) | (lane == cur) | (lane == cur - 1)
    score = jnp.where(valid, imp_sel + FORCE_BONUS * forced.astype(F32), -1.0)
    score = jnp.where(lane < n_blocks, score, -2.0)
    rank = jnp.zeros((tq, nbl), F32)
    for j in range(n_blocks):
        sj = score[:, j:j + 1]
        rank = rank + ((sj > score) | ((sj == score) & (lane > j))).astype(F32)
    return (rank < float(min(N_SELECT, n_blocks))).astype(F32)


def _select_blocks_t(p_c, q0, tq, n_blocks, nbl):
    m = p_c.shape[1]
    n_sel = min(N_SELECT, n_blocks)
    nbs = -(-n_blocks // SUBLANES) * SUBLANES
    imp = _group_importance(p_c, tq)
    pool_t = ((_iota((nbs, m), 1) >> int(math.log2(SEL_RATIO))) == _iota((nbs, m), 0)).astype(F32)
    imp_t = _dot_nt(pool_t, imp, HI)
    pos_r = q0 + _iota((1, tq), 1)
    blk = _iota((nbs, tq), 0)
    valid = blk * SEL_BLOCK <= pos_r

    def ranked():
        cur = pos_r >> SEL_SHIFT
        forced = (blk == 0) | (blk == cur) | (blk == cur - 1)
        score = jnp.where(valid, imp_t + FORCE_BONUS * forced.astype(F32), -1.0)
        score = jnp.where(blk < n_blocks, score, -2.0)
        rank = jnp.zeros((nbs, tq), F32)
        for j in range(n_blocks):
            sj = score[j:j + 1, :]
            rank = rank + ((sj > score) | ((sj == score) & (blk > j))).astype(F32)
        return (rank < float(n_sel)).astype(F32)

    sel_t = lax.cond(q0 + tq <= n_sel * SEL_BLOCK, lambda: valid.astype(F32), ranked)
    sel_t = jnp.concatenate([sel_t, jnp.zeros((nbl - nbs, tq), F32)], axis=0)
    return sel_t.T


def _compress_rows(x, pw):
    n = x.shape[0] // CMP_STRIDE
    x3 = x.reshape(n, CMP_STRIDE, LANES)
    lo = jnp.sum(x3 * pw[:CMP_STRIDE][None], axis=1)
    hi = jnp.sum(x3 * pw[CMP_STRIDE:][None], axis=1)
    return lo, hi


def _combine_heads(o_ref, g, gates, zb, o_c, o_s, o_w, tq):
    hd = NSA_HEAD_DIM
    for h in range(NSA_GROUP):
        hh = NSA_GROUP * g + h
        c0 = SMALL_GB + 3 * hh
        rows = slice(h * tq, (h + 1) * tq)
        o_h = (gates[:, c0:c0 + 1] * o_c[rows] + gates[:, c0 + 1:c0 + 2] * o_s[rows]
               + gates[:, c0 + 2:c0 + 3] * o_w[rows])
        o_ref[:, hh * hd:(hh + 1) * hd] = o_h * _silu(zb[:, hh * hd:(hh + 1) * hd])


def _nsa_prompt_kernel(q_ref, zb_ref, sm_ref, kcmp_ref, vcmp_ref, kslc_ref, vslc_ref, kwin_ref, vwin_ref,
                       pk_ref, pv_ref, wk_ref, wv_ref, o_ref, kc_sc, vc_sc, k_sc2, vx_sc, *, t_len, tq, tk, span, nbl):
    i = pl.program_id(1)
    m = t_len // CMP_STRIDE
    n_blocks = t_len // SEL_BLOCK
    hd = NSA_HEAD_DIM

    @pl.when(i == 0)
    def _():
        for src, pw_ref, w_ref, dst in ((kcmp_ref, pk_ref, wk_ref, kc_sc), (vcmp_ref, pv_ref, wv_ref, vc_sc)):
            lo, hi = _compress_rows(src[...], pw_ref[...])
            hi_next = pltpu.roll(hi, m - 1, 0)
            pre = jnp.where(_iota((m, LANES), 0) < m - 1, lo + hi_next, 0.0)
            dst[...] = _dot(pre, w_ref[...], HI)
        k_sc2[0] = kslc_ref[...].astype(BF16)
        k_sc2[1] = kwin_ref[...].astype(BF16)
        ones = jnp.ones((t_len, hd), BF16)
        for idx, src in enumerate((vslc_ref, vwin_ref)):
            v = src[...].astype(BF16)
            for g in range(NSA_KV_HEADS):
                vx_sc[idx, g] = jnp.concatenate([v[:, g * hd:(g + 1) * hd], ones], axis=1)

    q0 = i * tq
    pos = q0 + _iota((tq, 1), 0)
    pos4 = _tile_rows(pos)
    q = q_ref[...]
    zb = zb_ref[...]
    gates = jax.nn.sigmoid(sm_ref[...])
    c_end = _iota((1, m), 1) * CMP_STRIDE + (CMP_LEN - 1)
    n_kt = (q0 + tq + tk - 1) // tk
    w_start = pl.multiple_of(jnp.clip(q0 - WINDOW, 0, t_len - span), tq)
    w_dist = pos - (w_start + _iota((1, span), 1))
    w_bias = _tile_rows(jnp.where((w_dist >= 0) & (w_dist <= WINDOW), 0.0, NEG_INF))

    for g in range(NSA_KV_HEADS):
        gs = slice(g * hd, (g + 1) * hd)
        qs = _stack_heads(q, g)
        qs_bf = qs.astype(BF16)
        p_c = _masked_softmax2(_dot_nt(qs, kc_sc[:, gs], HI), c_end <= pos4)
        o_c = _dot(p_c, vc_sc[:, gs])
        sel = _select_blocks_t(p_c, q0, tq, n_blocks, nbl)
        sel_bias = ((sel - 1.0) * -NEG_INF).astype(BF16)

        def sel_body(kt, carry, qs_bf=qs_bf, sel_bias=sel_bias, gs=gs, g=g):
            m_run, acc = carry
            k0 = pl.multiple_of(kt * tk, tk)
            expand = ((k0 + _iota((nbl, tk), 1)) >> SEL_SHIFT) == _iota((nbl, tk), 0)
            bias = _dot(sel_bias, expand.astype(BF16)) + jnp.where(k0 + _iota((1, tk), 1) <= pos, 0.0, NEG_INF)
            s = _dot_nt(qs_bf, k_sc2[0, pl.ds(k0, tk), gs]) + _tile_rows(bias)
            m_new = jnp.maximum(m_run, jnp.max(s, axis=-1, keepdims=True))
            e = jnp.exp2(s - m_new).astype(BF16)
            acc_new = jnp.exp2(m_run - m_new) * acc + _dot(e, vx_sc[0, g, pl.ds(k0, tk), :])
            return m_new, acc_new

        init = (jnp.full((NSA_GROUP * tq, 1), NEG_INF, F32), jnp.zeros((NSA_GROUP * tq, 2 * hd), F32))
        _, acc_s = lax.fori_loop(0, n_kt, sel_body, init)
        o_s = acc_s[:, :hd] / jnp.maximum(acc_s[:, hd:hd + 1], 1e-30)

        s_w = _dot_nt(qs_bf, k_sc2[1, pl.ds(w_start, span), gs]) + w_bias
        e_w = jnp.exp2(s_w - jnp.max(s_w, axis=-1, keepdims=True)).astype(BF16)
        acc_w = _dot(e_w, vx_sc[1, g, pl.ds(w_start, span), :])
        o_w = acc_w[:, :hd] / jnp.maximum(acc_w[:, hd:hd + 1], 1e-30)
        _combine_heads(o_ref, g, gates, zb, o_c, o_s, o_w, tq)


def _nsa_prompt(p3, pos_k, pos_v, w_k, w_v, *, tq, tk):
    b, t, _ = p3.shape
    span = WINDOW + tq
    assert t % tq == 0 and t % tk == 0 and t >= span and t % SEL_BLOCK == 0 and tq % LANES == 0
    assert WINDOW % tq == 0
    m = t // CMP_STRIDE
    nbl = -(-(t // SEL_BLOCK) // LANES) * LANES
    kvc = COL_KV // LANES

    def full(c):
        return pl.BlockSpec((None, t, LANES), lambda bi, i: (bi, 0, c))

    const2 = lambda bi, i: (0, 0)
    kern = functools.partial(_nsa_prompt_kernel, t_len=t, tq=tq, tk=tk, span=span, nbl=nbl)
    return pl.pallas_call(
        kern,
        grid=(b, t // tq),
        in_specs=[
            pl.BlockSpec((None, tq, NSA_DIM), lambda bi, i: (bi, i, COL_QB // NSA_DIM)),
            pl.BlockSpec((None, tq, NSA_DIM), lambda bi, i: (bi, i, COL_ZB // NSA_DIM)),
            pl.BlockSpec((None, tq, LANES), lambda bi, i: (bi, i, COL_SMALL // LANES)),
            full(kvc), full(kvc + 1), full(kvc + 2), full(kvc + 3), full(kvc + 4), full(kvc + 5),
            pl.BlockSpec((CMP_LEN, LANES), const2),
            pl.BlockSpec((CMP_LEN, LANES), const2),
            pl.BlockSpec((LANES, LANES), const2),
            pl.BlockSpec((LANES, LANES), const2),
        ],
        out_specs=pl.BlockSpec((None, tq, NSA_DIM), lambda bi, i: (bi, i, 0)),
        out_shape=jax.ShapeDtypeStruct((b, t, NSA_DIM), F32),
        scratch_shapes=[pltpu.VMEM((m, LANES), F32), pltpu.VMEM((m, LANES), F32),
                        pltpu.VMEM((2, t, LANES), BF16),
                        pltpu.VMEM((2, NSA_KV_HEADS, t, 2 * NSA_HEAD_DIM), BF16)],
        compiler_params=pltpu.CompilerParams(
            dimension_semantics=("parallel", "arbitrary"), vmem_limit_bytes=VMEM_LIMIT),
    )(p3, p3, p3, p3, p3, p3, p3, p3, p3, pos_k, pos_v, w_k, w_v)


def _cmp_paged_kernel(pt_ref, *refs, k_pages):
    del pt_ref
    kp = refs[:k_pages]
    kn = refs[k_pages]
    vp = refs[k_pages + 1:2 * k_pages + 1]
    vn = refs[2 * k_pages + 1]
    pk_ref, pv_ref, wk_ref, wv_ref, kc_ref, vc_ref = refs[2 * k_pages + 2:]
    n = kc_ref.shape[0]
    last_row = _iota((n, LANES), 0) == n - 1

    def row_major(r):
        return r[...].reshape(LANES, LANES).T

    for pages, nxt, pw_ref, w_ref, out in ((kp, kn, pk_ref, wk_ref, kc_ref), (vp, vn, pv_ref, wv_ref, vc_ref)):
        pw = pw_ref[...]
        parts = [_compress_rows(row_major(r), pw) for r in pages]
        lo = jnp.concatenate([p[0] for p in parts], axis=0)
        hi = jnp.concatenate([p[1] for p in parts], axis=0)
        hi_head = jnp.sum(row_major(nxt)[:CMP_STRIDE] * pw[CMP_STRIDE:], axis=0, keepdims=True)
        hi_next = jnp.where(last_row, hi_head, pltpu.roll(hi, n - 1, 0))
        out[...] = _dot(lo + hi_next, w_ref[...], HI)


def _cmp_paged(pool_k, pool_v, page_table, pos_k, pos_v, w_k, w_v, *, k_pages):
    b, n_pages = page_table.shape
    page = pool_k.shape[3]
    per_step = k_pages * page // CMP_STRIDE
    assert page == LANES and n_pages % k_pages == 0
    blk = (None, NSA_KV_HEADS, NSA_HEAD_DIM, page)

    def page_spec(i):
        return pl.BlockSpec(blk, lambda bi, j, pt: (pt[bi, j * k_pages + i], 0, 0, 0))

    next_spec = pl.BlockSpec(
        blk, lambda bi, j, pt: (pt[bi, jnp.minimum((j + 1) * k_pages, n_pages - 1)], 0, 0, 0))
    const2 = lambda bi, j, pt: (0, 0)
    in_specs = ([page_spec(i) for i in range(k_pages)] + [next_spec]
                + [page_spec(i) for i in range(k_pages)] + [next_spec]
                + [pl.BlockSpec((CMP_LEN, LANES), const2)] * 2 + [pl.BlockSpec((LANES, LANES), const2)] * 2)
    out_spec = pl.BlockSpec((None, per_step, LANES), lambda bi, j, pt: (bi, j, 0))
    m = n_pages * page // CMP_STRIDE
    return pl.pallas_call(
        functools.partial(_cmp_paged_kernel, k_pages=k_pages),
        grid_spec=pltpu.PrefetchScalarGridSpec(
            num_scalar_prefetch=1, grid=(b, n_pages // k_pages), in_specs=in_specs,
            out_specs=[out_spec, out_spec]),
        out_shape=[jax.ShapeDtypeStruct((b, m, LANES), F32)] * 2,
        compiler_params=pltpu.CompilerParams(
            dimension_semantics=("parallel", "arbitrary"), vmem_limit_bytes=VMEM_LIMIT),
    )(page_table, *([pool_k] * (k_pages + 1)), *([pool_v] * (k_pages + 1)), pos_k, pos_v, w_k, w_v)


def _new_key_scores(qs, new_k, r, gs):
    return jnp.sum(qs * new_k[r:r + 1, gs], axis=-1, keepdims=True)


def _nsa_sample_a_kernel(q_ref, kc_ref, vc_ref, bk_ref, bv_ref, nk_ref, nv_ref,
                         oc_ref, ow_ref, sel_ref, wk_out, wv_out, *, past, t_valid, tp, n_blocks, nbl):
    hd = NSA_HEAD_DIM
    m = kc_ref.shape[0]
    w_buf = bk_ref.shape[0]
    pos = past + _iota((tp, 1), 0)
    pos4 = _tile_rows(pos)
    q = q_ref[...]
    new_k = nk_ref[...]
    new_v = nv_ref[...]
    c_end = _iota((1, m), 1) * CMP_STRIDE + (CMP_LEN - 1)
    buf_pos = (past - w_buf) + _iota((1, w_buf), 1)
    buf_dist = pos4 - buf_pos
    buf_mask = (buf_dist >= 0) & (buf_dist <= WINDOW)
    for g in range(NSA_KV_HEADS):
        gs = slice(g * hd, (g + 1) * hd)
        qs = _stack_heads(q, g)
        p_c = _masked_softmax2(_dot_nt(qs, kc_ref[:, gs], HI), c_end <= pos4)
        oc_ref[g] = _dot(p_c, vc_ref[:, gs])
        sel_ref[g] = _select_blocks(p_c, pos, tp, n_blocks, nbl)
        s_buf = jnp.where(buf_mask, _dot_nt(qs, bk_ref[:, gs]), NEG_INF)
        s_new, new_mask = [], []
        for r in range(t_valid):
            dist = pos4 - (past + r)
            mk = (dist >= 0) & (dist <= WINDOW)
            new_mask.append(mk)
            s_new.append(jnp.where(mk, _new_key_scores(qs, new_k, r, gs), NEG_INF))
        m_all = jnp.max(s_buf, axis=-1, keepdims=True)
        for sn in s_new:
            m_all = jnp.maximum(m_all, sn)
        e_buf = jnp.where(buf_mask, jnp.exp2(s_buf - m_all), 0.0)
        l_all = jnp.sum(e_buf, axis=-1, keepdims=True)
        acc = _dot(e_buf, bv_ref[:, gs])
        for r in range(t_valid):
            e_r = jnp.where(new_mask[r], jnp.exp2(s_new[r] - m_all), 0.0)
            l_all = l_all + e_r
            acc = acc + e_r * new_v[r:r + 1, gs]
        ow_ref[g] = acc / jnp.maximum(l_all, 1e-30)
    wk_out[0:w_buf - t_valid, :] = bk_ref[t_valid:w_buf, :]
    wk_out[w_buf - t_valid:w_buf, :] = new_k[0:t_valid]
    wv_out[0:w_buf - t_valid, :] = bv_ref[t_valid:w_buf, :]
    wv_out[w_buf - t_valid:w_buf, :] = new_v[0:t_valid]


def _nsa_sample_a(p3, kc, vc, buf_k, buf_v, *, past, t_valid, n_blocks, nbl):
    b, tp, _ = p3.shape
    m = kc.shape[1]
    w_buf = buf_k.shape[1]
    assert w_buf == WINDOW and t_valid <= w_buf
    kvc = COL_KV // LANES
    rows = NSA_GROUP * tp
    kern = functools.partial(_nsa_sample_a_kernel, past=past, t_valid=t_valid, tp=tp,
                             n_blocks=n_blocks, nbl=nbl)
    per_b3 = lambda bi: (bi, 0, 0)
    per_b4 = lambda bi: (bi, 0, 0, 0)
    return pl.pallas_call(
        kern,
        grid=(b,),
        in_specs=[
            pl.BlockSpec((None, tp, NSA_DIM), lambda bi: (bi, 0, COL_QB // NSA_DIM)),
            pl.BlockSpec((None, m, LANES), per_b3),
            pl.BlockSpec((None, m, LANES), per_b3),
            pl.BlockSpec((None, w_buf, LANES), per_b3),
            pl.BlockSpec((None, w_buf, LANES), per_b3),
            pl.BlockSpec((None, tp, LANES), lambda bi: (bi, 0, kvc + 4)),
            pl.BlockSpec((None, tp, LANES), lambda bi: (bi, 0, kvc + 5)),
        ],
        out_specs=[
            pl.BlockSpec((None, NSA_KV_HEADS, rows, NSA_HEAD_DIM), per_b4),
            pl.BlockSpec((None, NSA_KV_HEADS, rows, NSA_HEAD_DIM), per_b4),
            pl.BlockSpec((None, NSA_KV_HEADS, tp, nbl), per_b4),
            pl.BlockSpec((None, w_buf, LANES), per_b3),
            pl.BlockSpec((None, w_buf, LANES), per_b3),
        ],
        out_shape=[
            jax.ShapeDtypeStruct((b, NSA_KV_HEADS, rows, NSA_HEAD_DIM), F32),
            jax.ShapeDtypeStruct((b, NSA_KV_HEADS, rows, NSA_HEAD_DIM), F32),
            jax.ShapeDtypeStruct((b, NSA_KV_HEADS, tp, nbl), F32),
            jax.ShapeDtypeStruct((b, w_buf, LANES), F32),
            jax.ShapeDtypeStruct((b, w_buf, LANES), F32),
        ],
        compiler_params=pltpu.CompilerParams(
            dimension_semantics=("parallel",), vmem_limit_bytes=VMEM_LIMIT),
    )(p3, kc, vc, buf_k, buf_v, p3, p3)


def _nsa_sample_b_kernel(pt_ref, q_ref, zb_ref, sm_ref, nk_ref, nv_ref, sel_ref, selj_ref, exp_ref,
                         oc_ref, ow_ref, *refs, k_pages, past, t_valid, tp, n_blocks, nbl):
    del pt_ref
    kp = refs[:k_pages]
    vp = refs[k_pages:2 * k_pages]
    o_ref, m_sc, l_sc, acc_sc = refs[2 * k_pages:]
    hd = NSA_HEAD_DIM
    j = pl.program_id(1)

    @pl.when(j == 0)
    def _():
        m_sc[...] = jnp.full(m_sc.shape, NEG_INF, F32)
        l_sc[...] = jnp.zeros(l_sc.shape, F32)
        acc_sc[...] = jnp.zeros(acc_sc.shape, F32)

    q = q_ref[...]
    expand = exp_ref[...]
    for g in range(NSA_KV_HEADS):
        qs = _stack_heads(q, g).astype(BF16)
        k_t = jnp.concatenate([r[g] for r in kp], axis=1).astype(BF16)
        v_t = jnp.concatenate([r[g] for r in vp], axis=1).astype(BF16)
        sel_bias = ((selj_ref[g] - 1.0) * -NEG_INF).astype(BF16)
        s = _dot(qs, k_t) + _tile_rows(_dot(sel_bias, expand))
        m_run = m_sc[g]
        m_new = jnp.maximum(m_run, jnp.max(s, axis=-1, keepdims=True))
        alpha = jnp.exp2(m_run - m_new)
        e = jnp.exp2(s - m_new)
        l_sc[g] = alpha * l_sc[g] + jnp.sum(e, axis=-1, keepdims=True)
        acc_sc[g] = alpha * acc_sc[g] + _dot_nt(e.astype(BF16), v_t)
        m_sc[g] = m_new

    @pl.when(j == pl.num_programs(1) - 1)
    def _():
        pos4 = _tile_rows(past + _iota((tp, 1), 0))
        gates = jax.nn.sigmoid(sm_ref[...])
        zb = zb_ref[...]
        new_k = nk_ref[...]
        new_v = nv_ref[...]
        for g in range(NSA_KV_HEADS):
            gs = slice(g * hd, (g + 1) * hd)
            qs = _stack_heads(q, g)
            picked = _tile_rows(sel_ref[g][:, n_blocks - 1:n_blocks] > 0.5)
            m_run, l_run, acc = m_sc[g], l_sc[g], acc_sc[g]
            s_new, new_mask = [], []
            for r in range(t_valid):
                mk = picked & (past + r <= pos4)
                new_mask.append(mk)
                s_new.append(jnp.where(mk, _new_key_scores(qs, new_k, r, gs), NEG_INF))
            m_all = m_run
            for sn in s_new:
                m_all = jnp.maximum(m_all, sn)
            alpha = jnp.exp2(m_run - m_all)
            l_all = alpha * l_run
            acc = alpha * acc
            for r in range(t_valid):
                e_r = jnp.where(new_mask[r], jnp.exp2(s_new[r] - m_all), 0.0)
                l_all = l_all + e_r
                acc = acc + e_r * new_v[r:r + 1, gs]
            o_s = acc / jnp.maximum(l_all, 1e-30)
            _combine_heads(o_ref, g, gates, zb, oc_ref[g], o_s, ow_ref[g], tp)


def _nsa_sample_b(p3, sel, o_c, o_w, pool_k, pool_v, page_table, *, k_pages, past, t_valid, n_blocks, nbl):
    b, tp, _ = p3.shape
    n_pages = page_table.shape[1]
    page = pool_k.shape[3]
    n_steps = n_pages // k_pages
    n_keys = k_pages * page
    per_step = n_keys // SEL_BLOCK
    assert n_pages % k_pages == 0 and page % SEL_BLOCK == 0 and past == n_pages * page
    assert n_blocks == past // SEL_BLOCK + 1 and t_valid <= SEL_BLOCK and per_step <= LANES
    kvc = COL_KV // LANES
    rows = NSA_GROUP * tp
    sel_steps = sel[..., :n_steps * per_step].reshape(b, NSA_KV_HEADS, tp, n_steps, per_step)
    sel_steps = jnp.pad(sel_steps.transpose(0, 3, 1, 2, 4), ((0, 0),) * 4 + ((0, LANES - per_step),))
    expand = (jnp.arange(LANES)[:, None] == (jnp.arange(n_keys) // SEL_BLOCK)[None, :]).astype(BF16)
    blk = (None, NSA_KV_HEADS, NSA_HEAD_DIM, page)

    def page_spec(i):
        return pl.BlockSpec(blk, lambda bi, j, pt: (pt[bi, j * k_pages + i], 0, 0, 0))

    per_b4 = lambda bi, j, pt: (bi, 0, 0, 0)
    in_specs = [
        pl.BlockSpec((None, tp, NSA_DIM), lambda bi, j, pt: (bi, 0, COL_QB // NSA_DIM)),
        pl.BlockSpec((None, tp, NSA_DIM), lambda bi, j, pt: (bi, 0, COL_ZB // NSA_DIM)),
        pl.BlockSpec((None, tp, LANES), lambda bi, j, pt: (bi, 0, COL_SMALL // LANES)),
        pl.BlockSpec((None, tp, LANES), lambda bi, j, pt: (bi, 0, kvc + 2)),
        pl.BlockSpec((None, tp, LANES), lambda bi, j, pt: (bi, 0, kvc + 3)),
        pl.BlockSpec((None, NSA_KV_HEADS, tp, nbl), per_b4),
        pl.BlockSpec((None, None, NSA_KV_HEADS, tp, LANES), lambda bi, j, pt: (bi, j, 0, 0, 0)),
        pl.BlockSpec((LANES, n_keys), lambda bi, j, pt: (0, 0)),
        pl.BlockSpec((None, NSA_KV_HEADS, rows, NSA_HEAD_DIM), per_b4),
        pl.BlockSpec((None, NSA_KV_HEADS, rows, NSA_HEAD_DIM), per_b4),
    ] + [page_spec(i) for i in range(k_pages)] * 2
    kern = functools.partial(_nsa_sample_b_kernel, k_pages=k_pages, past=past, t_valid=t_valid, tp=tp,
                             n_blocks=n_blocks, nbl=nbl)
    return pl.pallas_call(
        kern,
        grid_spec=pltpu.PrefetchScalarGridSpec(
            num_scalar_prefetch=1, grid=(b, n_steps), in_specs=in_specs,
            out_specs=pl.BlockSpec((None, tp, NSA_DIM), lambda bi, j, pt: (bi, 0, 0)),
            scratch_shapes=[pltpu.VMEM((NSA_KV_HEADS, rows, 1), F32), pltpu.VMEM((NSA_KV_HEADS, rows, 1), F32),
                            pltpu.VMEM((NSA_KV_HEADS, rows, NSA_HEAD_DIM), F32)]),
        out_shape=jax.ShapeDtypeStruct((b, tp, NSA_DIM), F32),
        compiler_params=pltpu.CompilerParams(
            dimension_semantics=("parallel", "arbitrary"), vmem_limit_bytes=VMEM_LIMIT),
    )(page_table, p3, p3, p3, p3, p3, sel, sel_steps, expand, o_c, o_w,
      *([pool_k] * k_pages), *([pool_v] * k_pages))


def _merge_kernel(x_ref, oa_ref, ob_ref, ga_ref, gb_ref, wa_ref, wb_ref, wo_ref, fg_ref, y_ref):
    mixed = (jax.nn.sigmoid(ga_ref[...]) * _dot(oa_ref[...].astype(BF16), wa_ref[...])
             + jax.nn.sigmoid(gb_ref[...]) * _dot(ob_ref[...].astype(BF16), wb_ref[...]))
    x_new = x_ref[...] + _dot(mixed.astype(BF16), wo_ref[...])
    y = x_new * lax.rsqrt(jnp.mean(x_new * x_new, axis=-1, keepdims=True) + RMS_EPS)
    y_ref[...] = y * fg_ref[...]


def _merge(x2d, o_a, o_b, p2d, w_a, w_b, w_o, final_gain, tm):
    rows, d = x2d.shape
    const2 = lambda i: (0, 0)
    return pl.pallas_call(
        _merge_kernel,
        grid=(rows // tm,),
        in_specs=[
            pl.BlockSpec((tm, d), lambda i: (i, 0)),
            pl.BlockSpec((tm, GDN_DIM), lambda i: (i, 0)),
            pl.BlockSpec((tm, NSA_DIM), lambda i: (i, 0)),
            pl.BlockSpec((tm, d), lambda i: (i, COL_GATE_A // D_MODEL)),
            pl.BlockSpec((tm, d), lambda i: (i, COL_GATE_B // D_MODEL)),
            pl.BlockSpec((GDN_DIM, d), const2),
            pl.BlockSpec((NSA_DIM, d), const2),
            pl.BlockSpec((d, d), const2),
            pl.BlockSpec((1, d), const2),
        ],
        out_specs=pl.BlockSpec((tm, d), lambda i: (i, 0)),
        out_shape=jax.ShapeDtypeStruct((rows, d), F32),
        compiler_params=pltpu.CompilerParams(
            dimension_semantics=("parallel",), vmem_limit_bytes=VMEM_LIMIT),
    )(x2d, o_a, o_b, p2d, p2d, w_a, w_b, w_o, final_gain)


def _pack_w_in(w_in):
    bounds = [0]
    for width in (GDN_QKV_DIM, GDN_HEADS, GDN_HEADS, GDN_DIM, NSA_DIM, 6 * NSA_KV_DIM, 3 * NSA_HEADS,
                  NSA_DIM, 2 * D_MODEL):
        bounds.append(bounds[-1] + width)
    qkv, a, bt, z_a, q_b, kv_b, g_b, z_b, merge = (w_in[:, bounds[i]:bounds[i + 1]] for i in range(9))
    small = jnp.concatenate([a, bt, g_b], axis=1)
    small = jnp.pad(small, ((0, 0), (0, N_PACKED - COL_SMALL - small.shape[1])))
    packed = jnp.concatenate([qkv, z_a, merge, q_b, z_b, kv_b, small], axis=1)
    assert packed.shape[1] == N_PACKED
    return packed.astype(BF16)


def _lane_vec(v):
    return jnp.pad(v.astype(F32), (0, LANES - v.shape[0])).reshape(1, LANES)


def _block_diag2(w):
    z = jnp.zeros_like(w)
    return jnp.concatenate([jnp.concatenate([w, z], axis=1), jnp.concatenate([z, w], axis=1)], axis=0)


def _tile_for(n, candidates):
    for c in candidates:
        if n % c == 0:
            return c
    raise ValueError(f"no tile for {n}")


def kernel(x_prompt, x_sample, cache_k_cmp, cache_v_cmp, cache_k_slc, cache_v_slc, state_win_k, state_win_v,
           state_conv, state_gdn, page_table, norm_gain, w_in, conv_w, a_log, dt_bias, gdn_gain, cmp_pos_k,
           cmp_w_k, cmp_pos_v, cmp_w_v, w_branch_a, w_branch_b, w_out, final_gain):
    assert w_in.shape[0] == 1, "one layer"
    bp, tp_len, d = x_prompt.shape
    bs, ts, _ = x_sample.shape
    n_pool, page = cache_k_cmp.shape[1:3]
    n_pages = page_table.shape[1]
    past = n_pages * page
    assert past % CMP_STRIDE == 0 and ts < CMP_STRIDE and ts >= CONV_WIDTH - 1 and ts <= SUBLANES
    t_pad = SUBLANES

    w_packed = _pack_w_in(w_in[0])
    gain_in = norm_gain[0].reshape(1, d)
    fgain = final_gain.reshape(1, d)
    w_a = w_branch_a[0].astype(BF16)
    w_b = w_branch_b[0].astype(BF16)
    w_o = w_out[0].astype(BF16)
    cw = conv_w[0]
    alog = _lane_vec(a_log[0])
    dtb = _lane_vec(dt_bias[0])
    ggain = gdn_gain[0].reshape(1, GDN_HEAD_DIM)
    pos_k = jnp.tile(cmp_pos_k[0], (1, NSA_KV_HEADS))
    pos_v = jnp.tile(cmp_pos_v[0], (1, NSA_KV_HEADS))
    wk2 = _block_diag2(cmp_w_k[0])
    wv2 = _block_diag2(cmp_w_v[0])
    tn = _tile_for(N_PACKED, (2048, 1024, 512, 256))
    kvc = COL_KV

    rows_p = bp * tp_len
    x2p = x_prompt.reshape(rows_p, d)
    p2 = _proj(x2p, gain_in, w_packed, _tile_for(rows_p, (1024, 512, 256, 128)), tn)
    p3 = p2.reshape(bp, tp_len, N_PACKED)
    tb = _tile_for(tp_len, (256, 128, 64))
    o_a, gdn_p = _gdn(p3, cw, jnp.zeros((bp, SUBLANES, GDN_QKV_DIM), F32), alog, dtb, ggain,
                      jnp.zeros((bp, GDN_HEADS, GDN_HEAD_DIM, GDN_HEAD_DIM), F32),
                      tb=tb, chunk=GDN_CHUNK, t_valid=tp_len)
    o_b = _nsa_prompt(p3, pos_k, pos_v, wk2, wv2, tq=256, tk=_tile_for(tp_len, (512, 256, 128)))
    y_p = _merge(x2p, o_a.reshape(rows_p, GDN_DIM), o_b.reshape(rows_p, NSA_DIM), p2, w_a, w_b, w_o, fgain,
                 _tile_for(rows_p, (512, 256, 128)))
    y_prompt = y_p.reshape(bp, tp_len, d)

    def kv_out(p, i, t_keep_from, t_to):
        return p[:, t_keep_from:t_to, kvc + i * LANES:kvc + (i + 1) * LANES].reshape(
            1, p.shape[0], t_to - t_keep_from, NSA_KV_HEADS, NSA_HEAD_DIM)

    keep_p = min(WINDOW, tp_len)
    prompt_state = ([kv_out(p3, i, 0, tp_len) for i in range(4)]
                    + [kv_out(p3, i, tp_len - keep_p, tp_len) for i in (4, 5)]
                    + [p3[:, tp_len - (CONV_WIDTH - 1):, :GDN_QKV_DIM][None], gdn_p[None]])

    xs_pad = jnp.pad(x_sample, ((0, 0), (0, t_pad - ts), (0, 0)))
    rows_s = bs * t_pad
    x2s = xs_pad.reshape(rows_s, d)
    ps2 = _proj(x2s, gain_in, w_packed, _tile_for(rows_s, (512, 256, 128, 8)), tn)
    ps3 = ps2.reshape(bs, t_pad, N_PACKED)
    prefix_s = jnp.pad(state_conv[0], ((0, 0), (SUBLANES - (CONV_WIDTH - 1), 0), (0, 0)))
    o_a_s, gdn_s = _gdn(ps3, cw, prefix_s, alog, dtb, ggain, state_gdn[0], tb=t_pad, chunk=t_pad, t_valid=ts)

    pools = [c[0].transpose(0, 2, 3, 1) for c in (cache_k_cmp, cache_v_cmp, cache_k_slc, cache_v_slc)]
    kc_s, vc_s = _cmp_paged(pools[0], pools[1], page_table, pos_k, pos_v, wk2, wv2,
                            k_pages=_tile_for(n_pages, (16, 8, 4, 2, 1)))
    n_blocks = -(-(past + ts) // SEL_BLOCK)
    nbl = -(-n_blocks // LANES) * LANES
    w_buf = state_win_k.shape[2]
    o_c_s, o_w_s, sel_s, win_k_s, win_v_s = _nsa_sample_a(
        ps3, kc_s, vc_s, state_win_k[0].reshape(bs, w_buf, NSA_KV_DIM), state_win_v[0].reshape(bs, w_buf, NSA_KV_DIM),
        past=past, t_valid=ts, n_blocks=n_blocks, nbl=nbl)
    o_b_s = _nsa_sample_b(ps3, sel_s, o_c_s, o_w_s, pools[2], pools[3], page_table,
                          k_pages=_tile_for(n_pages, (16, 8, 4, 2, 1)),
                          past=past, t_valid=ts, n_blocks=n_blocks, nbl=nbl)
    y_s = _merge(x2s, o_a_s.reshape(rows_s, GDN_DIM), o_b_s.reshape(rows_s, NSA_DIM), ps2, w_a, w_b, w_o, fgain,
                 _tile_for(rows_s, (512, 256, 128, 8)))
    y_sample = y_s.reshape(bs, t_pad, d)[:, :ts]

    keep_s = min(WINDOW, w_buf + ts)
    assert keep_s == w_buf
    sample_state = ([kv_out(ps3, i, 0, ts) for i in range(4)]
                    + [win_k_s.reshape(1, bs, keep_s, NSA_KV_HEADS, NSA_HEAD_DIM),
                       win_v_s.reshape(1, bs, keep_s, NSA_KV_HEADS, NSA_HEAD_DIM),
                       ps3[:, ts - (CONV_WIDTH - 1):ts, :GDN_QKV_DIM][None], gdn_s[None]])
    return (y_prompt, y_sample, *prompt_state, *sample_state)
```

```python
import functools
import math

import jax
import jax.numpy as jnp
from jax import lax
from jax.experimental import pallas as pl
from jax.experimental.pallas import tpu as pltpu

F32 = jnp.float32
BF16 = jnp.bfloat16
HI = lax.Precision.HIGHEST

LANES = 128
SUBLANES = 8
VMEM_LIMIT = 56 * 1024 * 1024

D_MODEL = 1024
GDN_HEADS = 8
GDN_HEAD_DIM = 128
GDN_DIM = GDN_HEADS * GDN_HEAD_DIM
GDN_QKV_DIM = 3 * GDN_DIM
CONV_WIDTH = 4
GDN_CHUNK = 64
NSA_HEADS = 8
NSA_KV_HEADS = 2
NSA_GROUP = NSA_HEADS // NSA_KV_HEADS
NSA_HEAD_DIM = 64
NSA_DIM = NSA_HEADS * NSA_HEAD_DIM
NSA_KV_DIM = NSA_KV_HEADS * NSA_HEAD_DIM
CMP_STRIDE = 16
CMP_LEN = 2 * CMP_STRIDE
SEL_BLOCK = 64
SEL_SHIFT = 6
SEL_RATIO = SEL_BLOCK // CMP_STRIDE
N_SELECT = 16
WINDOW = 512
FORCE_BONUS = 100.0
RMS_EPS = 1e-6
NEG_INF = -1e30
NSA_SCALE = NSA_HEAD_DIM ** -0.5
LOG2E = math.log2(math.e)

COL_QKV = 0
COL_ZA = 3072
COL_GATE_A = 4096
COL_GATE_B = 5120
COL_QB = 6144
COL_ZB = 6656
COL_KV = 7168
COL_SMALL = 7936
N_PACKED = 8192
SMALL_A = 0
SMALL_BT = GDN_HEADS
SMALL_GB = 2 * GDN_HEADS


def _dot(a, b, precision=None):
    return jnp.dot(a, b, precision=precision, preferred_element_type=F32)


def _dot_nt(a, b, precision=None):
    return lax.dot_general(a, b, (((1,), (1,)), ((), ())), precision=precision,
                           preferred_element_type=F32)


def _dot_tn(a, b, precision=None):
    return lax.dot_general(a, b, (((0,), (0,)), ((), ())), precision=precision,
                           preferred_element_type=F32)


def _mm(a, b):
    return jnp.dot(a.astype(BF16), b.astype(BF16), preferred_element_type=F32)


def _silu(x):
    return x * jax.nn.sigmoid(x)


def _iota(shape, dim):
    return lax.broadcasted_iota(jnp.int32, shape, dim)


def _proj_kernel(x_ref, g_ref, w_ref, o_ref, h_ref):
    @pl.when(pl.program_id(1) == 0)
    def _():
        x = x_ref[...]
        y = x * lax.rsqrt(jnp.mean(x * x, axis=-1, keepdims=True) + RMS_EPS)
        h_ref[...] = (y * g_ref[...]).astype(BF16)

    o_ref[...] = jnp.dot(h_ref[...], w_ref[...], preferred_element_type=F32)


def _proj(x2d, gain, w_packed, tm, tn):
    rows, d = x2d.shape
    n = w_packed.shape[1]
    return pl.pallas_call(
        _proj_kernel,
        grid=(rows // tm, n // tn),
        in_specs=[
            pl.BlockSpec((tm, d), lambda i, j: (i, 0)),
            pl.BlockSpec((1, d), lambda i, j: (0, 0)),
            pl.BlockSpec((d, tn), lambda i, j: (0, j)),
        ],
        out_specs=pl.BlockSpec((tm, tn), lambda i, j: (i, j)),
        out_shape=jax.ShapeDtypeStruct((rows, n), F32),
        scratch_shapes=[pltpu.VMEM((tm, d), BF16)],
        compiler_params=pltpu.CompilerParams(
            dimension_semantics=("parallel", "arbitrary"), vmem_limit_bytes=VMEM_LIMIT),
    )(x2d, gain, w_packed)


def _gdn_kernel(q_ref, k_ref, v_ref, z_ref, sm_ref, cw_ref, pre_ref, alog_ref, dtb_ref, gain_ref, s0_ref,
                o_ref, sfin_ref,
                s_sc, tail_sc, q_sc, k_sc, v_sc, gc_sc, b_sc, gct_sc, *, tb, chunk, t_valid):
    t = pl.program_id(1)
    c = chunk
    hd = GDN_HEAD_DIM
    nh = GDN_HEADS

    @pl.when(t == 0)
    def _():
        s_sc[...] = s0_ref[...]
        tail_sc[...] = pre_ref[...]

    for idx, (x_ref, dst) in enumerate(((q_ref, q_sc), (k_ref, k_sc), (v_ref, v_sc))):
        cols = slice(idx * GDN_DIM, (idx + 1) * GDN_DIM)
        x = x_ref[...]
        cw = cw_ref[:, cols]
        last = CONV_WIDTH - 1
        head = jnp.concatenate([tail_sc[:, cols], x[0:SUBLANES]], axis=0)
        tail_sc[:, cols] = x[tb - SUBLANES:tb]
        y = cw[last:last + 1] * head[SUBLANES:2 * SUBLANES]
        for j in range(1, CONV_WIDTH):
            y = y + cw[last - j:last - j + 1] * head[SUBLANES - j:2 * SUBLANES - j]
        if tb > SUBLANES:
            rest = cw[last:last + 1] * x
            for j in range(1, CONV_WIDTH):
                rest = rest + cw[last - j:last - j + 1] * pltpu.roll(x, j, 0)
            y = jnp.concatenate([y, rest[SUBLANES:]], axis=0)
        y = _silu(y)
        if idx == 2:
            dst[...] = y
        else:
            scale = GDN_HEAD_DIM ** -0.5 if idx == 0 else 1.0
            for h in range(nh):
                yh = y[:, h * hd:(h + 1) * hd]
                dst[:, h * hd:(h + 1) * hd] = yh * (lax.rsqrt(jnp.sum(yh * yh, axis=-1, keepdims=True) + 1e-6) * scale)

    sm = sm_ref[...]
    xx = sm + dtb_ref[...]
    softplus = jnp.maximum(xx, 0.0) + jnp.log(1.0 + jnp.exp(-jnp.abs(xx)))
    row = _iota((tb, 1), 0)
    live = t * tb + row < t_valid
    g_small = jnp.where(live, -jnp.exp(alog_ref[...]) * softplus, 0.0)
    beta_small = jnp.where(live, jax.nn.sigmoid(sm), 0.0)
    row_in_chunk = row & (c - 1)
    gc_small = g_small
    shift = 1
    while shift < c:
        gc_small = gc_small + jnp.where(row_in_chunk >= shift, pltpu.roll(gc_small, shift, 0), 0.0)
        shift *= 2
    li = _iota((LANES, GDN_DIM), 0)
    head_of_lane = _iota((LANES, GDN_DIM), 1) >> int(math.log2(hd))

    def spread(x, first_lane):
        onehot = (li == head_of_lane + first_lane).astype(BF16)
        hi = x.astype(BF16)
        r1 = x - hi.astype(F32)
        mid = r1.astype(BF16)
        lo = (r1 - mid.astype(F32)).astype(BF16)
        return _dot(hi, onehot) + _dot(mid, onehot) + _dot(lo, onehot)

    gc_sc[...] = spread(gc_small, SMALL_A)
    b_sc[...] = spread(beta_small, SMALL_BT)
    for r0 in range(0, tb, LANES):
        n = min(LANES, tb - r0)
        part = gc_small[r0:r0 + n]
        if n < LANES:
            part = jnp.concatenate([part, jnp.zeros((LANES - n, LANES), F32)], axis=0)
        part_t = part.T
        for off in range(0, n, c):
            gct_sc[(r0 + off) // c] = part_t[:, off:off + c]

    ri = _iota((c, c), 0)
    ci = _iota((c, c), 1)
    causal = ri >= ci
    strict = ri > ci
    eye_f = (ri == ci).astype(F32)
    n_doubling = int(math.log2(c)) - 1
    gain = gain_ref[...]

    heads = range(nh)
    hs = [slice(h * hd, (h + 1) * hd) for h in heads]
    n_chunks = tb // c
    rw = [slice(ck * c, (ck + 1) * c) for ck in range(n_chunks)]
    pairs = [(ck, h) for ck in range(n_chunks) for h in heads]

    kq = {}
    for ck, h in pairs:
        kk = k_sc[rw[ck], hs[h]].astype(BF16)
        kq[ck, h] = _dot_nt(jnp.concatenate([kk, q_sc[rw[ck], hs[h]].astype(BF16)], axis=0), kk)
    decay = {(ck, h): jnp.where(causal, jnp.exp(gc_sc[rw[ck], hs[h]][:, :c] - gct_sc[ck, h:h + 1, :]), 0.0)
             for ck, h in pairs}
    attn = {p: kq[p][c:] * decay[p] for p in pairs}
    nk = {(ck, h): jnp.where(strict, -(b_sc[rw[ck], hs[h]][:, :c] * kq[ck, h][:c] * decay[ck, h]), 0.0)
          for ck, h in pairs}
    tinv = {p: eye_f + nk[p] for p in pairs}
    for _ in range(n_doubling):
        nk = {p: _mm(nk[p], nk[p]) for p in pairs}
        tinv = {p: tinv[p] + _mm(tinv[p], nk[p]) for p in pairs}
    uw = {}
    for ck, h in pairs:
        bb = b_sc[rw[ck], hs[h]]
        rhs = jnp.concatenate([v_sc[rw[ck], hs[h]] * bb,
                               k_sc[rw[ck], hs[h]] * (bb * jnp.exp(gc_sc[rw[ck], hs[h]]))], axis=1)
        uw[ck, h] = _mm(tinv[ck, h], rhs)

    state = [s_sc[h] for h in heads]
    for ck in range(n_chunks):
        rows = rw[ck]
        ws_qs = []
        for h in heads:
            q_dec = q_sc[rows, hs[h]] * jnp.exp(gc_sc[rows, hs[h]])
            ws_qs.append(_mm(jnp.concatenate([uw[ck, h][:, hd:], q_dec], axis=0), state[h]))
        v_new = [uw[ck, h][:, :hd] - ws_qs[h][:c] for h in heads]
        o = [ws_qs[h][c:] + _mm(attn[ck, h], v_new[h]) for h in heads]
        new_state = []
        for h in heads:
            gc = gc_sc[rows, hs[h]]
            g_last = gc[c - 1:c, :]
            k_dec = k_sc[rows, hs[h]] * jnp.exp(g_last - gc)
            new_state.append(state[h] * jnp.exp(g_last) + _dot_tn(k_dec, v_new[h]))
        state = new_state
        for h in heads:
            on = o[h] * lax.rsqrt(jnp.mean(o[h] * o[h], axis=-1, keepdims=True) + RMS_EPS) * gain
            o_ref[rows, hs[h]] = on * _silu(z_ref[rows, hs[h]])
    for h in heads:
        s_sc[h] = state[h]

    @pl.when(t == pl.num_programs(1) - 1)
    def _():
        sfin_ref[...] = s_sc[...]


def _gdn(p3, conv_w, prefix, alog, dtb, gain, s0, *, tb, chunk, t_valid):
    b, t, _ = p3.shape
    hd = GDN_HEAD_DIM
    gd = GDN_DIM
    const2 = lambda bi, ti: (0, 0)
    kern = functools.partial(_gdn_kernel, tb=tb, chunk=chunk, t_valid=t_valid)
    return pl.pallas_call(
        kern,
        grid=(b, t // tb),
        in_specs=[
            pl.BlockSpec((None, tb, gd), lambda bi, ti: (bi, ti, COL_QKV // gd)),
            pl.BlockSpec((None, tb, gd), lambda bi, ti: (bi, ti, COL_QKV // gd + 1)),
            pl.BlockSpec((None, tb, gd), lambda bi, ti: (bi, ti, COL_QKV // gd + 2)),
            pl.BlockSpec((None, tb, gd), lambda bi, ti: (bi, ti, COL_ZA // gd)),
            pl.BlockSpec((None, tb, LANES), lambda bi, ti: (bi, ti, COL_SMALL // LANES)),
            pl.BlockSpec((CONV_WIDTH, GDN_QKV_DIM), const2),
            pl.BlockSpec((None, SUBLANES, GDN_QKV_DIM), lambda bi, ti: (bi, 0, 0)),
            pl.BlockSpec((1, LANES), const2),
            pl.BlockSpec((1, LANES), const2),
            pl.BlockSpec((1, hd), const2),
            pl.BlockSpec((None, GDN_HEADS, hd, hd), lambda bi, ti: (bi, 0, 0, 0)),
        ],
        out_specs=[
            pl.BlockSpec((None, tb, gd), lambda bi, ti: (bi, ti, 0)),
            pl.BlockSpec((None, GDN_HEADS, hd, hd), lambda bi, ti: (bi, 0, 0, 0)),
        ],
        out_shape=[
            jax.ShapeDtypeStruct((b, t, gd), F32),
            jax.ShapeDtypeStruct((b, GDN_HEADS, hd, hd), F32),
        ],
        scratch_shapes=[
            pltpu.VMEM((GDN_HEADS, hd, hd), F32),
            pltpu.VMEM((SUBLANES, GDN_QKV_DIM), F32),
            pltpu.VMEM((tb, gd), F32),
            pltpu.VMEM((tb, gd), F32),
            pltpu.VMEM((tb, gd), F32),
            pltpu.VMEM((tb, gd), F32),
            pltpu.VMEM((tb, gd), F32),
            pltpu.VMEM((tb // chunk, LANES, chunk), F32),
        ],
        compiler_params=pltpu.CompilerParams(
            dimension_semantics=("parallel", "arbitrary"), vmem_limit_bytes=VMEM_LIMIT),
    )(p3, p3, p3, p3, p3, conv_w, prefix, alog, dtb, gain, s0)


def _masked_softmax2(s, mask):
    s = jnp.where(mask, s, NEG_INF)
    e = jnp.where(mask, jnp.exp2(s - jnp.max(s, axis=-1, keepdims=True)), 0.0)
    return e / jnp.maximum(jnp.sum(e, axis=-1, keepdims=True), 1e-30)


def _stack_heads(q, g):
    hd = NSA_HEAD_DIM
    parts = [q[:, (NSA_GROUP * g + h) * hd:(NSA_GROUP * g + h + 1) * hd] for h in range(NSA_GROUP)]
    return jnp.concatenate(parts, axis=0) * (NSA_SCALE * LOG2E)


def _tile_rows(x):
    return jnp.concatenate([x] * NSA_GROUP, axis=0)


def _group_importance(p_c, tq):
    imp = p_c[0:tq]
    for h in range(1, NSA_GROUP):
        imp = imp + p_c[h * tq:(h + 1) * tq]
    return imp


def _select_blocks(p_c, pos, tq, n_blocks, nbl):
    m = p_c.shape[1]
    imp = _group_importance(p_c, tq)
    pool = ((_iota((m, nbl), 0) >> int(math.log2(SEL_RATIO))) == _iota((m, nbl), 1)).astype(F32)
    imp_sel = _dot(imp, pool, HI)
    lane = _iota((tq, nbl), 1)
    valid = lane * SEL_BLOCK <= pos
    cur = pos >> SEL_SHIFT
    forced = (lane == 0) | (lane == cur) | (lane == cur - 1)
    score = jnp.where(valid, imp_sel + FORCE_BONUS * forced.astype(F32), -1.0)
    score = jnp.where(lane < n_blocks, score, -2.0)
    rank = jnp.zeros((tq, nbl), F32)
    for j in range(n_blocks):
        sj = score[:, j:j + 1]
        rank = rank + ((sj > score) | ((sj == score) & (lane > j))).astype(F32)
    return (rank < float(min(N_SELECT, n_blocks))).astype(F32)


def _select_blocks_t(p_c, q0, tq, n_blocks, nbl):
    m = p_c.shape[1]
    n_sel = min(N_SELECT, n_blocks)
    nbs = -(-n_blocks // SUBLANES) * SUBLANES
    imp = _group_importance(p_c, tq)
    pool_t = ((_iota((nbs, m), 1) >> int(math.log2(SEL_RATIO))) == _iota((nbs, m), 0)).astype(F32)
    imp_t = _dot_nt(pool_t, imp, HI)
    pos_r = q0 + _iota((1, tq), 1)
    blk = _iota((nbs, tq), 0)
    valid = blk * SEL_BLOCK <= pos_r

    def ranked():
        cur = pos_r >> SEL_SHIFT
        forced = (blk == 0) | (blk == cur) | (blk == cur - 1)
        score = jnp.where(valid, imp_t + FORCE_BONUS * forced.astype(F32), -1.0)
        score = jnp.where(blk < n_blocks, score, -2.0)
        rank = jnp.zeros((nbs, tq), F32)
        for j in range(n_blocks):
            sj = score[j:j + 1, :]
            rank = rank + ((sj > score) | ((sj == score) & (blk > j))).astype(F32)
        return (rank < float(n_sel)).astype(F32)

    sel_t = lax.cond(q0 + tq <= n_sel * SEL_BLOCK, lambda: valid.astype(F32), ranked)
    sel_t = jnp.concatenate([sel_t, jnp.zeros((nbl - nbs, tq), F32)], axis=0)
    return sel_t.T


def _compress_rows(x, pw):
    n = x.shape[0] // CMP_STRIDE
    x3 = x.reshape(n, CMP_STRIDE, LANES)
    lo = jnp.sum(x3 * pw[:CMP_STRIDE][None], axis=1)
    hi = jnp.sum(x3 * pw[CMP_STRIDE:][None], axis=1)
    return lo, hi


def _combine_heads(o_ref, g, gates, zb, o_c, o_s, o_w, tq):
    hd = NSA_HEAD_DIM
    for h in range(NSA_GROUP):
        hh = NSA_GROUP * g + h
        c0 = SMALL_GB + 3 * hh
        rows = slice(h * tq, (h + 1) * tq)
        o_h = (gates[:, c0:c0 + 1] * o_c[rows] + gates[:, c0 + 1:c0 + 2] * o_s[rows]
               + gates[:, c0 + 2:c0 + 3] * o_w[rows])
        o_ref[:, hh * hd:(hh + 1) * hd] = o_h * _silu(zb[:, hh * hd:(hh + 1) * hd])


def _nsa_prompt_kernel(q_ref, zb_ref, sm_ref, kcmp_ref, vcmp_ref, kslc_ref, vslc_ref, kwin_ref, vwin_ref,
                       pk_ref, pv_ref, wk_ref, wv_ref, o_ref, kc_sc, vc_sc, kx_sc, kw_sc, vx_sc, *,
                       t_len, tq, tk, span, nbl, nbo):
    i = pl.program_id(1)
    m = t_len // CMP_STRIDE
    n_blocks = t_len // SEL_BLOCK
    hd = NSA_HEAD_DIM

    @pl.when(i == 0)
    def _():
        for src, pw_ref, w_ref, dst in ((kcmp_ref, pk_ref, wk_ref, kc_sc), (vcmp_ref, pv_ref, wv_ref, vc_sc)):
            lo, hi = _compress_rows(src[...], pw_ref[...])
            hi_next = pltpu.roll(hi, m - 1, 0)
            pre = jnp.where(_iota((m, LANES), 0) < m - 1, lo + hi_next, 0.0)
            dst[...] = _dot(pre, w_ref[...], HI)
        k_slc = kslc_ref[...].astype(BF16)
        blk_onehot = ((_iota((t_len, nbo), 0) >> SEL_SHIFT) == _iota((t_len, nbo), 1)).astype(BF16)
        for g in range(NSA_KV_HEADS):
            kx_sc[g] = jnp.concatenate([k_slc[:, g * hd:(g + 1) * hd], blk_onehot], axis=1)
        kw_sc[...] = kwin_ref[...].astype(BF16)
        ones = jnp.ones((t_len, hd), BF16)
        for idx, src in enumerate((vslc_ref, vwin_ref)):
            v = src[...].astype(BF16)
            for g in range(NSA_KV_HEADS):
                vx_sc[idx, g] = jnp.concatenate([v[:, g * hd:(g + 1) * hd], ones], axis=1)

    q0 = i * tq
    pos = q0 + _iota((tq, 1), 0)
    pos4 = _tile_rows(pos)
    q = q_ref[...]
    zb = zb_ref[...]
    gates = jax.nn.sigmoid(sm_ref[...])
    c_end = _iota((1, m), 1) * CMP_STRIDE + (CMP_LEN - 1)
    n_full = q0 // tk
    w_start = pl.multiple_of(jnp.clip(q0 - WINDOW, 0, t_len - span), tq)
    w_dist = pos - (w_start + _iota((1, span), 1))
    w_bias = _tile_rows(jnp.where((w_dist >= 0) & (w_dist <= WINDOW), 0.0, NEG_INF))

    for g in range(NSA_KV_HEADS):
        gs = slice(g * hd, (g + 1) * hd)
        qs = _stack_heads(q, g)
        qs_bf = qs.astype(BF16)
        p_c = _masked_softmax2(_dot_nt(qs, kc_sc[:, gs], HI), c_end <= pos4)
        o_c = _dot(p_c, vc_sc[:, gs])
        sel = _select_blocks_t(p_c, q0, tq, n_blocks, nbl)
        sel_bias = ((sel[:, :nbo] - 1.0) * -NEG_INF).astype(BF16)
        q_aug = jnp.concatenate([qs_bf, _tile_rows(sel_bias)], axis=1)

        def sel_tile(kt, carry, causal_tile, q_aug=q_aug, g=g):
            m_run, acc = carry
            k0 = pl.multiple_of(kt * tk, tk)
            s = _dot_nt(q_aug, kx_sc[g, pl.ds(k0, tk), :])
            if causal_tile:
                s = s + _tile_rows(jnp.where(k0 + _iota((1, tk), 1) <= pos, 0.0, NEG_INF))
            m_new = jnp.maximum(m_run, jnp.max(s, axis=-1, keepdims=True))
            e = jnp.exp2(s - m_new).astype(BF16)
            acc_new = jnp.exp2(m_run - m_new) * acc + _dot(e, vx_sc[0, g, pl.ds(k0, tk), :])
            return m_new, acc_new

        init = (jnp.full((NSA_GROUP * tq, 1), NEG_INF, F32), jnp.zeros((NSA_GROUP * tq, 2 * hd), F32))
        carry = lax.fori_loop(0, n_full, functools.partial(sel_tile, causal_tile=False), init)
        _, acc_s = sel_tile(n_full, carry, True)
        o_s = acc_s[:, :hd] / jnp.maximum(acc_s[:, hd:hd + 1], 1e-30)

        s_w = _dot_nt(qs_bf, kw_sc[pl.ds(w_start, span), gs]) + w_bias
        e_w = jnp.exp2(s_w - jnp.max(s_w, axis=-1, keepdims=True)).astype(BF16)
        acc_w = _dot(e_w, vx_sc[1, g, pl.ds(w_start, span), :])
        o_w = acc_w[:, :hd] / jnp.maximum(acc_w[:, hd:hd + 1], 1e-30)
        _combine_heads(o_ref, g, gates, zb, o_c, o_s, o_w, tq)


def _nsa_prompt(p3, pos_k, pos_v, w_k, w_v, *, tq, tk):
    b, t, _ = p3.shape
    span = WINDOW + tq
    assert t % tq == 0 and t % tk == 0 and t >= span and t % SEL_BLOCK == 0 and tq % LANES == 0
    assert WINDOW % tq == 0 and tk % tq == 0
    nbo = -(-(t // SEL_BLOCK) // NSA_HEAD_DIM) * NSA_HEAD_DIM
    assert (NSA_HEAD_DIM + nbo) % LANES == 0
    m = t // CMP_STRIDE
    nbl = -(-(t // SEL_BLOCK) // LANES) * LANES
    kvc = COL_KV // LANES

    def full(c):
        return pl.BlockSpec((None, t, LANES), lambda bi, i: (bi, 0, c))

    const2 = lambda bi, i: (0, 0)
    kern = functools.partial(_nsa_prompt_kernel, t_len=t, tq=tq, tk=tk, span=span, nbl=nbl, nbo=nbo)
    return pl.pallas_call(
        kern,
        grid=(b, t // tq),
        in_specs=[
            pl.BlockSpec((None, tq, NSA_DIM), lambda bi, i: (bi, i, COL_QB // NSA_DIM)),
            pl.BlockSpec((None, tq, NSA_DIM), lambda bi, i: (bi, i, COL_ZB // NSA_DIM)),
            pl.BlockSpec((None, tq, LANES), lambda bi, i: (bi, i, COL_SMALL // LANES)),
            full(kvc), full(kvc + 1), full(kvc + 2), full(kvc + 3), full(kvc + 4), full(kvc + 5),
            pl.BlockSpec((CMP_LEN, LANES), const2),
            pl.BlockSpec((CMP_LEN, LANES), const2),
            pl.BlockSpec((LANES, LANES), const2),
            pl.BlockSpec((LANES, LANES), const2),
        ],
        out_specs=pl.BlockSpec((None, tq, NSA_DIM), lambda bi, i: (bi, i, 0)),
        out_shape=jax.ShapeDtypeStruct((b, t, NSA_DIM), F32),
        scratch_shapes=[pltpu.VMEM((m, LANES), F32), pltpu.VMEM((m, LANES), F32),
                        pltpu.VMEM((NSA_KV_HEADS, t, NSA_HEAD_DIM + nbo), BF16),
                        pltpu.VMEM((t, LANES), BF16),
                        pltpu.VMEM((2, NSA_KV_HEADS, t, 2 * NSA_HEAD_DIM), BF16)],
        compiler_params=pltpu.CompilerParams(
            dimension_semantics=("parallel", "arbitrary"), vmem_limit_bytes=VMEM_LIMIT),
    )(p3, p3, p3, p3, p3, p3, p3, p3, p3, pos_k, pos_v, w_k, w_v)


def _cmp_paged_kernel(pt_ref, *refs, k_pages):
    del pt_ref
    kp = refs[:k_pages]
    kn = refs[k_pages]
    vp = refs[k_pages + 1:2 * k_pages + 1]
    vn = refs[2 * k_pages + 1]
    pk_ref, pv_ref, wk_ref, wv_ref, kc_ref, vc_ref = refs[2 * k_pages + 2:]
    n = kc_ref.shape[0]
    last_row = _iota((n, LANES), 0) == n - 1

    def row_major(r):
        return r[...].reshape(LANES, LANES).T

    for pages, nxt, pw_ref, w_ref, out in ((kp, kn, pk_ref, wk_ref, kc_ref), (vp, vn, pv_ref, wv_ref, vc_ref)):
        pw = pw_ref[...]
        parts = [_compress_rows(row_major(r), pw) for r in pages]
        lo = jnp.concatenate([p[0] for p in parts], axis=0)
        hi = jnp.concatenate([p[1] for p in parts], axis=0)
        hi_head = jnp.sum(row_major(nxt)[:CMP_STRIDE] * pw[CMP_STRIDE:], axis=0, keepdims=True)
        hi_next = jnp.where(last_row, hi_head, pltpu.roll(hi, n - 1, 0))
        out[...] = _dot(lo + hi_next, w_ref[...], HI)


def _cmp_paged(pool_k, pool_v, page_table, pos_k, pos_v, w_k, w_v, *, k_pages):
    b, n_pages = page_table.shape
    page = pool_k.shape[3]
    per_step = k_pages * page // CMP_STRIDE
    assert page == LANES and n_pages % k_pages == 0
    blk = (None, NSA_KV_HEADS, NSA_HEAD_DIM, page)

    def page_spec(i):
        return pl.BlockSpec(blk, lambda bi, j, pt: (pt[bi, j * k_pages + i], 0, 0, 0))

    next_spec = pl.BlockSpec(
        blk, lambda bi, j, pt: (pt[bi, jnp.minimum((j + 1) * k_pages, n_pages - 1)], 0, 0, 0))
    const2 = lambda bi, j, pt: (0, 0)
    in_specs = ([page_spec(i) for i in range(k_pages)] + [next_spec]
                + [page_spec(i) for i in range(k_pages)] + [next_spec]
                + [pl.BlockSpec((CMP_LEN, LANES), const2)] * 2 + [pl.BlockSpec((LANES, LANES), const2)] * 2)
    out_spec = pl.BlockSpec((None, per_step, LANES), lambda bi, j, pt: (bi, j, 0))
    m = n_pages * page // CMP_STRIDE
    return pl.pallas_call(
        functools.partial(_cmp_paged_kernel, k_pages=k_pages),
        grid_spec=pltpu.PrefetchScalarGridSpec(
            num_scalar_prefetch=1, grid=(b, n_pages // k_pages), in_specs=in_specs,
            out_specs=[out_spec, out_spec]),
        out_shape=[jax.ShapeDtypeStruct((b, m, LANES), F32)] * 2,
        compiler_params=pltpu.CompilerParams(
            dimension_semantics=("parallel", "arbitrary"), vmem_limit_bytes=VMEM_LIMIT),
    )(page_table, *([pool_k] * (k_pages + 1)), *([pool_v] * (k_pages + 1)), pos_k, pos_v, w_k, w_v)


def _new_key_scores(qs, new_k, r, gs):
    return jnp.sum(qs * new_k[r:r + 1, gs], axis=-1, keepdims=True)


def _nsa_sample_a_kernel(q_ref, kc_ref, vc_ref, bk_ref, bv_ref, nk_ref, nv_ref,
                         oc_ref, ow_ref, sel_ref, wk_out, wv_out, *, past, t_valid, tp, n_blocks, nbl):
    hd = NSA_HEAD_DIM
    m = kc_ref.shape[0]
    w_buf = bk_ref.shape[0]
    pos = past + _iota((tp, 1), 0)
    pos4 = _tile_rows(pos)
    q = q_ref[...]
    new_k = nk_ref[...]
    new_v = nv_ref[...]
    c_end = _iota((1, m), 1) * CMP_STRIDE + (CMP_LEN - 1)
    buf_pos = (past - w_buf) + _iota((1, w_buf), 1)
    buf_dist = pos4 - buf_pos
    buf_mask = (buf_dist >= 0) & (buf_dist <= WINDOW)
    for g in range(NSA_KV_HEADS):
        gs = slice(g * hd, (g + 1) * hd)
        qs = _stack_heads(q, g)
        p_c = _masked_softmax2(_dot_nt(qs, kc_ref[:, gs], HI), c_end <= pos4)
        oc_ref[g] = _dot(p_c, vc_ref[:, gs])
        sel_ref[g] = _select_blocks(p_c, pos, tp, n_blocks, nbl)
        s_buf = jnp.where(buf_mask, _dot_nt(qs, bk_ref[:, gs]), NEG_INF)
        s_new, new_mask = [], []
        for r in range(t_valid):
            dist = pos4 - (past + r)
            mk = (dist >= 0) & (dist <= WINDOW)
            new_mask.append(mk)
            s_new.append(jnp.where(mk, _new_key_scores(qs, new_k, r, gs), NEG_INF))
        m_all = jnp.max(s_buf, axis=-1, keepdims=True)
        for sn in s_new:
            m_all = jnp.maximum(m_all, sn)
        e_buf = jnp.where(buf_mask, jnp.exp2(s_buf - m_all), 0.0)
        l_all = jnp.sum(e_buf, axis=-1, keepdims=True)
        acc = _dot(e_buf, bv_ref[:, gs])
        for r in range(t_valid):
            e_r = jnp.where(new_mask[r], jnp.exp2(s_new[r] - m_all), 0.0)
            l_all = l_all + e_r
            acc = acc + e_r * new_v[r:r + 1, gs]
        ow_ref[g] = acc / jnp.maximum(l_all, 1e-30)
    wk_out[0:w_buf - t_valid, :] = bk_ref[t_valid:w_buf, :]
    wk_out[w_buf - t_valid:w_buf, :] = new_k[0:t_valid]
    wv_out[0:w_buf - t_valid, :] = bv_ref[t_valid:w_buf, :]
    wv_out[w_buf - t_valid:w_buf, :] = new_v[0:t_valid]


def _nsa_sample_a(p3, kc, vc, buf_k, buf_v, *, past, t_valid, n_blocks, nbl):
    b, tp, _ = p3.shape
    m = kc.shape[1]
    w_buf = buf_k.shape[1]
    assert w_buf == WINDOW and t_valid <= w_buf
    kvc = COL_KV // LANES
    rows = NSA_GROUP * tp
    kern = functools.partial(_nsa_sample_a_kernel, past=past, t_valid=t_valid, tp=tp,
                             n_blocks=n_blocks, nbl=nbl)
    per_b3 = lambda bi: (bi, 0, 0)
    per_b4 = lambda bi: (bi, 0, 0, 0)
    return pl.pallas_call(
        kern,
        grid=(b,),
        in_specs=[
            pl.BlockSpec((None, tp, NSA_DIM), lambda bi: (bi, 0, COL_QB // NSA_DIM)),
            pl.BlockSpec((None, m, LANES), per_b3),
            pl.BlockSpec((None, m, LANES), per_b3),
            pl.BlockSpec((None, w_buf, LANES), per_b3),
            pl.BlockSpec((None, w_buf, LANES), per_b3),
            pl.BlockSpec((None, tp, LANES), lambda bi: (bi, 0, kvc + 4)),
            pl.BlockSpec((None, tp, LANES), lambda bi: (bi, 0, kvc + 5)),
        ],
        out_specs=[
            pl.BlockSpec((None, NSA_KV_HEADS, rows, NSA_HEAD_DIM), per_b4),
            pl.BlockSpec((None, NSA_KV_HEADS, rows, NSA_HEAD_DIM), per_b4),
            pl.BlockSpec((None, NSA_KV_HEADS, tp, nbl), per_b4),
            pl.BlockSpec((None, w_buf, LANES), per_b3),
            pl.BlockSpec((None, w_buf, LANES), per_b3),
        ],
        out_shape=[
            jax.ShapeDtypeStruct((b, NSA_KV_HEADS, rows, NSA_HEAD_DIM), F32),
            jax.ShapeDtypeStruct((b, NSA_KV_HEADS, rows, NSA_HEAD_DIM), F32),
            jax.ShapeDtypeStruct((b, NSA_KV_HEADS, tp, nbl), F32),
            jax.ShapeDtypeStruct((b, w_buf, LANES), F32),
            jax.ShapeDtypeStruct((b, w_buf, LANES), F32),
        ],
        compiler_params=pltpu.CompilerParams(
            dimension_semantics=("parallel",), vmem_limit_bytes=VMEM_LIMIT),
    )(p3, kc, vc, buf_k, buf_v, p3, p3)


def _nsa_sample_b_kernel(pt_ref, q_ref, zb_ref, sm_ref, nk_ref, nv_ref, sel_ref, selj_ref, exp_ref,
                         oc_ref, ow_ref, *refs, k_pages, past, t_valid, tp, n_blocks, nbl):
    del pt_ref
    kp = refs[:k_pages]
    vp = refs[k_pages:2 * k_pages]
    o_ref, m_sc, l_sc, acc_sc = refs[2 * k_pages:]
    hd = NSA_HEAD_DIM
    j = pl.program_id(1)

    @pl.when(j == 0)
    def _():
        m_sc[...] = jnp.full(m_sc.shape, NEG_INF, F32)
        l_sc[...] = jnp.zeros(l_sc.shape, F32)
        acc_sc[...] = jnp.zeros(acc_sc.shape, F32)

    q = q_ref[...]
    expand = exp_ref[...]
    groups = range(NSA_KV_HEADS)
    s = []
    for g in groups:
        k_t = jnp.concatenate([r[g] for r in kp], axis=1).astype(BF16)
        sel_bias = ((selj_ref[g] - 1.0) * -NEG_INF).astype(BF16)
        s.append(_dot(_stack_heads(q, g).astype(BF16), k_t) + _tile_rows(_dot(sel_bias, expand)))
    m_run = [m_sc[g] for g in groups]
    m_new = [jnp.maximum(m_run[g], jnp.max(s[g], axis=-1, keepdims=True)) for g in groups]
    alpha = [jnp.exp2(m_run[g] - m_new[g]) for g in groups]
    e = [jnp.exp2(s[g] - m_new[g]) for g in groups]
    l_new = [alpha[g] * l_sc[g] + jnp.sum(e[g], axis=-1, keepdims=True) for g in groups]
    acc_new = []
    for g in groups:
        v_t = jnp.concatenate([r[g] for r in vp], axis=1).astype(BF16)
        acc_new.append(alpha[g] * acc_sc[g] + _dot_nt(e[g].astype(BF16), v_t))
    for g in groups:
        m_sc[g] = m_new[g]
        l_sc[g] = l_new[g]
        acc_sc[g] = acc_new[g]

    @pl.when(j == pl.num_programs(1) - 1)
    def _():
        pos4 = _tile_rows(past + _iota((tp, 1), 0))
        gates = jax.nn.sigmoid(sm_ref[...])
        zb = zb_ref[...]
        new_k = nk_ref[...]
        new_v = nv_ref[...]
        for g in range(NSA_KV_HEADS):
            gs = slice(g * hd, (g + 1) * hd)
            qs = _stack_heads(q, g)
            picked = _tile_rows(sel_ref[g][:, n_blocks - 1:n_blocks] > 0.5)
            m_run, l_run, acc = m_sc[g], l_sc[g], acc_sc[g]
            s_new, new_mask = [], []
            for r in range(t_valid):
                mk = picked & (past + r <= pos4)
                new_mask.append(mk)
                s_new.append(jnp.where(mk, _new_key_scores(qs, new_k, r, gs), NEG_INF))
            m_all = m_run
            for sn in s_new:
                m_all = jnp.maximum(m_all, sn)
            alpha = jnp.exp2(m_run - m_all)
            l_all = alpha * l_run
            acc = alpha * acc
            for r in range(t_valid):
                e_r = jnp.where(new_mask[r], jnp.exp2(s_new[r] - m_all), 0.0)
                l_all = l_all + e_r
                acc = acc + e_r * new_v[r:r + 1, gs]
            o_s = acc / jnp.maximum(l_all, 1e-30)
            _combine_heads(o_ref, g, gates, zb, oc_ref[g], o_s, ow_ref[g], tp)


def _nsa_sample_b(p3, sel, o_c, o_w, pool_k, pool_v, page_table, *, k_pages, past, t_valid, n_blocks, nbl):
    b, tp, _ = p3.shape
    n_pages = page_table.shape[1]
    page = pool_k.shape[3]
    n_steps = n_pages // k_pages
    n_keys = k_pages * page
    per_step = n_keys // SEL_BLOCK
    assert n_pages % k_pages == 0 and page % SEL_BLOCK == 0 and past == n_pages * page
    assert n_blocks == past // SEL_BLOCK + 1 and t_valid <= SEL_BLOCK and per_step <= LANES
    kvc = COL_KV // LANES
    rows = NSA_GROUP * tp
    sel_steps = sel[..., :n_steps * per_step].reshape(b, NSA_KV_HEADS, tp, n_steps, per_step)
    sel_steps = jnp.pad(sel_steps.transpose(0, 3, 1, 2, 4), ((0, 0),) * 4 + ((0, LANES - per_step),))
    expand = (jnp.arange(LANES)[:, None] == (jnp.arange(n_keys) // SEL_BLOCK)[None, :]).astype(BF16)
    blk = (None, NSA_KV_HEADS, NSA_HEAD_DIM, page)

    def page_spec(i):
        return pl.BlockSpec(blk, lambda bi, j, pt: (pt[bi, j * k_pages + i], 0, 0, 0))

    per_b4 = lambda bi, j, pt: (bi, 0, 0, 0)
    in_specs = [
        pl.BlockSpec((None, tp, NSA_DIM), lambda bi, j, pt: (bi, 0, COL_QB // NSA_DIM)),
        pl.BlockSpec((None, tp, NSA_DIM), lambda bi, j, pt: (bi, 0, COL_ZB // NSA_DIM)),
        pl.BlockSpec((None, tp, LANES), lambda bi, j, pt: (bi, 0, COL_SMALL // LANES)),
        pl.BlockSpec((None, tp, LANES), lambda bi, j, pt: (bi, 0, kvc + 2)),
        pl.BlockSpec((None, tp, LANES), lambda bi, j, pt: (bi, 0, kvc + 3)),
        pl.BlockSpec((None, NSA_KV_HEADS, tp, nbl), per_b4),
        pl.BlockSpec((None, None, NSA_KV_HEADS, tp, LANES), lambda bi, j, pt: (bi, j, 0, 0, 0)),
        pl.BlockSpec((LANES, n_keys), lambda bi, j, pt: (0, 0)),
        pl.BlockSpec((None, NSA_KV_HEADS, rows, NSA_HEAD_DIM), per_b4),
        pl.BlockSpec((None, NSA_KV_HEADS, rows, NSA_HEAD_DIM), per_b4),
    ] + [page_spec(i) for i in range(k_pages)] * 2
    kern = functools.partial(_nsa_sample_b_kernel, k_pages=k_pages, past=past, t_valid=t_valid, tp=tp,
                             n_blocks=n_blocks, nbl=nbl)
    return pl.pallas_call(
        kern,
        grid_spec=pltpu.PrefetchScalarGridSpec(
            num_scalar_prefetch=1, grid=(b, n_steps), in_specs=in_specs,
            out_specs=pl.BlockSpec((None, tp, NSA_DIM), lambda bi, j, pt: (bi, 0, 0)),
            scratch_shapes=[pltpu.VMEM((NSA_KV_HEADS, rows, 1), F32), pltpu.VMEM((NSA_KV_HEADS, rows, 1), F32),
                            pltpu.VMEM((NSA_KV_HEADS, rows, NSA_HEAD_DIM), F32)]),
        out_shape=jax.ShapeDtypeStruct((b, tp, NSA_DIM), F32),
        compiler_params=pltpu.CompilerParams(
            dimension_semantics=("parallel", "arbitrary"), vmem_limit_bytes=VMEM_LIMIT),
    )(page_table, p3, p3, p3, p3, p3, sel, sel_steps, expand, o_c, o_w,
      *([pool_k] * k_pages), *([pool_v] * k_pages))


def _merge_kernel(x_ref, oa_ref, ob_ref, ga_ref, gb_ref, wa_ref, wb_ref, wo_ref, fg_ref, y_ref):
    mixed = (jax.nn.sigmoid(ga_ref[...]) * _dot(oa_ref[...].astype(BF16), wa_ref[...])
             + jax.nn.sigmoid(gb_ref[...]) * _dot(ob_ref[...].astype(BF16), wb_ref[...]))
    x_new = x_ref[...] + _dot(mixed.astype(BF16), wo_ref[...])
    y = x_new * lax.rsqrt(jnp.mean(x_new * x_new, axis=-1, keepdims=True) + RMS_EPS)
    y_ref[...] = y * fg_ref[...]


def _merge(x2d, o_a, o_b, p2d, w_a, w_b, w_o, final_gain, tm):
    rows, d = x2d.shape
    const2 = lambda i: (0, 0)
    return pl.pallas_call(
        _merge_kernel,
        grid=(rows // tm,),
        in_specs=[
            pl.BlockSpec((tm, d), lambda i: (i, 0)),
            pl.BlockSpec((tm, GDN_DIM), lambda i: (i, 0)),
            pl.BlockSpec((tm, NSA_DIM), lambda i: (i, 0)),
            pl.BlockSpec((tm, d), lambda i: (i, COL_GATE_A // D_MODEL)),
            pl.BlockSpec((tm, d), lambda i: (i, COL_GATE_B // D_MODEL)),
            pl.BlockSpec((GDN_DIM, d), const2),
            pl.BlockSpec((NSA_DIM, d), const2),
            pl.BlockSpec((d, d), const2),
            pl.BlockSpec((1, d), const2),
        ],
        out_specs=pl.BlockSpec((tm, d), lambda i: (i, 0)),
        out_shape=jax.ShapeDtypeStruct((rows, d), F32),
        compiler_params=pltpu.CompilerParams(
            dimension_semantics=("parallel",), vmem_limit_bytes=VMEM_LIMIT),
    )(x2d, o_a, o_b, p2d, p2d, w_a, w_b, w_o, final_gain)


def _pack_w_in(w_in):
    bounds = [0]
    for width in (GDN_QKV_DIM, GDN_HEADS, GDN_HEADS, GDN_DIM, NSA_DIM, 6 * NSA_KV_DIM, 3 * NSA_HEADS,
                  NSA_DIM, 2 * D_MODEL):
        bounds.append(bounds[-1] + width)
    qkv, a, bt, z_a, q_b, kv_b, g_b, z_b, merge = (w_in[:, bounds[i]:bounds[i + 1]] for i in range(9))
    small = jnp.concatenate([a, bt, g_b], axis=1)
    small = jnp.pad(small, ((0, 0), (0, N_PACKED - COL_SMALL - small.shape[1])))
    packed = jnp.concatenate([qkv, z_a, merge, q_b, z_b, kv_b, small], axis=1)
    assert packed.shape[1] == N_PACKED
    return packed.astype(BF16)


def _lane_vec(v):
    return jnp.pad(v.astype(F32), (0, LANES - v.shape[0])).reshape(1, LANES)


def _block_diag2(w):
    z = jnp.zeros_like(w)
    return jnp.concatenate([jnp.concatenate([w, z], axis=1), jnp.concatenate([z, w], axis=1)], axis=0)


def _tile_for(n, candidates):
    for c in candidates:
        if n % c == 0:
            return c
    raise ValueError(f"no tile for {n}")


def kernel(x_prompt, x_sample, cache_k_cmp, cache_v_cmp, cache_k_slc, cache_v_slc, state_win_k, state_win_v,
           state_conv, state_gdn, page_table, norm_gain, w_in, conv_w, a_log, dt_bias, gdn_gain, cmp_pos_k,
           cmp_w_k, cmp_pos_v, cmp_w_v, w_branch_a, w_branch_b, w_out, final_gain):
    assert w_in.shape[0] == 1, "one layer"
    bp, tp_len, d = x_prompt.shape
    bs, ts, _ = x_sample.shape
    n_pool, page = cache_k_cmp.shape[1:3]
    n_pages = page_table.shape[1]
    past = n_pages * page
    assert past % CMP_STRIDE == 0 and ts < CMP_STRIDE and ts >= CONV_WIDTH - 1 and ts <= SUBLANES
    t_pad = SUBLANES

    w_packed = _pack_w_in(w_in[0])
    gain_in = norm_gain[0].reshape(1, d)
    fgain = final_gain.reshape(1, d)
    w_a = w_branch_a[0].astype(BF16)
    w_b = w_branch_b[0].astype(BF16)
    w_o = w_out[0].astype(BF16)
    cw = conv_w[0]
    alog = _lane_vec(a_log[0])
    dtb = _lane_vec(dt_bias[0])
    ggain = gdn_gain[0].reshape(1, GDN_HEAD_DIM)
    pos_k = jnp.tile(cmp_pos_k[0], (1, NSA_KV_HEADS))
    pos_v = jnp.tile(cmp_pos_v[0], (1, NSA_KV_HEADS))
    wk2 = _block_diag2(cmp_w_k[0])
    wv2 = _block_diag2(cmp_w_v[0])
    tn = _tile_for(N_PACKED, (2048, 1024, 512, 256))
    kvc = COL_KV

    rows_p = bp * tp_len
    x2p = x_prompt.reshape(rows_p, d)
    p2 = _proj(x2p, gain_in, w_packed, _tile_for(rows_p, (1024, 512, 256, 128)), tn)
    p3 = p2.reshape(bp, tp_len, N_PACKED)
    tb = _tile_for(tp_len, (256, 128, 64))
    o_a, gdn_p = _gdn(p3, cw, jnp.zeros((bp, SUBLANES, GDN_QKV_DIM), F32), alog, dtb, ggain,
                      jnp.zeros((bp, GDN_HEADS, GDN_HEAD_DIM, GDN_HEAD_DIM), F32),
                      tb=tb, chunk=GDN_CHUNK, t_valid=tp_len)
    o_b = _nsa_prompt(p3, pos_k, pos_v, wk2, wv2, tq=256, tk=_tile_for(tp_len, (512, 256, 128)))
    y_p = _merge(x2p, o_a.reshape(rows_p, GDN_DIM), o_b.reshape(rows_p, NSA_DIM), p2, w_a, w_b, w_o, fgain,
                 _tile_for(rows_p, (512, 256, 128)))
    y_prompt = y_p.reshape(bp, tp_len, d)

    def kv_out(p, i, t_keep_from, t_to):
        return p[:, t_keep_from:t_to, kvc + i * LANES:kvc + (i + 1) * LANES].reshape(
            1, p.shape[0], t_to - t_keep_from, NSA_KV_HEADS, NSA_HEAD_DIM)

    keep_p = min(WINDOW, tp_len)
    prompt_state = ([kv_out(p3, i, 0, tp_len) for i in range(4)]
                    + [kv_out(p3, i, tp_len - keep_p, tp_len) for i in (4, 5)]
                    + [p3[:, tp_len - (CONV_WIDTH - 1):, :GDN_QKV_DIM][None], gdn_p[None]])

    xs_pad = jnp.pad(x_sample, ((0, 0), (0, t_pad - ts), (0, 0)))
    rows_s = bs * t_pad
    x2s = xs_pad.reshape(rows_s, d)
    ps2 = _proj(x2s, gain_in, w_packed, _tile_for(rows_s, (512, 256, 128, 8)), tn)
    ps3 = ps2.reshape(bs, t_pad, N_PACKED)
    prefix_s = jnp.pad(state_conv[0], ((0, 0), (SUBLANES - (CONV_WIDTH - 1), 0), (0, 0)))
    o_a_s, gdn_s = _gdn(ps3, cw, prefix_s, alog, dtb, ggain, state_gdn[0], tb=t_pad, chunk=t_pad, t_valid=ts)

    pools = [c[0].transpose(0, 2, 3, 1) for c in (cache_k_cmp, cache_v_cmp, cache_k_slc, cache_v_slc)]
    kc_s, vc_s = _cmp_paged(pools[0], pools[1], page_table, pos_k, pos_v, wk2, wv2,
                            k_pages=_tile_for(n_pages, (16, 8, 4, 2, 1)))
    n_blocks = -(-(past + ts) // SEL_BLOCK)
    nbl = -(-n_blocks // LANES) * LANES
    w_buf = state_win_k.shape[2]
    o_c_s, o_w_s, sel_s, win_k_s, win_v_s = _nsa_sample_a(
        ps3, kc_s, vc_s, state_win_k[0].reshape(bs, w_buf, NSA_KV_DIM), state_win_v[0].reshape(bs, w_buf, NSA_KV_DIM),
        past=past, t_valid=ts, n_blocks=n_blocks, nbl=nbl)
    o_b_s = _nsa_sample_b(ps3, sel_s, o_c_s, o_w_s, pools[2], pools[3], page_table,
                          k_pages=_tile_for(n_pages, (64, 32, 16, 8, 4, 2, 1)),
                          past=past, t_valid=ts, n_blocks=n_blocks, nbl=nbl)
    y_s = _merge(x2s, o_a_s.reshape(rows_s, GDN_DIM), o_b_s.reshape(rows_s, NSA_DIM), ps2, w_a, w_b, w_o, fgain,
                 _tile_for(rows_s, (512, 256, 128, 8)))
    y_sample = y_s.reshape(bs, t_pad, d)[:, :ts]

    keep_s = min(WINDOW, w_buf + ts)
    assert keep_s == w_buf
    sample_state = ([kv_out(ps3, i, 0, ts) for i in range(4)]
                    + [win_k_s.reshape(1, bs, keep_s, NSA_KV_HEADS, NSA_HEAD_DIM),
                       win_v_s.reshape(1, bs, keep_s, NSA_KV_HEADS, NSA_HEAD_DIM),
                       ps3[:, ts - (CONV_WIDTH - 1):ts, :GDN_QKV_DIM][None], gdn_s[None]])
    return (y_prompt, y_sample, *prompt_state, *sample_state)
```

```python
import functools
import math

import jax
import jax.numpy as jnp
from jax import lax
from jax.experimental import pallas as pl
from jax.experimental.pallas import tpu as pltpu

F32 = jnp.float32
BF16 = jnp.bfloat16
HI = lax.Precision.HIGHEST

LANES = 128
SUBLANES = 8
VMEM_LIMIT = 56 * 1024 * 1024

D_MODEL = 1024
GDN_HEADS = 8
GDN_HEAD_DIM = 128
GDN_DIM = GDN_HEADS * GDN_HEAD_DIM
GDN_QKV_DIM = 3 * GDN_DIM
CONV_WIDTH = 4
GDN_CHUNK = 64
NSA_HEADS = 8
NSA_KV_HEADS = 2
NSA_GROUP = NSA_HEADS // NSA_KV_HEADS
NSA_HEAD_DIM = 64
NSA_DIM = NSA_HEADS * NSA_HEAD_DIM
NSA_KV_DIM = NSA_KV_HEADS * NSA_HEAD_DIM
CMP_STRIDE = 16
CMP_LEN = 2 * CMP_STRIDE
SEL_BLOCK = 64
SEL_SHIFT = 6
SEL_RATIO = SEL_BLOCK // CMP_STRIDE
N_SELECT = 16
WINDOW = 512
FORCE_BONUS = 100.0
RMS_EPS = 1e-6
NEG_INF = -1e30
NSA_SCALE = NSA_HEAD_DIM ** -0.5
LOG2E = math.log2(math.e)

COL_QKV = 0
COL_ZA = 3072
COL_GATE_A = 4096
COL_GATE_B = 5120
COL_QB = 6144
COL_ZB = 6656
COL_KV = 7168
COL_SMALL = 7936
N_PACKED = 8192
SMALL_A = 0
SMALL_BT = GDN_HEADS
SMALL_GB = 2 * GDN_HEADS


def _dot(a, b, precision=None):
    return jnp.dot(a, b, precision=precision, preferred_element_type=F32)


def _dot_nt(a, b, precision=None):
    return lax.dot_general(a, b, (((1,), (1,)), ((), ())), precision=precision,
                           preferred_element_type=F32)


def _dot_tn(a, b, precision=None):
    return lax.dot_general(a, b, (((0,), (0,)), ((), ())), precision=precision,
                           preferred_element_type=F32)


def _mm(a, b):
    return jnp.dot(a.astype(BF16), b.astype(BF16), preferred_element_type=F32)


def _silu(x):
    return x * jax.nn.sigmoid(x)


def _iota(shape, dim):
    return lax.broadcasted_iota(jnp.int32, shape, dim)


def _proj_kernel(x_ref, g_ref, w_ref, o_ref, h_ref):
    @pl.when(pl.program_id(1) == 0)
    def _():
        x = x_ref[...]
        y = x * lax.rsqrt(jnp.mean(x * x, axis=-1, keepdims=True) + RMS_EPS)
        h_ref[...] = (y * g_ref[...]).astype(BF16)

    o_ref[...] = jnp.dot(h_ref[...], w_ref[...], preferred_element_type=F32)


def _proj(x2d, gain, w_packed, tm, tn):
    rows, d = x2d.shape
    n = w_packed.shape[1]
    return pl.pallas_call(
        _proj_kernel,
        grid=(rows // tm, n // tn),
        in_specs=[
            pl.BlockSpec((tm, d), lambda i, j: (i, 0)),
            pl.BlockSpec((1, d), lambda i, j: (0, 0)),
            pl.BlockSpec((d, tn), lambda i, j: (0, j)),
        ],
        out_specs=pl.BlockSpec((tm, tn), lambda i, j: (i, j)),
        out_shape=jax.ShapeDtypeStruct((rows, n), F32),
        scratch_shapes=[pltpu.VMEM((tm, d), BF16)],
        compiler_params=pltpu.CompilerParams(
            dimension_semantics=("parallel", "arbitrary"), vmem_limit_bytes=VMEM_LIMIT),
    )(x2d, gain, w_packed)


def _gdn_kernel(q_ref, k_ref, v_ref, z_ref, sm_ref, cw_ref, pre_ref, alog_ref, dtb_ref, gain_ref, s0_ref,
                o_ref, sfin_ref,
                s_sc, tail_sc, q_sc, k_sc, v_sc, gc_sc, b_sc, gct_sc, *, tb, chunk, t_valid):
    t = pl.program_id(1)
    c = chunk
    hd = GDN_HEAD_DIM
    nh = GDN_HEADS

    @pl.when(t == 0)
    def _():
        s_sc[...] = s0_ref[...]
        tail_sc[...] = pre_ref[...]

    for idx, (x_ref, dst) in enumerate(((q_ref, q_sc), (k_ref, k_sc), (v_ref, v_sc))):
        cols = slice(idx * GDN_DIM, (idx + 1) * GDN_DIM)
        x = x_ref[...]
        cw = cw_ref[:, cols]
        last = CONV_WIDTH - 1
        head = jnp.concatenate([tail_sc[:, cols], x[0:SUBLANES]], axis=0)
        tail_sc[:, cols] = x[tb - SUBLANES:tb]
        y = cw[last:last + 1] * head[SUBLANES:2 * SUBLANES]
        for j in range(1, CONV_WIDTH):
            y = y + cw[last - j:last - j + 1] * head[SUBLANES - j:2 * SUBLANES - j]
        if tb > SUBLANES:
            rest = cw[last:last + 1] * x
            for j in range(1, CONV_WIDTH):
                rest = rest + cw[last - j:last - j + 1] * pltpu.roll(x, j, 0)
            y = jnp.concatenate([y, rest[SUBLANES:]], axis=0)
        y = _silu(y)
        if idx == 2:
            dst[...] = y
        else:
            scale = GDN_HEAD_DIM ** -0.5 if idx == 0 else 1.0
            for h in range(nh):
                yh = y[:, h * hd:(h + 1) * hd]
                dst[:, h * hd:(h + 1) * hd] = yh * (lax.rsqrt(jnp.sum(yh * yh, axis=-1, keepdims=True) + 1e-6) * scale)

    sm = sm_ref[...]
    xx = sm + dtb_ref[...]
    softplus = jnp.maximum(xx, 0.0) + jnp.log(1.0 + jnp.exp(-jnp.abs(xx)))
    row = _iota((tb, 1), 0)
    live = t * tb + row < t_valid
    g_small = jnp.where(live, -jnp.exp(alog_ref[...]) * softplus, 0.0)
    beta_small = jnp.where(live, jax.nn.sigmoid(sm), 0.0)
    row_in_chunk = row & (c - 1)
    gc_small = g_small
    shift = 1
    while shift < c:
        gc_small = gc_small + jnp.where(row_in_chunk >= shift, pltpu.roll(gc_small, shift, 0), 0.0)
        shift *= 2
    li = _iota((LANES, GDN_DIM), 0)
    head_of_lane = _iota((LANES, GDN_DIM), 1) >> int(math.log2(hd))

    def spread(x, first_lane):
        onehot = (li == head_of_lane + first_lane).astype(BF16)
        hi = x.astype(BF16)
        r1 = x - hi.astype(F32)
        mid = r1.astype(BF16)
        lo = (r1 - mid.astype(F32)).astype(BF16)
        return _dot(hi, onehot) + _dot(mid, onehot) + _dot(lo, onehot)

    gc_sc[...] = spread(gc_small, SMALL_A)
    b_sc[...] = spread(beta_small, SMALL_BT)
    for r0 in range(0, tb, LANES):
        n = min(LANES, tb - r0)
        part = gc_small[r0:r0 + n]
        if n < LANES:
            part = jnp.concatenate([part, jnp.zeros((LANES - n, LANES), F32)], axis=0)
        part_t = part.T
        for off in range(0, n, c):
            gct_sc[(r0 + off) // c] = part_t[:, off:off + c]

    ri = _iota((c, c), 0)
    ci = _iota((c, c), 1)
    causal = ri >= ci
    strict = ri > ci
    eye_f = (ri == ci).astype(F32)
    n_doubling = int(math.log2(c)) - 1
    gain = gain_ref[...]

    heads = range(nh)
    hs = [slice(h * hd, (h + 1) * hd) for h in heads]
    n_chunks = tb // c
    rw = [slice(ck * c, (ck + 1) * c) for ck in range(n_chunks)]
    pairs = [(ck, h) for ck in range(n_chunks) for h in heads]

    kq = {}
    for ck, h in pairs:
        kk = k_sc[rw[ck], hs[h]].astype(BF16)
        kq[ck, h] = _dot_nt(jnp.concatenate([kk, q_sc[rw[ck], hs[h]].astype(BF16)], axis=0), kk)
    decay = {(ck, h): jnp.where(causal, jnp.exp(gc_sc[rw[ck], hs[h]][:, :c] - gct_sc[ck, h:h + 1, :]), 0.0)
             for ck, h in pairs}
    attn = {p: kq[p][c:] * decay[p] for p in pairs}
    nk = {(ck, h): jnp.where(strict, -(b_sc[rw[ck], hs[h]][:, :c] * kq[ck, h][:c] * decay[ck, h]), 0.0)
          for ck, h in pairs}
    tinv = {p: eye_f + nk[p] for p in pairs}
    for _ in range(n_doubling):
        nk = {p: _mm(nk[p], nk[p]) for p in pairs}
        tinv = {p: tinv[p] + _mm(tinv[p], nk[p]) for p in pairs}
    uw = {}
    for ck, h in pairs:
        bb = b_sc[rw[ck], hs[h]]
        rhs = jnp.concatenate([v_sc[rw[ck], hs[h]] * bb,
                               k_sc[rw[ck], hs[h]] * (bb * jnp.exp(gc_sc[rw[ck], hs[h]]))], axis=1)
        uw[ck, h] = _mm(tinv[ck, h], rhs)

    state = [s_sc[h] for h in heads]
    for ck in range(n_chunks):
        rows = rw[ck]
        ws_qs = []
        for h in heads:
            q_dec = q_sc[rows, hs[h]] * jnp.exp(gc_sc[rows, hs[h]])
            ws_qs.append(_mm(jnp.concatenate([uw[ck, h][:, hd:], q_dec], axis=0), state[h]))
        v_new = [uw[ck, h][:, :hd] - ws_qs[h][:c] for h in heads]
        o = [ws_qs[h][c:] + _mm(attn[ck, h], v_new[h]) for h in heads]
        new_state = []
        for h in heads:
            gc = gc_sc[rows, hs[h]]
            g_last = gc[c - 1:c, :]
            k_dec = k_sc[rows, hs[h]] * jnp.exp(g_last - gc)
            new_state.append(state[h] * jnp.exp(g_last) + _dot_tn(k_dec, v_new[h]))
        state = new_state
        for h in heads:
            on = o[h] * lax.rsqrt(jnp.mean(o[h] * o[h], axis=-1, keepdims=True) + RMS_EPS) * gain
            o_ref[rows, hs[h]] = on * _silu(z_ref[rows, hs[h]])
    for h in heads:
        s_sc[h] = state[h]

    @pl.when(t == pl.num_programs(1) - 1)
    def _():
        sfin_ref[...] = s_sc[...]


def _gdn(p3, conv_w, prefix, alog, dtb, gain, s0, *, tb, chunk, t_valid):
    b, t, _ = p3.shape
    hd = GDN_HEAD_DIM
    gd = GDN_DIM
    const2 = lambda bi, ti: (0, 0)
    kern = functools.partial(_gdn_kernel, tb=tb, chunk=chunk, t_valid=t_valid)
    return pl.pallas_call(
        kern,
        grid=(b, t // tb),
        in_specs=[
            pl.BlockSpec((None, tb, gd), lambda bi, ti: (bi, ti, COL_QKV // gd)),
            pl.BlockSpec((None, tb, gd), lambda bi, ti: (bi, ti, COL_QKV // gd + 1)),
            pl.BlockSpec((None, tb, gd), lambda bi, ti: (bi, ti, COL_QKV // gd + 2)),
            pl.BlockSpec((None, tb, gd), lambda bi, ti: (bi, ti, COL_ZA // gd)),
            pl.BlockSpec((None, tb, LANES), lambda bi, ti: (bi, ti, COL_SMALL // LANES)),
            pl.BlockSpec((CONV_WIDTH, GDN_QKV_DIM), const2),
            pl.BlockSpec((None, SUBLANES, GDN_QKV_DIM), lambda bi, ti: (bi, 0, 0)),
            pl.BlockSpec((1, LANES), const2),
            pl.BlockSpec((1, LANES), const2),
            pl.BlockSpec((1, hd), const2),
            pl.BlockSpec((None, GDN_HEADS, hd, hd), lambda bi, ti: (bi, 0, 0, 0)),
        ],
        out_specs=[
            pl.BlockSpec((None, tb, gd), lambda bi, ti: (bi, ti, 0)),
            pl.BlockSpec((None, GDN_HEADS, hd, hd), lambda bi, ti: (bi, 0, 0, 0)),
        ],
        out_shape=[
            jax.ShapeDtypeStruct((b, t, gd), F32),
            jax.ShapeDtypeStruct((b, GDN_HEADS, hd, hd), F32),
        ],
        scratch_shapes=[
            pltpu.VMEM((GDN_HEADS, hd, hd), F32),
            pltpu.VMEM((SUBLANES, GDN_QKV_DIM), F32),
            pltpu.VMEM((tb, gd), F32),
            pltpu.VMEM((tb, gd), F32),
            pltpu.VMEM((tb, gd), F32),
            pltpu.VMEM((tb, gd), F32),
            pltpu.VMEM((tb, gd), F32),
            pltpu.VMEM((tb // chunk, LANES, chunk), F32),
        ],
        compiler_params=pltpu.CompilerParams(
            dimension_semantics=("parallel", "arbitrary"), vmem_limit_bytes=VMEM_LIMIT),
    )(p3, p3, p3, p3, p3, conv_w, prefix, alog, dtb, gain, s0)


def _inv_denominator(total):
    return 1.0 / jnp.maximum(total, 1e-30)


def _masked_softmax2(s, mask):
    s = jnp.where(mask, s, NEG_INF)
    e = jnp.where(mask, jnp.exp2(s - jnp.max(s, axis=-1, keepdims=True)), 0.0)
    return e * _inv_denominator(jnp.sum(e, axis=-1, keepdims=True))


def _stack_heads(q, g):
    hd = NSA_HEAD_DIM
    parts = [q[:, (NSA_GROUP * g + h) * hd:(NSA_GROUP * g + h + 1) * hd] for h in range(NSA_GROUP)]
    return jnp.concatenate(parts, axis=0) * (NSA_SCALE * LOG2E)


def _tile_rows(x):
    return jnp.concatenate([x] * NSA_GROUP, axis=0)


def _group_importance(p_c, tq):
    imp = p_c[0:tq]
    for h in range(1, NSA_GROUP):
        imp = imp + p_c[h * tq:(h + 1) * tq]
    return imp


def _select_blocks(p_c, pos, tq, n_blocks, nbl):
    m = p_c.shape[1]
    imp = _group_importance(p_c, tq)
    pool = ((_iota((m, nbl), 0) >> int(math.log2(SEL_RATIO))) == _iota((m, nbl), 1)).astype(F32)
    imp_sel = _dot(imp, pool, HI)
    lane = _iota((tq, nbl), 1)
    valid = lane * SEL_BLOCK <= pos
    cur = pos >> SEL_SHIFT
    forced = (lane == 0) | (lane == cur) | (lane == cur - 1)
    score = jnp.where(valid, imp_sel + FORCE_BONUS * forced.astype(F32), -1.0)
    score = jnp.where(lane < n_blocks, score, -2.0)
    rank = jnp.zeros((tq, nbl), F32)
    for j in range(n_blocks):
        sj = score[:, j:j + 1]
        rank = rank + ((sj > score) | ((sj == score) & (lane > j))).astype(F32)
    return (rank < float(min(N_SELECT, n_blocks))).astype(F32)


def _select_blocks_t(p_c, q0, tq, n_blocks, nbl):
    m = p_c.shape[1]
    n_sel = min(N_SELECT, n_blocks)
    nbs = -(-n_blocks // SUBLANES) * SUBLANES
    imp = _group_importance(p_c, tq)
    pool_t = ((_iota((nbs, m), 1) >> int(math.log2(SEL_RATIO))) == _iota((nbs, m), 0)).astype(F32)
    imp_t = _dot_nt(pool_t, imp, HI)
    pos_r = q0 + _iota((1, tq), 1)
    blk = _iota((nbs, tq), 0)
    valid = blk * SEL_BLOCK <= pos_r

    def ranked():
        cur = pos_r >> SEL_SHIFT
        forced = (blk == 0) | (blk == cur) | (blk == cur - 1)
        score = jnp.where(valid, imp_t + FORCE_BONUS * forced.astype(F32), -1.0)
        score = jnp.where(blk < n_blocks, score, -2.0)
        rank = jnp.zeros((nbs, tq), F32)
        for j in range(n_blocks):
            sj = score[j:j + 1, :]
            rank = rank + ((sj > score) | ((sj == score) & (blk > j))).astype(F32)
        return (rank < float(n_sel)).astype(F32)

    sel_t = lax.cond(q0 + tq <= n_sel * SEL_BLOCK, lambda: valid.astype(F32), ranked)
    sel_t = jnp.concatenate([sel_t, jnp.zeros((nbl - nbs, tq), F32)], axis=0)
    return sel_t.T


def _compress_rows(x, pw):
    n = x.shape[0] // CMP_STRIDE
    x3 = x.reshape(n, CMP_STRIDE, LANES)
    lo = jnp.sum(x3 * pw[:CMP_STRIDE][None], axis=1)
    hi = jnp.sum(x3 * pw[CMP_STRIDE:][None], axis=1)
    return lo, hi


def _combine_heads(o_ref, g, gates, zb, o_c, o_s, o_w, tq):
    hd = NSA_HEAD_DIM
    for h in range(NSA_GROUP):
        hh = NSA_GROUP * g + h
        c0 = SMALL_GB + 3 * hh
        rows = slice(h * tq, (h + 1) * tq)
        o_h = (gates[:, c0:c0 + 1] * o_c[rows] + gates[:, c0 + 1:c0 + 2] * o_s[rows]
               + gates[:, c0 + 2:c0 + 3] * o_w[rows])
        o_ref[:, hh * hd:(hh + 1) * hd] = o_h * _silu(zb[:, hh * hd:(hh + 1) * hd])


def _nsa_prompt_kernel(q_ref, zb_ref, sm_ref, kcmp_ref, vcmp_ref, kslc_ref, vslc_ref, kwin_ref, vwin_ref,
                       pk_ref, pv_ref, wk_ref, wv_ref, o_ref, kc_sc, vc_sc, kcx_sc, kx_sc, kw_sc, vx_sc, *,
                       t_len, tq, tk, span, nbl, nbo):
    i = pl.program_id(1)
    m = t_len // CMP_STRIDE
    n_blocks = t_len // SEL_BLOCK
    hd = NSA_HEAD_DIM

    @pl.when(i == 0)
    def _():
        for src, pw_ref, w_ref, dst in ((kcmp_ref, pk_ref, wk_ref, kc_sc), (vcmp_ref, pv_ref, wv_ref, vc_sc)):
            lo, hi = _compress_rows(src[...], pw_ref[...])
            hi_next = pltpu.roll(hi, m - 1, 0)
            pre = jnp.where(_iota((m, LANES), 0) < m - 1, lo + hi_next, 0.0)
            dst[...] = _dot(pre, w_ref[...], HI)
        kc = kc_sc[...]
        kc_hi = kc.astype(BF16)
        kc_lo = (kc - kc_hi.astype(F32)).astype(BF16)
        for g in range(NSA_KV_HEADS):
            gs = slice(g * hd, (g + 1) * hd)
            kcx_sc[g] = jnp.concatenate([kc_hi[:, gs], kc_hi[:, gs], kc_lo[:, gs], jnp.zeros((m, hd), BF16)], axis=1)
        k_slc = kslc_ref[...].astype(BF16)
        blk_onehot = ((_iota((t_len, nbo), 0) >> SEL_SHIFT) == _iota((t_len, nbo), 1)).astype(BF16)
        for g in range(NSA_KV_HEADS):
            kx_sc[g] = jnp.concatenate([k_slc[:, g * hd:(g + 1) * hd], blk_onehot], axis=1)
        kw_sc[...] = kwin_ref[...].astype(BF16)
        ones = jnp.ones((t_len, hd), BF16)
        for idx, src in enumerate((vslc_ref, vwin_ref)):
            v = src[...].astype(BF16)
            for g in range(NSA_KV_HEADS):
                vx_sc[idx, g] = jnp.concatenate([v[:, g * hd:(g + 1) * hd], ones], axis=1)

    q0 = i * tq
    pos = q0 + _iota((tq, 1), 0)
    pos4 = _tile_rows(pos)
    q = q_ref[...]
    zb = zb_ref[...]
    gates = jax.nn.sigmoid(sm_ref[...])
    c_end = _iota((1, m), 1) * CMP_STRIDE + (CMP_LEN - 1)
    n_full = q0 // tk
    w_start = pl.multiple_of(jnp.clip(q0 - WINDOW, 0, t_len - span), tq)
    w_dist = pos - (w_start + _iota((1, span), 1))
    w_bias = _tile_rows(jnp.where((w_dist >= 0) & (w_dist <= WINDOW), 0.0, NEG_INF))

    for g in range(NSA_KV_HEADS):
        gs = slice(g * hd, (g + 1) * hd)
        qs = _stack_heads(q, g)
        qs_bf = qs.astype(BF16)
        qs_lo = (qs - qs_bf.astype(F32)).astype(BF16)
        q_split = jnp.concatenate([qs_bf, qs_lo, qs_bf, jnp.zeros_like(qs_bf)], axis=1)
        p_c = _masked_softmax2(_dot_nt(q_split, kcx_sc[g]), c_end <= pos4)
        o_c = _dot(p_c, vc_sc[:, gs])
        sel = _select_blocks_t(p_c, q0, tq, n_blocks, nbl)
        sel_bias = ((sel[:, :nbo] - 1.0) * -NEG_INF).astype(BF16)
        q_aug = jnp.concatenate([qs_bf, _tile_rows(sel_bias)], axis=1)

        def sel_tile(kt, carry, causal_tile, q_aug=q_aug, g=g):
            m_run, acc = carry
            k0 = pl.multiple_of(kt * tk, tk)
            s = _dot_nt(q_aug, kx_sc[g, pl.ds(k0, tk), :])
            if causal_tile:
                s = s + _tile_rows(jnp.where(k0 + _iota((1, tk), 1) <= pos, 0.0, NEG_INF))
            m_new = jnp.maximum(m_run, jnp.max(s, axis=-1, keepdims=True))
            e = jnp.exp2(s - m_new).astype(BF16)
            acc_new = jnp.exp2(m_run - m_new) * acc + _dot(e, vx_sc[0, g, pl.ds(k0, tk), :])
            return m_new, acc_new

        init = (jnp.full((NSA_GROUP * tq, 1), NEG_INF, F32), jnp.zeros((NSA_GROUP * tq, 2 * hd), F32))
        carry = lax.fori_loop(0, n_full, functools.partial(sel_tile, causal_tile=False), init)
        _, acc_s = sel_tile(n_full, carry, True)
        o_s = acc_s[:, :hd] * _inv_denominator(acc_s[:, hd:hd + 1])

        s_w = _dot_nt(qs_bf, kw_sc[pl.ds(w_start, span), gs]) + w_bias
        e_w = jnp.exp2(s_w - jnp.max(s_w, axis=-1, keepdims=True)).astype(BF16)
        acc_w = _dot(e_w, vx_sc[1, g, pl.ds(w_start, span), :])
        o_w = acc_w[:, :hd] * _inv_denominator(acc_w[:, hd:hd + 1])
        _combine_heads(o_ref, g, gates, zb, o_c, o_s, o_w, tq)


def _nsa_prompt(p3, pos_k, pos_v, w_k, w_v, *, tq, tk):
    b, t, _ = p3.shape
    span = WINDOW + tq
    assert t % tq == 0 and t % tk == 0 and t >= span and t % SEL_BLOCK == 0 and tq % LANES == 0
    assert WINDOW % tq == 0 and tk % tq == 0
    nbo = -(-(t // SEL_BLOCK) // NSA_HEAD_DIM) * NSA_HEAD_DIM
    assert (NSA_HEAD_DIM + nbo) % LANES == 0
    m = t // CMP_STRIDE
    nbl = -(-(t // SEL_BLOCK) // LANES) * LANES
    kvc = COL_KV // LANES

    def full(c):
        return pl.BlockSpec((None, t, LANES), lambda bi, i: (bi, 0, c))

    const2 = lambda bi, i: (0, 0)
    kern = functools.partial(_nsa_prompt_kernel, t_len=t, tq=tq, tk=tk, span=span, nbl=nbl, nbo=nbo)
    return pl.pallas_call(
        kern,
        grid=(b, t // tq),
        in_specs=[
            pl.BlockSpec((None, tq, NSA_DIM), lambda bi, i: (bi, i, COL_QB // NSA_DIM)),
            pl.BlockSpec((None, tq, NSA_DIM), lambda bi, i: (bi, i, COL_ZB // NSA_DIM)),
            pl.BlockSpec((None, tq, LANES), lambda bi, i: (bi, i, COL_SMALL // LANES)),
            full(kvc), full(kvc + 1), full(kvc + 2), full(kvc + 3), full(kvc + 4), full(kvc + 5),
            pl.BlockSpec((CMP_LEN, LANES), const2),
            pl.BlockSpec((CMP_LEN, LANES), const2),
            pl.BlockSpec((LANES, LANES), const2),
            pl.BlockSpec((LANES, LANES), const2),
        ],
        out_specs=pl.BlockSpec((None, tq, NSA_DIM), lambda bi, i: (bi, i, 0)),
        out_shape=jax.ShapeDtypeStruct((b, t, NSA_DIM), F32),
        scratch_shapes=[pltpu.VMEM((m, LANES), F32), pltpu.VMEM((m, LANES), F32),
                        pltpu.VMEM((NSA_KV_HEADS, m, 4 * NSA_HEAD_DIM), BF16),
                        pltpu.VMEM((NSA_KV_HEADS, t, NSA_HEAD_DIM + nbo), BF16),
                        pltpu.VMEM((t, LANES), BF16),
                        pltpu.VMEM((2, NSA_KV_HEADS, t, 2 * NSA_HEAD_DIM), BF16)],
        compiler_params=pltpu.CompilerParams(
            dimension_semantics=("parallel", "arbitrary"), vmem_limit_bytes=VMEM_LIMIT),
    )(p3, p3, p3, p3, p3, p3, p3, p3, p3, pos_k, pos_v, w_k, w_v)


def _cmp_paged_kernel(pt_ref, *refs, k_pages):
    del pt_ref
    kp = refs[:k_pages]
    kn = refs[k_pages]
    vp = refs[k_pages + 1:2 * k_pages + 1]
    vn = refs[2 * k_pages + 1]
    pk_ref, pv_ref, wk_ref, wv_ref, kc_ref, vc_ref = refs[2 * k_pages + 2:]
    n = kc_ref.shape[0]
    last_row = _iota((n, LANES), 0) == n - 1

    def row_major(r):
        return r[...].reshape(LANES, LANES).T

    for pages, nxt, pw_ref, w_ref, out in ((kp, kn, pk_ref, wk_ref, kc_ref), (vp, vn, pv_ref, wv_ref, vc_ref)):
        pw = pw_ref[...]
        parts = [_compress_rows(row_major(r), pw) for r in pages]
        lo = jnp.concatenate([p[0] for p in parts], axis=0)
        hi = jnp.concatenate([p[1] for p in parts], axis=0)
        hi_head = jnp.sum(row_major(nxt)[:CMP_STRIDE] * pw[CMP_STRIDE:], axis=0, keepdims=True)
        hi_next = jnp.where(last_row, hi_head, pltpu.roll(hi, n - 1, 0))
        out[...] = _dot(lo + hi_next, w_ref[...], HI)


def _cmp_paged(pool_k, pool_v, page_table, pos_k, pos_v, w_k, w_v, *, k_pages):
    b, n_pages = page_table.shape
    page = pool_k.shape[3]
    per_step = k_pages * page // CMP_STRIDE
    assert page == LANES and n_pages % k_pages == 0
    blk = (None, NSA_KV_HEADS, NSA_HEAD_DIM, page)

    def page_spec(i):
        return pl.BlockSpec(blk, lambda bi, j, pt: (pt[bi, j * k_pages + i], 0, 0, 0))

    next_spec = pl.BlockSpec(
        blk, lambda bi, j, pt: (pt[bi, jnp.minimum((j + 1) * k_pages, n_pages - 1)], 0, 0, 0))
    const2 = lambda bi, j, pt: (0, 0)
    in_specs = ([page_spec(i) for i in range(k_pages)] + [next_spec]
                + [page_spec(i) for i in range(k_pages)] + [next_spec]
                + [pl.BlockSpec((CMP_LEN, LANES), const2)] * 2 + [pl.BlockSpec((LANES, LANES), const2)] * 2)
    out_spec = pl.BlockSpec((None, per_step, LANES), lambda bi, j, pt: (bi, j, 0))
    m = n_pages * page // CMP_STRIDE
    return pl.pallas_call(
        functools.partial(_cmp_paged_kernel, k_pages=k_pages),
        grid_spec=pltpu.PrefetchScalarGridSpec(
            num_scalar_prefetch=1, grid=(b, n_pages // k_pages), in_specs=in_specs,
            out_specs=[out_spec, out_spec]),
        out_shape=[jax.ShapeDtypeStruct((b, m, LANES), F32)] * 2,
        compiler_params=pltpu.CompilerParams(
            dimension_semantics=("parallel", "arbitrary"), vmem_limit_bytes=VMEM_LIMIT),
    )(page_table, *([pool_k] * (k_pages + 1)), *([pool_v] * (k_pages + 1)), pos_k, pos_v, w_k, w_v)


def _new_key_scores(qs, new_k, r, gs):
    return jnp.sum(qs * new_k[r:r + 1, gs], axis=-1, keepdims=True)


def _nsa_sample_a_kernel(q_ref, kc_ref, vc_ref, bk_ref, bv_ref, nk_ref, nv_ref,
                         oc_ref, ow_ref, sel_ref, wk_out, wv_out, *, past, t_valid, tp, n_blocks, nbl):
    hd = NSA_HEAD_DIM
    m = kc_ref.shape[0]
    w_buf = bk_ref.shape[0]
    pos = past + _iota((tp, 1), 0)
    pos4 = _tile_rows(pos)
    q = q_ref[...]
    new_k = nk_ref[...]
    new_v = nv_ref[...]
    c_end = _iota((1, m), 1) * CMP_STRIDE + (CMP_LEN - 1)
    buf_pos = (past - w_buf) + _iota((1, w_buf), 1)
    buf_dist = pos4 - buf_pos
    buf_mask = (buf_dist >= 0) & (buf_dist <= WINDOW)
    for g in range(NSA_KV_HEADS):
        gs = slice(g * hd, (g + 1) * hd)
        qs = _stack_heads(q, g)
        p_c = _masked_softmax2(_dot_nt(qs, kc_ref[:, gs], HI), c_end <= pos4)
        oc_ref[g] = _dot(p_c, vc_ref[:, gs])
        sel_ref[g] = _select_blocks(p_c, pos, tp, n_blocks, nbl)
        s_buf = jnp.where(buf_mask, _dot_nt(qs, bk_ref[:, gs]), NEG_INF)
        s_new, new_mask = [], []
        for r in range(t_valid):
            dist = pos4 - (past + r)
            mk = (dist >= 0) & (dist <= WINDOW)
            new_mask.append(mk)
            s_new.append(jnp.where(mk, _new_key_scores(qs, new_k, r, gs), NEG_INF))
        m_all = jnp.max(s_buf, axis=-1, keepdims=True)
        for sn in s_new:
            m_all = jnp.maximum(m_all, sn)
        e_buf = jnp.where(buf_mask, jnp.exp2(s_buf - m_all), 0.0)
        l_all = jnp.sum(e_buf, axis=-1, keepdims=True)
        acc = _dot(e_buf, bv_ref[:, gs])
        for r in range(t_valid):
            e_r = jnp.where(new_mask[r], jnp.exp2(s_new[r] - m_all), 0.0)
            l_all = l_all + e_r
            acc = acc + e_r * new_v[r:r + 1, gs]
        ow_ref[g] = acc * _inv_denominator(l_all)
    wk_out[0:w_buf - t_valid, :] = bk_ref[t_valid:w_buf, :]
    wk_out[w_buf - t_valid:w_buf, :] = new_k[0:t_valid]
    wv_out[0:w_buf - t_valid, :] = bv_ref[t_valid:w_buf, :]
    wv_out[w_buf - t_valid:w_buf, :] = new_v[0:t_valid]


def _nsa_sample_a(p3, kc, vc, buf_k, buf_v, *, past, t_valid, n_blocks, nbl):
    b, tp, _ = p3.shape
    m = kc.shape[1]
    w_buf = buf_k.shape[1]
    assert w_buf == WINDOW and t_valid <= w_buf
    kvc = COL_KV // LANES
    rows = NSA_GROUP * tp
    kern = functools.partial(_nsa_sample_a_kernel, past=past, t_valid=t_valid, tp=tp,
                             n_blocks=n_blocks, nbl=nbl)
    per_b3 = lambda bi: (bi, 0, 0)
    per_b4 = lambda bi: (bi, 0, 0, 0)
    return pl.pallas_call(
        kern,
        grid=(b,),
        in_specs=[
            pl.BlockSpec((None, tp, NSA_DIM), lambda bi: (bi, 0, COL_QB // NSA_DIM)),
            pl.BlockSpec((None, m, LANES), per_b3),
            pl.BlockSpec((None, m, LANES), per_b3),
            pl.BlockSpec((None, w_buf, LANES), per_b3),
            pl.BlockSpec((None, w_buf, LANES), per_b3),
            pl.BlockSpec((None, tp, LANES), lambda bi: (bi, 0, kvc + 4)),
            pl.BlockSpec((None, tp, LANES), lambda bi: (bi, 0, kvc + 5)),
        ],
        out_specs=[
            pl.BlockSpec((None, NSA_KV_HEADS, rows, NSA_HEAD_DIM), per_b4),
            pl.BlockSpec((None, NSA_KV_HEADS, rows, NSA_HEAD_DIM), per_b4),
            pl.BlockSpec((None, NSA_KV_HEADS, tp, nbl), per_b4),
            pl.BlockSpec((None, w_buf, LANES), per_b3),
            pl.BlockSpec((None, w_buf, LANES), per_b3),
        ],
        out_shape=[
            jax.ShapeDtypeStruct((b, NSA_KV_HEADS, rows, NSA_HEAD_DIM), F32),
            jax.ShapeDtypeStruct((b, NSA_KV_HEADS, rows, NSA_HEAD_DIM), F32),
            jax.ShapeDtypeStruct((b, NSA_KV_HEADS, tp, nbl), F32),
            jax.ShapeDtypeStruct((b, w_buf, LANES), F32),
            jax.ShapeDtypeStruct((b, w_buf, LANES), F32),
        ],
        compiler_params=pltpu.CompilerParams(
            dimension_semantics=("parallel",), vmem_limit_bytes=VMEM_LIMIT),
    )(p3, kc, vc, buf_k, buf_v, p3, p3)


def _nsa_sample_b_kernel(pt_ref, q_ref, zb_ref, sm_ref, nk_ref, nv_ref, sel_ref, selj_ref, exp_ref,
                         oc_ref, ow_ref, *refs, k_pages, past, t_valid, tp, n_blocks, nbl):
    del pt_ref
    kp = refs[:k_pages]
    vp = refs[k_pages:2 * k_pages]
    o_ref, m_sc, l_sc, acc_sc = refs[2 * k_pages:]
    hd = NSA_HEAD_DIM
    j = pl.program_id(1)

    @pl.when(j == 0)
    def _():
        m_sc[...] = jnp.full(m_sc.shape, NEG_INF, F32)
        l_sc[...] = jnp.zeros(l_sc.shape, F32)
        acc_sc[...] = jnp.zeros(acc_sc.shape, F32)

    q = q_ref[...]
    expand = exp_ref[...]
    groups = range(NSA_KV_HEADS)
    s = []
    for g in groups:
        k_t = jnp.concatenate([r[g] for r in kp], axis=1).astype(BF16)
        sel_bias = ((selj_ref[g] - 1.0) * -NEG_INF).astype(BF16)
        s.append(_dot(_stack_heads(q, g).astype(BF16), k_t) + _tile_rows(_dot(sel_bias, expand)))
    m_run = [m_sc[g] for g in groups]
    m_new = [jnp.maximum(m_run[g], jnp.max(s[g], axis=-1, keepdims=True)) for g in groups]
    alpha = [jnp.exp2(m_run[g] - m_new[g]) for g in groups]
    e = [jnp.exp2(s[g] - m_new[g]) for g in groups]
    l_new = [alpha[g] * l_sc[g] + jnp.sum(e[g], axis=-1, keepdims=True) for g in groups]
    acc_new = []
    for g in groups:
        v_t = jnp.concatenate([r[g] for r in vp], axis=1).astype(BF16)
        acc_new.append(alpha[g] * acc_sc[g] + _dot_nt(e[g].astype(BF16), v_t))
    for g in groups:
        m_sc[g] = m_new[g]
        l_sc[g] = l_new[g]
        acc_sc[g] = acc_new[g]

    @pl.when(j == pl.num_programs(1) - 1)
    def _():
        pos4 = _tile_rows(past + _iota((tp, 1), 0))
        gates = jax.nn.sigmoid(sm_ref[...])
        zb = zb_ref[...]
        new_k = nk_ref[...]
        new_v = nv_ref[...]
        for g in range(NSA_KV_HEADS):
            gs = slice(g * hd, (g + 1) * hd)
            qs = _stack_heads(q, g)
            picked = _tile_rows(sel_ref[g][:, n_blocks - 1:n_blocks] > 0.5)
            m_run, l_run, acc = m_sc[g], l_sc[g], acc_sc[g]
            s_new, new_mask = [], []
            for r in range(t_valid):
                mk = picked & (past + r <= pos4)
                new_mask.append(mk)
                s_new.append(jnp.where(mk, _new_key_scores(qs, new_k, r, gs), NEG_INF))
            m_all = m_run
            for sn in s_new:
                m_all = jnp.maximum(m_all, sn)
            alpha = jnp.exp2(m_run - m_all)
            l_all = alpha * l_run
            acc = alpha * acc
            for r in range(t_valid):
                e_r = jnp.where(new_mask[r], jnp.exp2(s_new[r] - m_all), 0.0)
                l_all = l_all + e_r
                acc = acc + e_r * new_v[r:r + 1, gs]
            o_s = acc * _inv_denominator(l_all)
            _combine_heads(o_ref, g, gates, zb, oc_ref[g], o_s, ow_ref[g], tp)


def _nsa_sample_b(p3, sel, o_c, o_w, pool_k, pool_v, page_table, *, k_pages, past, t_valid, n_blocks, nbl):
    b, tp, _ = p3.shape
    n_pages = page_table.shape[1]
    page = pool_k.shape[3]
    n_steps = n_pages // k_pages
    n_keys = k_pages * page
    per_step = n_keys // SEL_BLOCK
    assert n_pages % k_pages == 0 and page % SEL_BLOCK == 0 and past == n_pages * page
    assert n_blocks == past // SEL_BLOCK + 1 and t_valid <= SEL_BLOCK and per_step <= LANES
    kvc = COL_KV // LANES
    rows = NSA_GROUP * tp
    sel_steps = sel[..., :n_steps * per_step].reshape(b, NSA_KV_HEADS, tp, n_steps, per_step)
    sel_steps = jnp.pad(sel_steps.transpose(0, 3, 1, 2, 4), ((0, 0),) * 4 + ((0, LANES - per_step),))
    expand = (jnp.arange(LANES)[:, None] == (jnp.arange(n_keys) // SEL_BLOCK)[None, :]).astype(BF16)
    blk = (None, NSA_KV_HEADS, NSA_HEAD_DIM, page)

    def page_spec(i):
        return pl.BlockSpec(blk, lambda bi, j, pt: (pt[bi, j * k_pages + i], 0, 0, 0))

    per_b4 = lambda bi, j, pt: (bi, 0, 0, 0)
    in_specs = [
        pl.BlockSpec((None, tp, NSA_DIM), lambda bi, j, pt: (bi, 0, COL_QB // NSA_DIM)),
        pl.BlockSpec((None, tp, NSA_DIM), lambda bi, j, pt: (bi, 0, COL_ZB // NSA_DIM)),
        pl.BlockSpec((None, tp, LANES), lambda bi, j, pt: (bi, 0, COL_SMALL // LANES)),
        pl.BlockSpec((None, tp, LANES), lambda bi, j, pt: (bi, 0, kvc + 2)),
        pl.BlockSpec((None, tp, LANES), lambda bi, j, pt: (bi, 0, kvc + 3)),
        pl.BlockSpec((None, NSA_KV_HEADS, tp, nbl), per_b4),
        pl.BlockSpec((None, None, NSA_KV_HEADS, tp, LANES), lambda bi, j, pt: (bi, j, 0, 0, 0)),
        pl.BlockSpec((LANES, n_keys), lambda bi, j, pt: (0, 0)),
        pl.BlockSpec((None, NSA_KV_HEADS, rows, NSA_HEAD_DIM), per_b4),
        pl.BlockSpec((None, NSA_KV_HEADS, rows, NSA_HEAD_DIM), per_b4),
    ] + [page_spec(i) for i in range(k_pages)] * 2
    kern = functools.partial(_nsa_sample_b_kernel, k_pages=k_pages, past=past, t_valid=t_valid, tp=tp,
                             n_blocks=n_blocks, nbl=nbl)
    return pl.pallas_call(
        kern,
        grid_spec=pltpu.PrefetchScalarGridSpec(
            num_scalar_prefetch=1, grid=(b, n_steps), in_specs=in_specs,
            out_specs=pl.BlockSpec((None, tp, NSA_DIM), lambda bi, j, pt: (bi, 0, 0)),
            scratch_shapes=[pltpu.VMEM((NSA_KV_HEADS, rows, 1), F32), pltpu.VMEM((NSA_KV_HEADS, rows, 1), F32),
                            pltpu.VMEM((NSA_KV_HEADS, rows, NSA_HEAD_DIM), F32)]),
        out_shape=jax.ShapeDtypeStruct((b, tp, NSA_DIM), F32),
        compiler_params=pltpu.CompilerParams(
            dimension_semantics=("parallel", "arbitrary"), vmem_limit_bytes=VMEM_LIMIT),
    )(page_table, p3, p3, p3, p3, p3, sel, sel_steps, expand, o_c, o_w,
      *([pool_k] * k_pages), *([pool_v] * k_pages))


def _merge_kernel(x_ref, oa_ref, ob_ref, ga_ref, gb_ref, wa_ref, wb_ref, wo_ref, fg_ref, y_ref):
    mixed = (jax.nn.sigmoid(ga_ref[...]) * _dot(oa_ref[...].astype(BF16), wa_ref[...])
             + jax.nn.sigmoid(gb_ref[...]) * _dot(ob_ref[...].astype(BF16), wb_ref[...]))
    x_new = x_ref[...] + _dot(mixed.astype(BF16), wo_ref[...])
    y = x_new * lax.rsqrt(jnp.mean(x_new * x_new, axis=-1, keepdims=True) + RMS_EPS)
    y_ref[...] = y * fg_ref[...]


def _merge(x2d, o_a, o_b, p2d, w_a, w_b, w_o, final_gain, tm):
    rows, d = x2d.shape
    const2 = lambda i: (0, 0)
    return pl.pallas_call(
        _merge_kernel,
        grid=(rows // tm,),
        in_specs=[
            pl.BlockSpec((tm, d), lambda i: (i, 0)),
            pl.BlockSpec((tm, GDN_DIM), lambda i: (i, 0)),
            pl.BlockSpec((tm, NSA_DIM), lambda i: (i, 0)),
            pl.BlockSpec((tm, d), lambda i: (i, COL_GATE_A // D_MODEL)),
            pl.BlockSpec((tm, d), lambda i: (i, COL_GATE_B // D_MODEL)),
            pl.BlockSpec((GDN_DIM, d), const2),
            pl.BlockSpec((NSA_DIM, d), const2),
            pl.BlockSpec((d, d), const2),
            pl.BlockSpec((1, d), const2),
        ],
        out_specs=pl.BlockSpec((tm, d), lambda i: (i, 0)),
        out_shape=jax.ShapeDtypeStruct((rows, d), F32),
        compiler_params=pltpu.CompilerParams(
            dimension_semantics=("parallel",), vmem_limit_bytes=VMEM_LIMIT),
    )(x2d, o_a, o_b, p2d, p2d, w_a, w_b, w_o, final_gain)


def _pack_w_in(w_in):
    bounds = [0]
    for width in (GDN_QKV_DIM, GDN_HEADS, GDN_HEADS, GDN_DIM, NSA_DIM, 6 * NSA_KV_DIM, 3 * NSA_HEADS,
                  NSA_DIM, 2 * D_MODEL):
        bounds.append(bounds[-1] + width)
    qkv, a, bt, z_a, q_b, kv_b, g_b, z_b, merge = (w_in[:, bounds[i]:bounds[i + 1]] for i in range(9))
    small = jnp.concatenate([a, bt, g_b], axis=1)
    small = jnp.pad(small, ((0, 0), (0, N_PACKED - COL_SMALL - small.shape[1])))
    packed = jnp.concatenate([qkv, z_a, merge, q_b, z_b, kv_b, small], axis=1)
    assert packed.shape[1] == N_PACKED
    return packed.astype(BF16)


def _lane_vec(v):
    return jnp.pad(v.astype(F32), (0, LANES - v.shape[0])).reshape(1, LANES)


def _block_diag2(w):
    z = jnp.zeros_like(w)
    return jnp.concatenate([jnp.concatenate([w, z], axis=1), jnp.concatenate([z, w], axis=1)], axis=0)


def _tile_for(n, candidates):
    for c in candidates:
        if n % c == 0:
            return c
    raise ValueError(f"no tile for {n}")


def kernel(x_prompt, x_sample, cache_k_cmp, cache_v_cmp, cache_k_slc, cache_v_slc, state_win_k, state_win_v,
           state_conv, state_gdn, page_table, norm_gain, w_in, conv_w, a_log, dt_bias, gdn_gain, cmp_pos_k,
           cmp_w_k, cmp_pos_v, cmp_w_v, w_branch_a, w_branch_b, w_out, final_gain):
    assert w_in.shape[0] == 1, "one layer"
    bp, tp_len, d = x_prompt.shape
    bs, ts, _ = x_sample.shape
    n_pool, page = cache_k_cmp.shape[1:3]
    n_pages = page_table.shape[1]
    past = n_pages * page
    assert past % CMP_STRIDE == 0 and ts < CMP_STRIDE and ts >= CONV_WIDTH - 1 and ts <= SUBLANES
    t_pad = SUBLANES

    w_packed = _pack_w_in(w_in[0])
    gain_in = norm_gain[0].reshape(1, d)
    fgain = final_gain.reshape(1, d)
    w_a = w_branch_a[0].astype(BF16)
    w_b = w_branch_b[0].astype(BF16)
    w_o = w_out[0].astype(BF16)
    cw = conv_w[0]
    alog = _lane_vec(a_log[0])
    dtb = _lane_vec(dt_bias[0])
    ggain = gdn_gain[0].reshape(1, GDN_HEAD_DIM)
    pos_k = jnp.tile(cmp_pos_k[0], (1, NSA_KV_HEADS))
    pos_v = jnp.tile(cmp_pos_v[0], (1, NSA_KV_HEADS))
    wk2 = _block_diag2(cmp_w_k[0])
    wv2 = _block_diag2(cmp_w_v[0])
    tn = _tile_for(N_PACKED, (2048, 1024, 512, 256))
    kvc = COL_KV

    rows_p = bp * tp_len
    x2p = x_prompt.reshape(rows_p, d)
    p2 = _proj(x2p, gain_in, w_packed, _tile_for(rows_p, (1024, 512, 256, 128)), tn)
    p3 = p2.reshape(bp, tp_len, N_PACKED)
    tb = _tile_for(tp_len, (256, 128, 64))
    o_a, gdn_p = _gdn(p3, cw, jnp.zeros((bp, SUBLANES, GDN_QKV_DIM), F32), alog, dtb, ggain,
                      jnp.zeros((bp, GDN_HEADS, GDN_HEAD_DIM, GDN_HEAD_DIM), F32),
                      tb=tb, chunk=GDN_CHUNK, t_valid=tp_len)
    o_b = _nsa_prompt(p3, pos_k, pos_v, wk2, wv2, tq=256, tk=_tile_for(tp_len, (512, 256, 128)))
    y_p = _merge(x2p, o_a.reshape(rows_p, GDN_DIM), o_b.reshape(rows_p, NSA_DIM), p2, w_a, w_b, w_o, fgain,
                 _tile_for(rows_p, (512, 256, 128)))
    y_prompt = y_p.reshape(bp, tp_len, d)

    def kv_out(p, i, t_keep_from, t_to):
        return p[:, t_keep_from:t_to, kvc + i * LANES:kvc + (i + 1) * LANES].reshape(
            1, p.shape[0], t_to - t_keep_from, NSA_KV_HEADS, NSA_HEAD_DIM)

    keep_p = min(WINDOW, tp_len)
    prompt_state = ([kv_out(p3, i, 0, tp_len) for i in range(4)]
                    + [kv_out(p3, i, tp_len - keep_p, tp_len) for i in (4, 5)]
                    + [p3[:, tp_len - (CONV_WIDTH - 1):, :GDN_QKV_DIM][None], gdn_p[None]])

    xs_pad = jnp.pad(x_sample, ((0, 0), (0, t_pad - ts), (0, 0)))
    rows_s = bs * t_pad
    x2s = xs_pad.reshape(rows_s, d)
    ps2 = _proj(x2s, gain_in, w_packed, _tile_for(rows_s, (512, 256, 128, 8)), tn)
    ps3 = ps2.reshape(bs, t_pad, N_PACKED)
    prefix_s = jnp.pad(state_conv[0], ((0, 0), (SUBLANES - (CONV_WIDTH - 1), 0), (0, 0)))
    o_a_s, gdn_s = _gdn(ps3, cw, prefix_s, alog, dtb, ggain, state_gdn[0], tb=t_pad, chunk=t_pad, t_valid=ts)

    pools = [c[0].transpose(0, 2, 3, 1) for c in (cache_k_cmp, cache_v_cmp, cache_k_slc, cache_v_slc)]
    kc_s, vc_s = _cmp_paged(pools[0], pools[1], page_table, pos_k, pos_v, wk2, wv2,
                            k_pages=_tile_for(n_pages, (32, 16, 8, 4, 2, 1)))
    n_blocks = -(-(past + ts) // SEL_BLOCK)
    nbl = -(-n_blocks // LANES) * LANES
    w_buf = state_win_k.shape[2]
    o_c_s, o_w_s, sel_s, win_k_s, win_v_s = _nsa_sample_a(
        ps3, kc_s, vc_s, state_win_k[0].reshape(bs, w_buf, NSA_KV_DIM), state_win_v[0].reshape(bs, w_buf, NSA_KV_DIM),
        past=past, t_valid=ts, n_blocks=n_blocks, nbl=nbl)
    o_b_s = _nsa_sample_b(ps3, sel_s, o_c_s, o_w_s, pools[2], pools[3], page_table,
                          k_pages=_tile_for(n_pages, (64, 32, 16, 8, 4, 2, 1)),
                          past=past, t_valid=ts, n_blocks=n_blocks, nbl=nbl)
    y_s = _merge(x2s, o_a_s.reshape(rows_s, GDN_DIM), o_b_s.reshape(rows_s, NSA_DIM), ps2, w_a, w_b, w_o, fgain,
                 _tile_for(rows_s, (512, 256, 128, 8)))
    y_sample = y_s.reshape(bs, t_pad, d)[:, :ts]

    keep_s = min(WINDOW, w_buf + ts)
    assert keep_s == w_buf
    sample_state = ([kv_out(ps3, i, 0, ts) for i in range(4)]
                    + [win_k_s.reshape(1, bs, keep_s, NSA_KV_HEADS, NSA_HEAD_DIM),
                       win_v_s.reshape(1, bs, keep_s, NSA_KV_HEADS, NSA_HEAD_DIM),
                       ps3[:, ts - (CONV_WIDTH - 1):ts, :GDN_QKV_DIM][None], gdn_s[None]])
    return (y_prompt, y_sample, *prompt_state, *sample_state)
```

```python
import functools
import math

import jax
import jax.numpy as jnp
from jax import lax
from jax.experimental import pallas as pl
from jax.experimental.pallas import tpu as pltpu

F32 = jnp.float32
BF16 = jnp.bfloat16
HI = lax.Precision.HIGHEST

LANES = 128
SUBLANES = 8
VMEM_LIMIT = 56 * 1024 * 1024

D_MODEL = 1024
GDN_HEADS = 8
GDN_HEAD_DIM = 128
GDN_DIM = GDN_HEADS * GDN_HEAD_DIM
GDN_QKV_DIM = 3 * GDN_DIM
CONV_WIDTH = 4
GDN_CHUNK = 64
NSA_HEADS = 8
NSA_KV_HEADS = 2
NSA_GROUP = NSA_HEADS // NSA_KV_HEADS
NSA_HEAD_DIM = 64
NSA_DIM = NSA_HEADS * NSA_HEAD_DIM
NSA_KV_DIM = NSA_KV_HEADS * NSA_HEAD_DIM
CMP_STRIDE = 16
CMP_LEN = 2 * CMP_STRIDE
SEL_BLOCK = 64
SEL_SHIFT = 6
SEL_RATIO = SEL_BLOCK // CMP_STRIDE
N_SELECT = 16
WINDOW = 512
FORCE_BONUS = 100.0
RMS_EPS = 1e-6
NEG_INF = -1e30
NSA_SCALE = NSA_HEAD_DIM ** -0.5
LOG2E = math.log2(math.e)

COL_QKV = 0
COL_ZA = 3072
COL_GATE_A = 4096
COL_GATE_B = 5120
COL_QB = 6144
COL_ZB = 6656
COL_KV = 7168
COL_SMALL = 7936
N_PACKED = 8192
SMALL_A = 0
SMALL_BT = GDN_HEADS
SMALL_GB = 2 * GDN_HEADS


def _dot(a, b, precision=None):
    return jnp.dot(a, b, precision=precision, preferred_element_type=F32)


def _dot_nt(a, b, precision=None):
    return lax.dot_general(a, b, (((1,), (1,)), ((), ())), precision=precision,
                           preferred_element_type=F32)


def _dot_tn(a, b, precision=None):
    return lax.dot_general(a, b, (((0,), (0,)), ((), ())), precision=precision,
                           preferred_element_type=F32)


def _mm(a, b):
    return jnp.dot(a.astype(BF16), b.astype(BF16), preferred_element_type=F32)


def _silu(x):
    return x * jax.nn.sigmoid(x)


def _iota(shape, dim):
    return lax.broadcasted_iota(jnp.int32, shape, dim)


def _proj_kernel(x_ref, g_ref, w_ref, o_ref, h_ref):
    @pl.when(pl.program_id(1) == 0)
    def _():
        x = x_ref[...]
        y = x * lax.rsqrt(jnp.mean(x * x, axis=-1, keepdims=True) + RMS_EPS)
        h_ref[...] = (y * g_ref[...]).astype(BF16)

    o_ref[...] = jnp.dot(h_ref[...], w_ref[...], preferred_element_type=F32)


def _proj(x2d, gain, w_packed, tm, tn):
    rows, d = x2d.shape
    n = w_packed.shape[1]
    return pl.pallas_call(
        _proj_kernel,
        grid=(rows // tm, n // tn),
        in_specs=[
            pl.BlockSpec((tm, d), lambda i, j: (i, 0)),
            pl.BlockSpec((1, d), lambda i, j: (0, 0)),
            pl.BlockSpec((d, tn), lambda i, j: (0, j)),
        ],
        out_specs=pl.BlockSpec((tm, tn), lambda i, j: (i, j)),
        out_shape=jax.ShapeDtypeStruct((rows, n), F32),
        scratch_shapes=[pltpu.VMEM((tm, d), BF16)],
        compiler_params=pltpu.CompilerParams(
            dimension_semantics=("parallel", "arbitrary"), vmem_limit_bytes=VMEM_LIMIT),
    )(x2d, gain, w_packed)


def _gdn_kernel(q_ref, k_ref, v_ref, z_ref, sm_ref, cw_ref, pre_ref, alog_ref, dtb_ref, gain_ref, s0_ref,
                o_ref, sfin_ref,
                s_sc, tail_sc, q_sc, k_sc, v_sc, gc_sc, b_sc, gct_sc, *, tb, chunk, t_valid):
    t = pl.program_id(1)
    c = chunk
    hd = GDN_HEAD_DIM
    nh = GDN_HEADS

    @pl.when(t == 0)
    def _():
        s_sc[...] = s0_ref[...]
        tail_sc[...] = pre_ref[...]

    for idx, (x_ref, dst) in enumerate(((q_ref, q_sc), (k_ref, k_sc), (v_ref, v_sc))):
        cols = slice(idx * GDN_DIM, (idx + 1) * GDN_DIM)
        x = x_ref[...]
        cw = cw_ref[:, cols]
        last = CONV_WIDTH - 1
        head = jnp.concatenate([tail_sc[:, cols], x[0:SUBLANES]], axis=0)
        tail_sc[:, cols] = x[tb - SUBLANES:tb]
        y = cw[last:last + 1] * head[SUBLANES:2 * SUBLANES]
        for j in range(1, CONV_WIDTH):
            y = y + cw[last - j:last - j + 1] * head[SUBLANES - j:2 * SUBLANES - j]
        if tb > SUBLANES:
            rest = cw[last:last + 1] * x
            for j in range(1, CONV_WIDTH):
                rest = rest + cw[last - j:last - j + 1] * pltpu.roll(x, j, 0)
            y = jnp.concatenate([y, rest[SUBLANES:]], axis=0)
        y = _silu(y)
        if idx == 2:
            dst[...] = y
        else:
            scale = GDN_HEAD_DIM ** -0.5 if idx == 0 else 1.0
            for h in range(nh):
                yh = y[:, h * hd:(h + 1) * hd]
                dst[:, h * hd:(h + 1) * hd] = yh * (lax.rsqrt(jnp.sum(yh * yh, axis=-1, keepdims=True) + 1e-6) * scale)

    sm = sm_ref[...]
    xx = sm + dtb_ref[...]
    softplus = jnp.maximum(xx, 0.0) + jnp.log(1.0 + jnp.exp(-jnp.abs(xx)))
    row = _iota((tb, 1), 0)
    live = t * tb + row < t_valid
    g_small = jnp.where(live, -jnp.exp(alog_ref[...]) * softplus, 0.0)
    beta_small = jnp.where(live, jax.nn.sigmoid(sm), 0.0)
    row_in_chunk = row & (c - 1)
    gc_small = g_small
    shift = 1
    while shift < c:
        gc_small = gc_small + jnp.where(row_in_chunk >= shift, pltpu.roll(gc_small, shift, 0), 0.0)
        shift *= 2
    li = _iota((LANES, GDN_DIM), 0)
    head_of_lane = _iota((LANES, GDN_DIM), 1) >> int(math.log2(hd))

    def spread(x, first_lane):
        onehot = (li == head_of_lane + first_lane).astype(BF16)
        hi = x.astype(BF16)
        r1 = x - hi.astype(F32)
        mid = r1.astype(BF16)
        lo = (r1 - mid.astype(F32)).astype(BF16)
        return _dot(hi, onehot) + _dot(mid, onehot) + _dot(lo, onehot)

    gc_sc[...] = spread(gc_small, SMALL_A)
    b_sc[...] = spread(beta_small, SMALL_BT)
    for r0 in range(0, tb, LANES):
        n = min(LANES, tb - r0)
        part = gc_small[r0:r0 + n]
        if n < LANES:
            part = jnp.concatenate([part, jnp.zeros((LANES - n, LANES), F32)], axis=0)
        part_t = part.T
        for off in range(0, n, c):
            gct_sc[(r0 + off) // c] = part_t[:, off:off + c]

    ri = _iota((c, c), 0)
    ci = _iota((c, c), 1)
    causal = ri >= ci
    strict = ri > ci
    eye_f = (ri == ci).astype(F32)
    n_doubling = int(math.log2(c)) - 1
    gain = gain_ref[...]

    heads = range(nh)
    hs = [slice(h * hd, (h + 1) * hd) for h in heads]
    n_chunks = tb // c
    rw = [slice(ck * c, (ck + 1) * c) for ck in range(n_chunks)]
    pairs = [(ck, h) for ck in range(n_chunks) for h in heads]

    kq = {}
    for ck, h in pairs:
        kk = k_sc[rw[ck], hs[h]].astype(BF16)
        kq[ck, h] = _dot_nt(jnp.concatenate([kk, q_sc[rw[ck], hs[h]].astype(BF16)], axis=0), kk)
    decay = {(ck, h): jnp.where(causal, jnp.exp(gc_sc[rw[ck], hs[h]][:, :c] - gct_sc[ck, h:h + 1, :]), 0.0)
             for ck, h in pairs}
    attn = {p: kq[p][c:] * decay[p] for p in pairs}
    nk = {(ck, h): jnp.where(strict, -(b_sc[rw[ck], hs[h]][:, :c] * kq[ck, h][:c] * decay[ck, h]), 0.0)
          for ck, h in pairs}
    tinv = {p: eye_f + nk[p] for p in pairs}
    for _ in range(n_doubling):
        nk = {p: _mm(nk[p], nk[p]) for p in pairs}
        tinv = {p: tinv[p] + _mm(tinv[p], nk[p]) for p in pairs}
    uw = {}
    for ck, h in pairs:
        bb = b_sc[rw[ck], hs[h]]
        rhs = jnp.concatenate([v_sc[rw[ck], hs[h]] * bb,
                               k_sc[rw[ck], hs[h]] * (bb * jnp.exp(gc_sc[rw[ck], hs[h]]))], axis=1)
        uw[ck, h] = _mm(tinv[ck, h], rhs)

    state = [s_sc[h] for h in heads]
    for ck in range(n_chunks):
        rows = rw[ck]
        ws_qs = []
        for h in heads:
            q_dec = q_sc[rows, hs[h]] * jnp.exp(gc_sc[rows, hs[h]])
            ws_qs.append(_mm(jnp.concatenate([uw[ck, h][:, hd:], q_dec], axis=0), state[h]))
        v_new = [uw[ck, h][:, :hd] - ws_qs[h][:c] for h in heads]
        o = [ws_qs[h][c:] + _mm(attn[ck, h], v_new[h]) for h in heads]
        new_state = []
        for h in heads:
            gc = gc_sc[rows, hs[h]]
            g_last = gc[c - 1:c, :]
            k_dec = k_sc[rows, hs[h]] * jnp.exp(g_last - gc)
            new_state.append(state[h] * jnp.exp(g_last) + _dot_tn(k_dec, v_new[h]))
        state = new_state
        for h in heads:
            on = o[h] * lax.rsqrt(jnp.mean(o[h] * o[h], axis=-1, keepdims=True) + RMS_EPS) * gain
            o_ref[rows, hs[h]] = on * _silu(z_ref[rows, hs[h]])
    for h in heads:
        s_sc[h] = state[h]

    @pl.when(t == pl.num_programs(1) - 1)
    def _():
        sfin_ref[...] = s_sc[...]


def _gdn(p3, conv_w, prefix, alog, dtb, gain, s0, *, tb, chunk, t_valid):
    b, t, _ = p3.shape
    hd = GDN_HEAD_DIM
    gd = GDN_DIM
    const2 = lambda bi, ti: (0, 0)
    kern = functools.partial(_gdn_kernel, tb=tb, chunk=chunk, t_valid=t_valid)
    return pl.pallas_call(
        kern,
        grid=(b, t // tb),
        in_specs=[
            pl.BlockSpec((None, tb, gd), lambda bi, ti: (bi, ti, COL_QKV // gd)),
            pl.BlockSpec((None, tb, gd), lambda bi, ti: (bi, ti, COL_QKV // gd + 1)),
            pl.BlockSpec((None, tb, gd), lambda bi, ti: (bi, ti, COL_QKV // gd + 2)),
            pl.BlockSpec((None, tb, gd), lambda bi, ti: (bi, ti, COL_ZA // gd)),
            pl.BlockSpec((None, tb, LANES), lambda bi, ti: (bi, ti, COL_SMALL // LANES)),
            pl.BlockSpec((CONV_WIDTH, GDN_QKV_DIM), const2),
            pl.BlockSpec((None, SUBLANES, GDN_QKV_DIM), lambda bi, ti: (bi, 0, 0)),
            pl.BlockSpec((1, LANES), const2),
            pl.BlockSpec((1, LANES), const2),
            pl.BlockSpec((1, hd), const2),
            pl.BlockSpec((None, GDN_HEADS, hd, hd), lambda bi, ti: (bi, 0, 0, 0)),
        ],
        out_specs=[
            pl.BlockSpec((None, tb, gd), lambda bi, ti: (bi, ti, 0)),
            pl.BlockSpec((None, GDN_HEADS, hd, hd), lambda bi, ti: (bi, 0, 0, 0)),
        ],
        out_shape=[
            jax.ShapeDtypeStruct((b, t, gd), F32),
            jax.ShapeDtypeStruct((b, GDN_HEADS, hd, hd), F32),
        ],
        scratch_shapes=[
            pltpu.VMEM((GDN_HEADS, hd, hd), F32),
            pltpu.VMEM((SUBLANES, GDN_QKV_DIM), F32),
            pltpu.VMEM((tb, gd), F32),
            pltpu.VMEM((tb, gd), F32),
            pltpu.VMEM((tb, gd), F32),
            pltpu.VMEM((tb, gd), F32),
            pltpu.VMEM((tb, gd), F32),
            pltpu.VMEM((tb // chunk, LANES, chunk), F32),
        ],
        compiler_params=pltpu.CompilerParams(
            dimension_semantics=("parallel", "arbitrary"), vmem_limit_bytes=VMEM_LIMIT),
    )(p3, p3, p3, p3, p3, conv_w, prefix, alog, dtb, gain, s0)


def _inv_denominator(total):
    return 1.0 / jnp.maximum(total, 1e-30)


def _masked_softmax2(s, mask):
    s = jnp.where(mask, s, NEG_INF)
    e = jnp.where(mask, jnp.exp2(s - jnp.max(s, axis=-1, keepdims=True)), 0.0)
    return e * _inv_denominator(jnp.sum(e, axis=-1, keepdims=True))


def _stack_heads(q, g):
    hd = NSA_HEAD_DIM
    parts = [q[:, (NSA_GROUP * g + h) * hd:(NSA_GROUP * g + h + 1) * hd] for h in range(NSA_GROUP)]
    return jnp.concatenate(parts, axis=0) * (NSA_SCALE * LOG2E)


def _tile_rows(x):
    return jnp.concatenate([x] * NSA_GROUP, axis=0)


def _group_importance(p_c, tq):
    imp = p_c[0:tq]
    for h in range(1, NSA_GROUP):
        imp = imp + p_c[h * tq:(h + 1) * tq]
    return imp


def _select_blocks(p_c, pos, tq, n_blocks, nbl):
    m = p_c.shape[1]
    imp = _group_importance(p_c, tq)
    pool = ((_iota((m, nbl), 0) >> int(math.log2(SEL_RATIO))) == _iota((m, nbl), 1)).astype(F32)
    imp_sel = _dot(imp, pool, HI)
    lane = _iota((tq, nbl), 1)
    valid = lane * SEL_BLOCK <= pos
    cur = pos >> SEL_SHIFT
    forced = (lane == 0) | (lane == cur) | (lane == cur - 1)
    score = jnp.where(valid, imp_sel + FORCE_BONUS * forced.astype(F32), -1.0)
    score = jnp.where(lane < n_blocks, score, -2.0)
    rank = jnp.zeros((tq, nbl), F32)
    for j in range(n_blocks):
        sj = score[:, j:j + 1]
        rank = rank + ((sj > score) | ((sj == score) & (lane > j))).astype(F32)
    return (rank < float(min(N_SELECT, n_blocks))).astype(F32)


def _select_blocks_t(p_c, q0, tq, n_blocks, nbl):
    m = p_c.shape[1]
    n_sel = min(N_SELECT, n_blocks)
    nbs = -(-n_blocks // SUBLANES) * SUBLANES
    imp = _group_importance(p_c, tq)
    pool_t = ((_iota((nbs, m), 1) >> int(math.log2(SEL_RATIO))) == _iota((nbs, m), 0)).astype(F32)
    imp_t = _dot_nt(pool_t, imp, HI)
    pos_r = q0 + _iota((1, tq), 1)
    blk = _iota((nbs, tq), 0)
    valid = blk * SEL_BLOCK <= pos_r

    def ranked():
        cur = pos_r >> SEL_SHIFT
        forced = (blk == 0) | (blk == cur) | (blk == cur - 1)
        score = jnp.where(valid, imp_t + FORCE_BONUS * forced.astype(F32), -1.0)
        score = jnp.where(blk < n_blocks, score, -2.0)
        rank = jnp.zeros((nbs, tq), F32)
        for j in range(n_blocks):
            sj = score[j:j + 1, :]
            rank = rank + ((sj > score) | ((sj == score) & (blk > j))).astype(F32)
        return (rank < float(n_sel)).astype(F32)

    sel_t = lax.cond(q0 + tq <= n_sel * SEL_BLOCK, lambda: valid.astype(F32), ranked)
    sel_t = jnp.concatenate([sel_t, jnp.zeros((nbl - nbs, tq), F32)], axis=0)
    return sel_t.T


def _compress_rows(x, pw):
    n = x.shape[0] // CMP_STRIDE
    x3 = x.reshape(n, CMP_STRIDE, LANES)
    lo = jnp.sum(x3 * pw[:CMP_STRIDE][None], axis=1)
    hi = jnp.sum(x3 * pw[CMP_STRIDE:][None], axis=1)
    return lo, hi


def _combine_heads(o_ref, g, gates, zb, o_c, o_s, o_w, tq):
    hd = NSA_HEAD_DIM
    for h in range(NSA_GROUP):
        hh = NSA_GROUP * g + h
        c0 = SMALL_GB + 3 * hh
        rows = slice(h * tq, (h + 1) * tq)
        o_h = (gates[:, c0:c0 + 1] * o_c[rows] + gates[:, c0 + 1:c0 + 2] * o_s[rows]
               + gates[:, c0 + 2:c0 + 3] * o_w[rows])
        o_ref[:, hh * hd:(hh + 1) * hd] = o_h * _silu(zb[:, hh * hd:(hh + 1) * hd])


def _nsa_prompt_kernel(q_ref, zb_ref, sm_ref, kcmp_ref, vcmp_ref, kslc_ref, vslc_ref, kwin_ref, vwin_ref,
                       pk_ref, pv_ref, wk_ref, wv_ref, o_ref, kc_sc, vc_sc, kcx_sc, kx_sc, kw_sc, vx_sc, *,
                       t_len, tq, tk, span, nbl, nbo):
    i = pl.program_id(1)
    m = t_len // CMP_STRIDE
    n_blocks = t_len // SEL_BLOCK
    hd = NSA_HEAD_DIM

    @pl.when(i == 0)
    def _():
        for src, pw_ref, w_ref, dst in ((kcmp_ref, pk_ref, wk_ref, kc_sc), (vcmp_ref, pv_ref, wv_ref, vc_sc)):
            lo, hi = _compress_rows(src[...], pw_ref[...])
            hi_next = pltpu.roll(hi, m - 1, 0)
            pre = jnp.where(_iota((m, LANES), 0) < m - 1, lo + hi_next, 0.0)
            dst[...] = _dot(pre, w_ref[...], HI)
        kc = kc_sc[...]
        kc_hi = kc.astype(BF16)
        kc_lo = (kc - kc_hi.astype(F32)).astype(BF16)
        for g in range(NSA_KV_HEADS):
            gs = slice(g * hd, (g + 1) * hd)
            kcx_sc[g] = jnp.concatenate([kc_hi[:, gs], kc_hi[:, gs], kc_lo[:, gs], jnp.zeros((m, hd), BF16)], axis=1)
        k_slc = kslc_ref[...].astype(BF16)
        blk_onehot = ((_iota((t_len, nbo), 0) >> SEL_SHIFT) == _iota((t_len, nbo), 1)).astype(BF16)
        for g in range(NSA_KV_HEADS):
            kx_sc[g] = jnp.concatenate([k_slc[:, g * hd:(g + 1) * hd], blk_onehot], axis=1)
        kw_sc[...] = kwin_ref[...].astype(BF16)
        ones = jnp.ones((t_len, hd), BF16)
        for idx, src in enumerate((vslc_ref, vwin_ref)):
            v = src[...].astype(BF16)
            for g in range(NSA_KV_HEADS):
                vx_sc[idx, g] = jnp.concatenate([v[:, g * hd:(g + 1) * hd], ones], axis=1)

    q0 = i * tq
    pos = q0 + _iota((tq, 1), 0)
    pos4 = _tile_rows(pos)
    q = q_ref[...]
    zb = zb_ref[...]
    gates = jax.nn.sigmoid(sm_ref[...])
    c_end = _iota((1, m), 1) * CMP_STRIDE + (CMP_LEN - 1)
    n_full = q0 // tk
    w_start = pl.multiple_of(jnp.clip(q0 - WINDOW, 0, t_len - span), tq)
    w_dist = pos - (w_start + _iota((1, span), 1))
    w_bias = _tile_rows(jnp.where((w_dist >= 0) & (w_dist <= WINDOW), 0.0, NEG_INF))

    groups = range(NSA_KV_HEADS)
    gsl = [slice(g * hd, (g + 1) * hd) for g in groups]
    qs = [_stack_heads(q, g) for g in groups]
    qs_bf = [qs[g].astype(BF16) for g in groups]
    q_split = []
    for g in groups:
        qs_lo = (qs[g] - qs_bf[g].astype(F32)).astype(BF16)
        q_split.append(jnp.concatenate([qs_bf[g], qs_lo, qs_bf[g], jnp.zeros_like(qs_bf[g])], axis=1))
    s_c = [_dot_nt(q_split[g], kcx_sc[g]) for g in groups]
    p_c = [_masked_softmax2(s_c[g], c_end <= pos4) for g in groups]
    o_c = [_dot(p_c[g], vc_sc[:, gsl[g]]) for g in groups]
    sel = [_select_blocks_t(p_c[g], q0, tq, n_blocks, nbl) for g in groups]
    q_aug = []
    for g in groups:
        sel_bias = ((sel[g][:, :nbo] - 1.0) * -NEG_INF).astype(BF16)
        q_aug.append(jnp.concatenate([qs_bf[g], _tile_rows(sel_bias)], axis=1))

    def sel_tile(kt, carry, causal_tile):
        k0 = pl.multiple_of(kt * tk, tk)
        s = [_dot_nt(q_aug[g], kx_sc[g, pl.ds(k0, tk), :]) for g in groups]
        if causal_tile:
            causal_bias = _tile_rows(jnp.where(k0 + _iota((1, tk), 1) <= pos, 0.0, NEG_INF))
            s = [s[g] + causal_bias for g in groups]
        m_new = [jnp.maximum(carry[2 * g], jnp.max(s[g], axis=-1, keepdims=True)) for g in groups]
        e = [jnp.exp2(s[g] - m_new[g]).astype(BF16) for g in groups]
        out = []
        for g in groups:
            acc_new = (jnp.exp2(carry[2 * g] - m_new[g]) * carry[2 * g + 1]
                       + _dot(e[g], vx_sc[0, g, pl.ds(k0, tk), :]))
            out += [m_new[g], acc_new]
        return tuple(out)

    init = (jnp.full((NSA_GROUP * tq, 1), NEG_INF, F32), jnp.zeros((NSA_GROUP * tq, 2 * hd), F32)) * NSA_KV_HEADS
    carry = lax.fori_loop(0, n_full, functools.partial(sel_tile, causal_tile=False), init)
    carry = sel_tile(n_full, carry, True)
    o_s = [carry[2 * g + 1][:, :hd] * _inv_denominator(carry[2 * g + 1][:, hd:hd + 1]) for g in groups]

    s_w = [_dot_nt(qs_bf[g], kw_sc[pl.ds(w_start, span), gsl[g]]) + w_bias for g in groups]
    e_w = [jnp.exp2(s_w[g] - jnp.max(s_w[g], axis=-1, keepdims=True)).astype(BF16) for g in groups]
    acc_w = [_dot(e_w[g], vx_sc[1, g, pl.ds(w_start, span), :]) for g in groups]
    o_w = [acc_w[g][:, :hd] * _inv_denominator(acc_w[g][:, hd:hd + 1]) for g in groups]
    for g in groups:
        _combine_heads(o_ref, g, gates, zb, o_c[g], o_s[g], o_w[g], tq)


def _nsa_prompt(p3, pos_k, pos_v, w_k, w_v, *, tq, tk):
    b, t, _ = p3.shape
    span = WINDOW + tq
    assert t % tq == 0 and t % tk == 0 and t >= span and t % SEL_BLOCK == 0 and tq % LANES == 0
    assert WINDOW % tq == 0 and tk % tq == 0
    nbo = -(-(t // SEL_BLOCK) // NSA_HEAD_DIM) * NSA_HEAD_DIM
    assert (NSA_HEAD_DIM + nbo) % LANES == 0
    m = t // CMP_STRIDE
    nbl = -(-(t // SEL_BLOCK) // LANES) * LANES
    kvc = COL_KV // LANES

    def full(c):
        return pl.BlockSpec((None, t, LANES), lambda bi, i: (bi, 0, c))

    const2 = lambda bi, i: (0, 0)
    kern = functools.partial(_nsa_prompt_kernel, t_len=t, tq=tq, tk=tk, span=span, nbl=nbl, nbo=nbo)
    return pl.pallas_call(
        kern,
        grid=(b, t // tq),
        in_specs=[
            pl.BlockSpec((None, tq, NSA_DIM), lambda bi, i: (bi, i, COL_QB // NSA_DIM)),
            pl.BlockSpec((None, tq, NSA_DIM), lambda bi, i: (bi, i, COL_ZB // NSA_DIM)),
            pl.BlockSpec((None, tq, LANES), lambda bi, i: (bi, i, COL_SMALL // LANES)),
            full(kvc), full(kvc + 1), full(kvc + 2), full(kvc + 3), full(kvc + 4), full(kvc + 5),
            pl.BlockSpec((CMP_LEN, LANES), const2),
            pl.BlockSpec((CMP_LEN, LANES), const2),
            pl.BlockSpec((LANES, LANES), const2),
            pl.BlockSpec((LANES, LANES), const2),
        ],
        out_specs=pl.BlockSpec((None, tq, NSA_DIM), lambda bi, i: (bi, i, 0)),
        out_shape=jax.ShapeDtypeStruct((b, t, NSA_DIM), F32),
        scratch_shapes=[pltpu.VMEM((m, LANES), F32), pltpu.VMEM((m, LANES), F32),
                        pltpu.VMEM((NSA_KV_HEADS, m, 4 * NSA_HEAD_DIM), BF16),
                        pltpu.VMEM((NSA_KV_HEADS, t, NSA_HEAD_DIM + nbo), BF16),
                        pltpu.VMEM((t, LANES), BF16),
                        pltpu.VMEM((2, NSA_KV_HEADS, t, 2 * NSA_HEAD_DIM), BF16)],
        compiler_params=pltpu.CompilerParams(
            dimension_semantics=("parallel", "arbitrary"), vmem_limit_bytes=VMEM_LIMIT),
    )(p3, p3, p3, p3, p3, p3, p3, p3, p3, pos_k, pos_v, w_k, w_v)


def _cmp_paged_kernel(pt_ref, *refs, k_pages):
    del pt_ref
    kp = refs[:k_pages]
    kn = refs[k_pages]
    vp = refs[k_pages + 1:2 * k_pages + 1]
    vn = refs[2 * k_pages + 1]
    pk_ref, pv_ref, wk_ref, wv_ref, kc_ref, vc_ref = refs[2 * k_pages + 2:]
    n = kc_ref.shape[0]
    last_row = _iota((n, LANES), 0) == n - 1

    def row_major(r):
        return r[...].reshape(LANES, LANES).T

    for pages, nxt, pw_ref, w_ref, out in ((kp, kn, pk_ref, wk_ref, kc_ref), (vp, vn, pv_ref, wv_ref, vc_ref)):
        pw = pw_ref[...]
        parts = [_compress_rows(row_major(r), pw) for r in pages]
        lo = jnp.concatenate([p[0] for p in parts], axis=0)
        hi = jnp.concatenate([p[1] for p in parts], axis=0)
        hi_head = jnp.sum(row_major(nxt)[:CMP_STRIDE] * pw[CMP_STRIDE:], axis=0, keepdims=True)
        hi_next = jnp.where(last_row, hi_head, pltpu.roll(hi, n - 1, 0))
        out[...] = _dot(lo + hi_next, w_ref[...], HI)


def _cmp_paged(pool_k, pool_v, page_table, pos_k, pos_v, w_k, w_v, *, k_pages):
    b, n_pages = page_table.shape
    page = pool_k.shape[3]
    per_step = k_pages * page // CMP_STRIDE
    assert page == LANES and n_pages % k_pages == 0
    blk = (None, NSA_KV_HEADS, NSA_HEAD_DIM, page)

    def page_spec(i):
        return pl.BlockSpec(blk, lambda bi, j, pt: (pt[bi, j * k_pages + i], 0, 0, 0))

    next_spec = pl.BlockSpec(
        blk, lambda bi, j, pt: (pt[bi, jnp.minimum((j + 1) * k_pages, n_pages - 1)], 0, 0, 0))
    const2 = lambda bi, j, pt: (0, 0)
    in_specs = ([page_spec(i) for i in range(k_pages)] + [next_spec]
                + [page_spec(i) for i in range(k_pages)] + [next_spec]
                + [pl.BlockSpec((CMP_LEN, LANES), const2)] * 2 + [pl.BlockSpec((LANES, LANES), const2)] * 2)
    out_spec = pl.BlockSpec((None, per_step, LANES), lambda bi, j, pt: (bi, j, 0))
    m = n_pages * page // CMP_STRIDE
    return pl.pallas_call(
        functools.partial(_cmp_paged_kernel, k_pages=k_pages),
        grid_spec=pltpu.PrefetchScalarGridSpec(
            num_scalar_prefetch=1, grid=(b, n_pages // k_pages), in_specs=in_specs,
            out_specs=[out_spec, out_spec]),
        out_shape=[jax.ShapeDtypeStruct((b, m, LANES), F32)] * 2,
        compiler_params=pltpu.CompilerParams(
            dimension_semantics=("parallel", "arbitrary"), vmem_limit_bytes=VMEM_LIMIT),
    )(page_table, *([pool_k] * (k_pages + 1)), *([pool_v] * (k_pages + 1)), pos_k, pos_v, w_k, w_v)


def _new_key_scores(qs, new_k, r, gs):
    return jnp.sum(qs * new_k[r:r + 1, gs], axis=-1, keepdims=True)


def _nsa_sample_a_kernel(q_ref, kc_ref, vc_ref, bk_ref, bv_ref, nk_ref, nv_ref,
                         oc_ref, ow_ref, sel_ref, wk_out, wv_out, *, past, t_valid, tp, n_blocks, nbl):
    hd = NSA_HEAD_DIM
    m = kc_ref.shape[0]
    w_buf = bk_ref.shape[0]
    pos = past + _iota((tp, 1), 0)
    pos4 = _tile_rows(pos)
    q = q_ref[...]
    new_k = nk_ref[...]
    new_v = nv_ref[...]
    c_end = _iota((1, m), 1) * CMP_STRIDE + (CMP_LEN - 1)
    buf_pos = (past - w_buf) + _iota((1, w_buf), 1)
    buf_dist = pos4 - buf_pos
    buf_mask = (buf_dist >= 0) & (buf_dist <= WINDOW)
    for g in range(NSA_KV_HEADS):
        gs = slice(g * hd, (g + 1) * hd)
        qs = _stack_heads(q, g)
        p_c = _masked_softmax2(_dot_nt(qs, kc_ref[:, gs], HI), c_end <= pos4)
        oc_ref[g] = _dot(p_c, vc_ref[:, gs])
        sel_ref[g] = _select_blocks(p_c, pos, tp, n_blocks, nbl)
        s_buf = jnp.where(buf_mask, _dot_nt(qs, bk_ref[:, gs]), NEG_INF)
        s_new, new_mask = [], []
        for r in range(t_valid):
            dist = pos4 - (past + r)
            mk = (dist >= 0) & (dist <= WINDOW)
            new_mask.append(mk)
            s_new.append(jnp.where(mk, _new_key_scores(qs, new_k, r, gs), NEG_INF))
        m_all = jnp.max(s_buf, axis=-1, keepdims=True)
        for sn in s_new:
            m_all = jnp.maximum(m_all, sn)
        e_buf = jnp.where(buf_mask, jnp.exp2(s_buf - m_all), 0.0)
        l_all = jnp.sum(e_buf, axis=-1, keepdims=True)
        acc = _dot(e_buf, bv_ref[:, gs])
        for r in range(t_valid):
            e_r = jnp.where(new_mask[r], jnp.exp2(s_new[r] - m_all), 0.0)
            l_all = l_all + e_r
            acc = acc + e_r * new_v[r:r + 1, gs]
        ow_ref[g] = acc * _inv_denominator(l_all)
    wk_out[0:w_buf - t_valid, :] = bk_ref[t_valid:w_buf, :]
    wk_out[w_buf - t_valid:w_buf, :] = new_k[0:t_valid]
    wv_out[0:w_buf - t_valid, :] = bv_ref[t_valid:w_buf, :]
    wv_out[w_buf - t_valid:w_buf, :] = new_v[0:t_valid]


def _nsa_sample_a(p3, kc, vc, buf_k, buf_v, *, past, t_valid, n_blocks, nbl):
    b, tp, _ = p3.shape
    m = kc.shape[1]
    w_buf = buf_k.shape[1]
    assert w_buf == WINDOW and t_valid <= w_buf
    kvc = COL_KV // LANES
    rows = NSA_GROUP * tp
    kern = functools.partial(_nsa_sample_a_kernel, past=past, t_valid=t_valid, tp=tp,
                             n_blocks=n_blocks, nbl=nbl)
    per_b3 = lambda bi: (bi, 0, 0)
    per_b4 = lambda bi: (bi, 0, 0, 0)
    return pl.pallas_call(
        kern,
        grid=(b,),
        in_specs=[
            pl.BlockSpec((None, tp, NSA_DIM), lambda bi: (bi, 0, COL_QB // NSA_DIM)),
            pl.BlockSpec((None, m, LANES), per_b3),
            pl.BlockSpec((None, m, LANES), per_b3),
            pl.BlockSpec((None, w_buf, LANES), per_b3),
            pl.BlockSpec((None, w_buf, LANES), per_b3),
            pl.BlockSpec((None, tp, LANES), lambda bi: (bi, 0, kvc + 4)),
            pl.BlockSpec((None, tp, LANES), lambda bi: (bi, 0, kvc + 5)),
        ],
        out_specs=[
            pl.BlockSpec((None, NSA_KV_HEADS, rows, NSA_HEAD_DIM), per_b4),
            pl.BlockSpec((None, NSA_KV_HEADS, rows, NSA_HEAD_DIM), per_b4),
            pl.BlockSpec((None, NSA_KV_HEADS, tp, nbl), per_b4),
            pl.BlockSpec((None, w_buf, LANES), per_b3),
            pl.BlockSpec((None, w_buf, LANES), per_b3),
        ],
        out_shape=[
            jax.ShapeDtypeStruct((b, NSA_KV_HEADS, rows, NSA_HEAD_DIM), F32),
            jax.ShapeDtypeStruct((b, NSA_KV_HEADS, rows, NSA_HEAD_DIM), F32),
            jax.ShapeDtypeStruct((b, NSA_KV_HEADS, tp, nbl), F32),
            jax.ShapeDtypeStruct((b, w_buf, LANES), F32),
            jax.ShapeDtypeStruct((b, w_buf, LANES), F32),
        ],
        compiler_params=pltpu.CompilerParams(
            dimension_semantics=("parallel",), vmem_limit_bytes=VMEM_LIMIT),
    )(p3, kc, vc, buf_k, buf_v, p3, p3)


def _nsa_sample_b_kernel(pt_ref, q_ref, zb_ref, sm_ref, nk_ref, nv_ref, sel_ref, selj_ref, exp_ref,
                         oc_ref, ow_ref, *refs, k_pages, past, t_valid, tp, n_blocks, nbl):
    del pt_ref
    kp = refs[:k_pages]
    vp = refs[k_pages:2 * k_pages]
    o_ref, m_sc, l_sc, acc_sc = refs[2 * k_pages:]
    hd = NSA_HEAD_DIM
    j = pl.program_id(1)

    @pl.when(j == 0)
    def _():
        m_sc[...] = jnp.full(m_sc.shape, NEG_INF, F32)
        l_sc[...] = jnp.zeros(l_sc.shape, F32)
        acc_sc[...] = jnp.zeros(acc_sc.shape, F32)

    q = q_ref[...]
    expand = exp_ref[...]
    groups = range(NSA_KV_HEADS)
    s = []
    for g in groups:
        k_t = jnp.concatenate([r[g] for r in kp], axis=1).astype(BF16)
        sel_bias = ((selj_ref[g] - 1.0) * -NEG_INF).astype(BF16)
        s.append(_dot(_stack_heads(q, g).astype(BF16), k_t) + _tile_rows(_dot(sel_bias, expand)))
    m_run = [m_sc[g] for g in groups]
    m_new = [jnp.maximum(m_run[g], jnp.max(s[g], axis=-1, keepdims=True)) for g in groups]
    alpha = [jnp.exp2(m_run[g] - m_new[g]) for g in groups]
    e = [jnp.exp2(s[g] - m_new[g]) for g in groups]
    l_new = [alpha[g] * l_sc[g] + jnp.sum(e[g], axis=-1, keepdims=True) for g in groups]
    acc_new = []
    for g in groups:
        v_t = jnp.concatenate([r[g] for r in vp], axis=1).astype(BF16)
        acc_new.append(alpha[g] * acc_sc[g] + _dot_nt(e[g].astype(BF16), v_t))
    for g in groups:
        m_sc[g] = m_new[g]
        l_sc[g] = l_new[g]
        acc_sc[g] = acc_new[g]

    @pl.when(j == pl.num_programs(1) - 1)
    def _():
        pos4 = _tile_rows(past + _iota((tp, 1), 0))
        gates = jax.nn.sigmoid(sm_ref[...])
        zb = zb_ref[...]
        new_k = nk_ref[...]
        new_v = nv_ref[...]
        for g in range(NSA_KV_HEADS):
            gs = slice(g * hd, (g + 1) * hd)
            qs = _stack_heads(q, g)
            picked = _tile_rows(sel_ref[g][:, n_blocks - 1:n_blocks] > 0.5)
            m_run, l_run, acc = m_sc[g], l_sc[g], acc_sc[g]
            s_new, new_mask = [], []
            for r in range(t_valid):
                mk = picked & (past + r <= pos4)
                new_mask.append(mk)
                s_new.append(jnp.where(mk, _new_key_scores(qs, new_k, r, gs), NEG_INF))
            m_all = m_run
            for sn in s_new:
                m_all = jnp.maximum(m_all, sn)
            alpha = jnp.exp2(m_run - m_all)
            l_all = alpha * l_run
            acc = alpha * acc
            for r in range(t_valid):
                e_r = jnp.where(new_mask[r], jnp.exp2(s_new[r] - m_all), 0.0)
                l_all = l_all + e_r
                acc = acc + e_r * new_v[r:r + 1, gs]
            o_s = acc * _inv_denominator(l_all)
            _combine_heads(o_ref, g, gates, zb, oc_ref[g], o_s, ow_ref[g], tp)


def _nsa_sample_b(p3, sel, o_c, o_w, pool_k, pool_v, page_table, *, k_pages, past, t_valid, n_blocks, nbl):
    b, tp, _ = p3.shape
    n_pages = page_table.shape[1]
    page = pool_k.shape[3]
    n_steps = n_pages // k_pages
    n_keys = k_pages * page
    per_step = n_keys // SEL_BLOCK
    assert n_pages % k_pages == 0 and page % SEL_BLOCK == 0 and past == n_pages * page
    assert n_blocks == past // SEL_BLOCK + 1 and t_valid <= SEL_BLOCK and per_step <= LANES
    kvc = COL_KV // LANES
    rows = NSA_GROUP * tp
    sel_steps = sel[..., :n_steps * per_step].reshape(b, NSA_KV_HEADS, tp, n_steps, per_step)
    sel_steps = jnp.pad(sel_steps.transpose(0, 3, 1, 2, 4), ((0, 0),) * 4 + ((0, LANES - per_step),))
    expand = (jnp.arange(LANES)[:, None] == (jnp.arange(n_keys) // SEL_BLOCK)[None, :]).astype(BF16)
    blk = (None, NSA_KV_HEADS, NSA_HEAD_DIM, page)

    def page_spec(i):
        return pl.BlockSpec(blk, lambda bi, j, pt: (pt[bi, j * k_pages + i], 0, 0, 0))

    per_b4 = lambda bi, j, pt: (bi, 0, 0, 0)
    in_specs = [
        pl.BlockSpec((None, tp, NSA_DIM), lambda bi, j, pt: (bi, 0, COL_QB // NSA_DIM)),
        pl.BlockSpec((None, tp, NSA_DIM), lambda bi, j, pt: (bi, 0, COL_ZB // NSA_DIM)),
        pl.BlockSpec((None, tp, LANES), lambda bi, j, pt: (bi, 0, COL_SMALL // LANES)),
        pl.BlockSpec((None, tp, LANES), lambda bi, j, pt: (bi, 0, kvc + 2)),
        pl.BlockSpec((None, tp, LANES), lambda bi, j, pt: (bi, 0, kvc + 3)),
        pl.BlockSpec((None, NSA_KV_HEADS, tp, nbl), per_b4),
        pl.BlockSpec((None, None, NSA_KV_HEADS, tp, LANES), lambda bi, j, pt: (bi, j, 0, 0, 0)),
        pl.BlockSpec((LANES, n_keys), lambda bi, j, pt: (0, 0)),
        pl.BlockSpec((None, NSA_KV_HEADS, rows, NSA_HEAD_DIM), per_b4),
        pl.BlockSpec((None, NSA_KV_HEADS, rows, NSA_HEAD_DIM), per_b4),
    ] + [page_spec(i) for i in range(k_pages)] * 2
    kern = functools.partial(_nsa_sample_b_kernel, k_pages=k_pages, past=past, t_valid=t_valid, tp=tp,
                             n_blocks=n_blocks, nbl=nbl)
    return pl.pallas_call(
        kern,
        grid_spec=pltpu.PrefetchScalarGridSpec(
            num_scalar_prefetch=1, grid=(b, n_steps), in_specs=in_specs,
            out_specs=pl.BlockSpec((None, tp, NSA_DIM), lambda bi, j, pt: (bi, 0, 0)),
            scratch_shapes=[pltpu.VMEM((NSA_KV_HEADS, rows, 1), F32), pltpu.VMEM((NSA_KV_HEADS, rows, 1), F32),
                            pltpu.VMEM((NSA_KV_HEADS, rows, NSA_HEAD_DIM), F32)]),
        out_shape=jax.ShapeDtypeStruct((b, tp, NSA_DIM), F32),
        compiler_params=pltpu.CompilerParams(
            dimension_semantics=("parallel", "arbitrary"), vmem_limit_bytes=VMEM_LIMIT),
    )(page_table, p3, p3, p3, p3, p3, sel, sel_steps, expand, o_c, o_w,
      *([pool_k] * k_pages), *([pool_v] * k_pages))


def _merge_kernel(x_ref, oa_ref, ob_ref, ga_ref, gb_ref, wa_ref, wb_ref, wo_ref, fg_ref, y_ref):
    mixed = (jax.nn.sigmoid(ga_ref[...]) * _dot(oa_ref[...].astype(BF16), wa_ref[...])
             + jax.nn.sigmoid(gb_ref[...]) * _dot(ob_ref[...].astype(BF16), wb_ref[...]))
    x_new = x_ref[...] + _dot(mixed.astype(BF16), wo_ref[...])
    y = x_new * lax.rsqrt(jnp.mean(x_new * x_new, axis=-1, keepdims=True) + RMS_EPS)
    y_ref[...] = y * fg_ref[...]


def _merge(x2d, o_a, o_b, p2d, w_a, w_b, w_o, final_gain, tm):
    rows, d = x2d.shape
    const2 = lambda i: (0, 0)
    return pl.pallas_call(
        _merge_kernel,
        grid=(rows // tm,),
        in_specs=[
            pl.BlockSpec((tm, d), lambda i: (i, 0)),
            pl.BlockSpec((tm, GDN_DIM), lambda i: (i, 0)),
            pl.BlockSpec((tm, NSA_DIM), lambda i: (i, 0)),
            pl.BlockSpec((tm, d), lambda i: (i, COL_GATE_A // D_MODEL)),
            pl.BlockSpec((tm, d), lambda i: (i, COL_GATE_B // D_MODEL)),
            pl.BlockSpec((GDN_DIM, d), const2),
            pl.BlockSpec((NSA_DIM, d), const2),
            pl.BlockSpec((d, d), const2),
            pl.BlockSpec((1, d), const2),
        ],
        out_specs=pl.BlockSpec((tm, d), lambda i: (i, 0)),
        out_shape=jax.ShapeDtypeStruct((rows, d), F32),
        compiler_params=pltpu.CompilerParams(
            dimension_semantics=("parallel",), vmem_limit_bytes=VMEM_LIMIT),
    )(x2d, o_a, o_b, p2d, p2d, w_a, w_b, w_o, final_gain)


def _pack_w_in(w_in):
    bounds = [0]
    for width in (GDN_QKV_DIM, GDN_HEADS, GDN_HEADS, GDN_DIM, NSA_DIM, 6 * NSA_KV_DIM, 3 * NSA_HEADS,
                  NSA_DIM, 2 * D_MODEL):
        bounds.append(bounds[-1] + width)
    qkv, a, bt, z_a, q_b, kv_b, g_b, z_b, merge = (w_in[:, bounds[i]:bounds[i + 1]] for i in range(9))
    small = jnp.concatenate([a, bt, g_b], axis=1)
    small = jnp.pad(small, ((0, 0), (0, N_PACKED - COL_SMALL - small.shape[1])))
    packed = jnp.concatenate([qkv, z_a, merge, q_b, z_b, kv_b, small], axis=1)
    assert packed.shape[1] == N_PACKED
    return packed.astype(BF16)


def _lane_vec(v):
    return jnp.pad(v.astype(F32), (0, LANES - v.shape[0])).reshape(1, LANES)


def _block_diag2(w):
    z = jnp.zeros_like(w)
    return jnp.concatenate([jnp.concatenate([w, z], axis=1), jnp.concatenate([z, w], axis=1)], axis=0)


def _tile_for(n, candidates):
    for c in candidates:
        if n % c == 0:
            return c
    raise ValueError(f"no tile for {n}")


def kernel(x_prompt, x_sample, cache_k_cmp, cache_v_cmp, cache_k_slc, cache_v_slc, state_win_k, state_win_v,
           state_conv, state_gdn, page_table, norm_gain, w_in, conv_w, a_log, dt_bias, gdn_gain, cmp_pos_k,
           cmp_w_k, cmp_pos_v, cmp_w_v, w_branch_a, w_branch_b, w_out, final_gain):
    assert w_in.shape[0] == 1, "one layer"
    bp, tp_len, d = x_prompt.shape
    bs, ts, _ = x_sample.shape
    n_pool, page = cache_k_cmp.shape[1:3]
    n_pages = page_table.shape[1]
    past = n_pages * page
    assert past % CMP_STRIDE == 0 and ts < CMP_STRIDE and ts >= CONV_WIDTH - 1 and ts <= SUBLANES
    t_pad = SUBLANES

    w_packed = _pack_w_in(w_in[0])
    gain_in = norm_gain[0].reshape(1, d)
    fgain = final_gain.reshape(1, d)
    w_a = w_branch_a[0].astype(BF16)
    w_b = w_branch_b[0].astype(BF16)
    w_o = w_out[0].astype(BF16)
    cw = conv_w[0]
    alog = _lane_vec(a_log[0])
    dtb = _lane_vec(dt_bias[0])
    ggain = gdn_gain[0].reshape(1, GDN_HEAD_DIM)
    pos_k = jnp.tile(cmp_pos_k[0], (1, NSA_KV_HEADS))
    pos_v = jnp.tile(cmp_pos_v[0], (1, NSA_KV_HEADS))
    wk2 = _block_diag2(cmp_w_k[0])
    wv2 = _block_diag2(cmp_w_v[0])
    tn = _tile_for(N_PACKED, (2048, 1024, 512, 256))
    kvc = COL_KV

    rows_p = bp * tp_len
    x2p = x_prompt.reshape(rows_p, d)
    p2 = _proj(x2p, gain_in, w_packed, _tile_for(rows_p, (1024, 512, 256, 128)), tn)
    p3 = p2.reshape(bp, tp_len, N_PACKED)
    tb = _tile_for(tp_len, (256, 128, 64))
    o_a, gdn_p = _gdn(p3, cw, jnp.zeros((bp, SUBLANES, GDN_QKV_DIM), F32), alog, dtb, ggain,
                      jnp.zeros((bp, GDN_HEADS, GDN_HEAD_DIM, GDN_HEAD_DIM), F32),
                      tb=tb, chunk=GDN_CHUNK, t_valid=tp_len)
    o_b = _nsa_prompt(p3, pos_k, pos_v, wk2, wv2, tq=256, tk=_tile_for(tp_len, (512, 256, 128)))
    y_p = _merge(x2p, o_a.reshape(rows_p, GDN_DIM), o_b.reshape(rows_p, NSA_DIM), p2, w_a, w_b, w_o, fgain,
                 _tile_for(rows_p, (512, 256, 128)))
    y_prompt = y_p.reshape(bp, tp_len, d)

    def kv_out(p, i, t_keep_from, t_to):
        return p[:, t_keep_from:t_to, kvc + i * LANES:kvc + (i + 1) * LANES].reshape(
            1, p.shape[0], t_to - t_keep_from, NSA_KV_HEADS, NSA_HEAD_DIM)

    keep_p = min(WINDOW, tp_len)
    prompt_state = ([kv_out(p3, i, 0, tp_len) for i in range(4)]
                    + [kv_out(p3, i, tp_len - keep_p, tp_len) for i in (4, 5)]
                    + [p3[:, tp_len - (CONV_WIDTH - 1):, :GDN_QKV_DIM][None], gdn_p[None]])

    xs_pad = jnp.pad(x_sample, ((0, 0), (0, t_pad - ts), (0, 0)))
    rows_s = bs * t_pad
    x2s = xs_pad.reshape(rows_s, d)
    ps2 = _proj(x2s, gain_in, w_packed, _tile_for(rows_s, (512, 256, 128, 8)), tn)
    ps3 = ps2.reshape(bs, t_pad, N_PACKED)
    prefix_s = jnp.pad(state_conv[0], ((0, 0), (SUBLANES - (CONV_WIDTH - 1), 0), (0, 0)))
    o_a_s, gdn_s = _gdn(ps3, cw, prefix_s, alog, dtb, ggain, state_gdn[0], tb=t_pad, chunk=t_pad, t_valid=ts)

    pools = [c[0].transpose(0, 2, 3, 1) for c in (cache_k_cmp, cache_v_cmp, cache_k_slc, cache_v_slc)]
    kc_s, vc_s = _cmp_paged(pools[0], pools[1], page_table, pos_k, pos_v, wk2, wv2,
                            k_pages=_tile_for(n_pages, (64, 32, 16, 8, 4, 2, 1)))
    n_blocks = -(-(past + ts) // SEL_BLOCK)
    nbl = -(-n_blocks // LANES) * LANES
    w_buf = state_win_k.shape[2]
    o_c_s, o_w_s, sel_s, win_k_s, win_v_s = _nsa_sample_a(
        ps3, kc_s, vc_s, state_win_k[0].reshape(bs, w_buf, NSA_KV_DIM), state_win_v[0].reshape(bs, w_buf, NSA_KV_DIM),
        past=past, t_valid=ts, n_blocks=n_blocks, nbl=nbl)
    o_b_s = _nsa_sample_b(ps3, sel_s, o_c_s, o_w_s, pools[2], pools[3], page_table,
                          k_pages=_tile_for(n_pages, (64, 32, 16, 8, 4, 2, 1)),
                          past=past, t_valid=ts, n_blocks=n_blocks, nbl=nbl)
    y_s = _merge(x2s, o_a_s.reshape(rows_s, GDN_DIM), o_b_s.reshape(rows_s, NSA_DIM), ps2, w_a, w_b, w_o, fgain,
                 _tile_for(rows_s, (512, 256, 128, 8)))
    y_sample = y_s.reshape(bs, t_pad, d)[:, :ts]

    keep_s = min(WINDOW, w_buf + ts)
    assert keep_s == w_buf
    sample_state = ([kv_out(ps3, i, 0, ts) for i in range(4)]
                    + [win_k_s.reshape(1, bs, keep_s, NSA_KV_HEADS, NSA_HEAD_DIM),
                       win_v_s.reshape(1, bs, keep_s, NSA_KV_HEADS, NSA_HEAD_DIM),
                       ps3[:, ts - (CONV_WIDTH - 1):ts, :GDN_QKV_DIM][None], gdn_s[None]])
    return (y_prompt, y_sample, *prompt_state, *sample_state)
```

```python
import functools
import math

import jax
import jax.numpy as jnp
from jax import lax
from jax.experimental import pallas as pl
from jax.experimental.pallas import tpu as pltpu

F32 = jnp.float32
BF16 = jnp.bfloat16
HI = lax.Precision.HIGHEST

LANES = 128
SUBLANES = 8
VMEM_LIMIT = 56 * 1024 * 1024

D_MODEL = 1024
GDN_HEADS = 8
GDN_HEAD_DIM = 128
GDN_DIM = GDN_HEADS * GDN_HEAD_DIM
GDN_QKV_DIM = 3 * GDN_DIM
CONV_WIDTH = 4
GDN_CHUNK = 64
NSA_HEADS = 8
NSA_KV_HEADS = 2
NSA_GROUP = NSA_HEADS // NSA_KV_HEADS
NSA_HEAD_DIM = 64
NSA_DIM = NSA_HEADS * NSA_HEAD_DIM
NSA_KV_DIM = NSA_KV_HEADS * NSA_HEAD_DIM
CMP_STRIDE = 16
CMP_LEN = 2 * CMP_STRIDE
SEL_BLOCK = 64
SEL_SHIFT = 6
SEL_RATIO = SEL_BLOCK // CMP_STRIDE
N_SELECT = 16
WINDOW = 512
FORCE_BONUS = 100.0
RMS_EPS = 1e-6
NEG_INF = -1e30
NSA_SCALE = NSA_HEAD_DIM ** -0.5
LOG2E = math.log2(math.e)

COL_QKV = 0
COL_ZA = 3072
COL_GATE_A = 4096
COL_GATE_B = 5120
COL_QB = 6144
COL_ZB = 6656
COL_KV = 7168
COL_SMALL = 7936
N_PACKED = 8192
SMALL_A = 0
SMALL_BT = GDN_HEADS
SMALL_GB = 2 * GDN_HEADS


def _dot(a, b, precision=None):
    return jnp.dot(a, b, precision=precision, preferred_element_type=F32)


def _dot_nt(a, b, precision=None):
    return lax.dot_general(a, b, (((1,), (1,)), ((), ())), precision=precision,
                           preferred_element_type=F32)


def _dot_tn(a, b, precision=None):
    return lax.dot_general(a, b, (((0,), (0,)), ((), ())), precision=precision,
                           preferred_element_type=F32)


def _mm(a, b):
    return jnp.dot(a.astype(BF16), b.astype(BF16), preferred_element_type=F32)


def _silu(x):
    return x * jax.nn.sigmoid(x)


def _iota(shape, dim):
    return lax.broadcasted_iota(jnp.int32, shape, dim)


def _proj_kernel(x_ref, g_ref, w_ref, o_ref, h_ref):
    @pl.when(pl.program_id(1) == 0)
    def _():
        x = x_ref[...]
        y = x * lax.rsqrt(jnp.mean(x * x, axis=-1, keepdims=True) + RMS_EPS)
        h_ref[...] = (y * g_ref[...]).astype(BF16)

    o_ref[...] = jnp.dot(h_ref[...], w_ref[...], preferred_element_type=F32)


def _proj(x2d, gain, w_packed, tm, tn):
    rows, d = x2d.shape
    n = w_packed.shape[1]
    return pl.pallas_call(
        _proj_kernel,
        grid=(rows // tm, n // tn),
        in_specs=[
            pl.BlockSpec((tm, d), lambda i, j: (i, 0)),
            pl.BlockSpec((1, d), lambda i, j: (0, 0)),
            pl.BlockSpec((d, tn), lambda i, j: (0, j)),
        ],
        out_specs=pl.BlockSpec((tm, tn), lambda i, j: (i, j)),
        out_shape=jax.ShapeDtypeStruct((rows, n), F32),
        scratch_shapes=[pltpu.VMEM((tm, d), BF16)],
        compiler_params=pltpu.CompilerParams(
            dimension_semantics=("parallel", "arbitrary"), vmem_limit_bytes=VMEM_LIMIT),
    )(x2d, gain, w_packed)


def _gdn_kernel(q_ref, k_ref, v_ref, z_ref, sm_ref, cw_ref, pre_ref, alog_ref, dtb_ref, gain_ref, s0_ref,
                o_ref, sfin_ref,
                s_sc, tail_sc, q_sc, k_sc, v_sc, gc_sc, b_sc, gct_sc, *, tb, chunk, t_valid):
    t = pl.program_id(1)
    c = chunk
    hd = GDN_HEAD_DIM
    nh = GDN_HEADS

    @pl.when(t == 0)
    def _():
        s_sc[...] = s0_ref[...]
        tail_sc[...] = pre_ref[...]

    for idx, (x_ref, dst) in enumerate(((q_ref, q_sc), (k_ref, k_sc), (v_ref, v_sc))):
        cols = slice(idx * GDN_DIM, (idx + 1) * GDN_DIM)
        x = x_ref[...]
        cw = cw_ref[:, cols]
        last = CONV_WIDTH - 1
        head = jnp.concatenate([tail_sc[:, cols], x[0:SUBLANES]], axis=0)
        tail_sc[:, cols] = x[tb - SUBLANES:tb]
        y = cw[last:last + 1] * head[SUBLANES:2 * SUBLANES]
        for j in range(1, CONV_WIDTH):
            y = y + cw[last - j:last - j + 1] * head[SUBLANES - j:2 * SUBLANES - j]
        if tb > SUBLANES:
            rest = cw[last:last + 1] * x
            for j in range(1, CONV_WIDTH):
                rest = rest + cw[last - j:last - j + 1] * pltpu.roll(x, j, 0)
            y = jnp.concatenate([y, rest[SUBLANES:]], axis=0)
        y = _silu(y)
        if idx == 2:
            dst[...] = y
        else:
            scale = GDN_HEAD_DIM ** -0.5 if idx == 0 else 1.0
            for h in range(nh):
                yh = y[:, h * hd:(h + 1) * hd]
                dst[:, h * hd:(h + 1) * hd] = yh * (lax.rsqrt(jnp.sum(yh * yh, axis=-1, keepdims=True) + 1e-6) * scale)

    sm = sm_ref[...]
    xx = sm + dtb_ref[...]
    softplus = jnp.maximum(xx, 0.0) + jnp.log(1.0 + jnp.exp(-jnp.abs(xx)))
    row = _iota((tb, 1), 0)
    live = t * tb + row < t_valid
    g_small = jnp.where(live, -jnp.exp(alog_ref[...]) * softplus, 0.0)
    beta_small = jnp.where(live, jax.nn.sigmoid(sm), 0.0)
    row_in_chunk = row & (c - 1)
    gc_small = g_small
    shift = 1
    while shift < c:
        gc_small = gc_small + jnp.where(row_in_chunk >= shift, pltpu.roll(gc_small, shift, 0), 0.0)
        shift *= 2
    li = _iota((LANES, GDN_DIM), 0)
    head_of_lane = _iota((LANES, GDN_DIM), 1) >> int(math.log2(hd))

    def spread(x, first_lane):
        onehot = (li == head_of_lane + first_lane).astype(BF16)
        hi = x.astype(BF16)
        r1 = x - hi.astype(F32)
        mid = r1.astype(BF16)
        lo = (r1 - mid.astype(F32)).astype(BF16)
        return _dot(hi, onehot) + _dot(mid, onehot) + _dot(lo, onehot)

    gc_sc[...] = spread(gc_small, SMALL_A)
    b_sc[...] = spread(beta_small, SMALL_BT)
    for r0 in range(0, tb, LANES):
        n = min(LANES, tb - r0)
        part = gc_small[r0:r0 + n]
        if n < LANES:
            part = jnp.concatenate([part, jnp.zeros((LANES - n, LANES), F32)], axis=0)
        part_t = part.T
        for off in range(0, n, c):
            gct_sc[(r0 + off) // c] = part_t[:, off:off + c]

    ri = _iota((c, c), 0)
    ci = _iota((c, c), 1)
    causal = ri >= ci
    strict = ri > ci
    eye_f = (ri == ci).astype(F32)
    n_doubling = int(math.log2(c)) - 1
    gain = gain_ref[...]

    heads = range(nh)
    hs = [slice(h * hd, (h + 1) * hd) for h in heads]
    n_chunks = tb // c
    rw = [slice(ck * c, (ck + 1) * c) for ck in range(n_chunks)]
    pairs = [(ck, h) for ck in range(n_chunks) for h in heads]

    kq = {}
    for ck, h in pairs:
        kk = k_sc[rw[ck], hs[h]].astype(BF16)
        kq[ck, h] = _dot_nt(jnp.concatenate([kk, q_sc[rw[ck], hs[h]].astype(BF16)], axis=0), kk)
    decay = {(ck, h): jnp.where(causal, jnp.exp(gc_sc[rw[ck], hs[h]][:, :c] - gct_sc[ck, h:h + 1, :]), 0.0)
             for ck, h in pairs}
    attn = {p: kq[p][c:] * decay[p] for p in pairs}
    nk = {(ck, h): jnp.where(strict, -(b_sc[rw[ck], hs[h]][:, :c] * kq[ck, h][:c] * decay[ck, h]), 0.0)
          for ck, h in pairs}
    tinv = {p: eye_f + nk[p] for p in pairs}
    for _ in range(n_doubling):
        nk = {p: _mm(nk[p], nk[p]) for p in pairs}
        tinv = {p: tinv[p] + _mm(tinv[p], nk[p]) for p in pairs}
    uw = {}
    for ck, h in pairs:
        bb = b_sc[rw[ck], hs[h]]
        rhs = jnp.concatenate([v_sc[rw[ck], hs[h]] * bb,
                               k_sc[rw[ck], hs[h]] * (bb * jnp.exp(gc_sc[rw[ck], hs[h]]))], axis=1)
        uw[ck, h] = _mm(tinv[ck, h], rhs)

    state = [s_sc[h] for h in heads]
    for ck in range(n_chunks):
        rows = rw[ck]
        ws_qs = []
        for h in heads:
            q_dec = q_sc[rows, hs[h]] * jnp.exp(gc_sc[rows, hs[h]])
            ws_qs.append(_mm(jnp.concatenate([uw[ck, h][:, hd:], q_dec], axis=0), state[h]))
        v_new = [uw[ck, h][:, :hd] - ws_qs[h][:c] for h in heads]
        o = [ws_qs[h][c:] + _mm(attn[ck, h], v_new[h]) for h in heads]
        new_state = []
        for h in heads:
            gc = gc_sc[rows, hs[h]]
            g_last = gc[c - 1:c, :]
            k_dec = k_sc[rows, hs[h]] * jnp.exp(g_last - gc)
            new_state.append(state[h] * jnp.exp(g_last) + _dot_tn(k_dec, v_new[h]))
        state = new_state
        for h in heads:
            on = o[h] * lax.rsqrt(jnp.mean(o[h] * o[h], axis=-1, keepdims=True) + RMS_EPS) * gain
            o_ref[rows, hs[h]] = on * _silu(z_ref[rows, hs[h]])
    for h in heads:
        s_sc[h] = state[h]

    @pl.when(t == pl.num_programs(1) - 1)
    def _():
        sfin_ref[...] = s_sc[...]


def _gdn(p3, conv_w, prefix, alog, dtb, gain, s0, *, tb, chunk, t_valid):
    b, t, _ = p3.shape
    hd = GDN_HEAD_DIM
    gd = GDN_DIM
    const2 = lambda bi, ti: (0, 0)
    kern = functools.partial(_gdn_kernel, tb=tb, chunk=chunk, t_valid=t_valid)
    return pl.pallas_call(
        kern,
        grid=(b, t // tb),
        in_specs=[
            pl.BlockSpec((None, tb, gd), lambda bi, ti: (bi, ti, COL_QKV // gd)),
            pl.BlockSpec((None, tb, gd), lambda bi, ti: (bi, ti, COL_QKV // gd + 1)),
            pl.BlockSpec((None, tb, gd), lambda bi, ti: (bi, ti, COL_QKV // gd + 2)),
            pl.BlockSpec((None, tb, gd), lambda bi, ti: (bi, ti, COL_ZA // gd)),
            pl.BlockSpec((None, tb, LANES), lambda bi, ti: (bi, ti, COL_SMALL // LANES)),
            pl.BlockSpec((CONV_WIDTH, GDN_QKV_DIM), const2),
            pl.BlockSpec((None, SUBLANES, GDN_QKV_DIM), lambda bi, ti: (bi, 0, 0)),
            pl.BlockSpec((1, LANES), const2),
            pl.BlockSpec((1, LANES), const2),
            pl.BlockSpec((1, hd), const2),
            pl.BlockSpec((None, GDN_HEADS, hd, hd), lambda bi, ti: (bi, 0, 0, 0)),
        ],
        out_specs=[
            pl.BlockSpec((None, tb, gd), lambda bi, ti: (bi, ti, 0)),
            pl.BlockSpec((None, GDN_HEADS, hd, hd), lambda bi, ti: (bi, 0, 0, 0)),
        ],
        out_shape=[
            jax.ShapeDtypeStruct((b, t, gd), F32),
            jax.ShapeDtypeStruct((b, GDN_HEADS, hd, hd), F32),
        ],
        scratch_shapes=[
            pltpu.VMEM((GDN_HEADS, hd, hd), F32),
            pltpu.VMEM((SUBLANES, GDN_QKV_DIM), F32),
            pltpu.VMEM((tb, gd), F32),
            pltpu.VMEM((tb, gd), F32),
            pltpu.VMEM((tb, gd), F32),
            pltpu.VMEM((tb, gd), F32),
            pltpu.VMEM((tb, gd), F32),
            pltpu.VMEM((tb // chunk, LANES, chunk), F32),
        ],
        compiler_params=pltpu.CompilerParams(
            dimension_semantics=("parallel", "arbitrary"), vmem_limit_bytes=VMEM_LIMIT),
    )(p3, p3, p3, p3, p3, conv_w, prefix, alog, dtb, gain, s0)


def _inv_denominator(total):
    return 1.0 / jnp.maximum(total, 1e-30)


def _masked_softmax2(s, mask):
    s = jnp.where(mask, s, NEG_INF)
    e = jnp.where(mask, jnp.exp2(s - jnp.max(s, axis=-1, keepdims=True)), 0.0)
    return e * _inv_denominator(jnp.sum(e, axis=-1, keepdims=True))


def _stack_heads(q, g):
    hd = NSA_HEAD_DIM
    parts = [q[:, (NSA_GROUP * g + h) * hd:(NSA_GROUP * g + h + 1) * hd] for h in range(NSA_GROUP)]
    return jnp.concatenate(parts, axis=0) * (NSA_SCALE * LOG2E)


def _tile_rows(x):
    return jnp.concatenate([x] * NSA_GROUP, axis=0)


def _group_importance(p_c, tq):
    imp = p_c[0:tq]
    for h in range(1, NSA_GROUP):
        imp = imp + p_c[h * tq:(h + 1) * tq]
    return imp


def _select_blocks(p_c, pos, tq, n_blocks, nbl):
    m = p_c.shape[1]
    imp = _group_importance(p_c, tq)
    pool = ((_iota((m, nbl), 0) >> int(math.log2(SEL_RATIO))) == _iota((m, nbl), 1)).astype(F32)
    imp_sel = _dot(imp, pool, HI)
    lane = _iota((tq, nbl), 1)
    valid = lane * SEL_BLOCK <= pos
    cur = pos >> SEL_SHIFT
    forced = (lane == 0) | (lane == cur) | (lane == cur - 1)
    score = jnp.where(valid, imp_sel + FORCE_BONUS * forced.astype(F32), -1.0)
    score = jnp.where(lane < n_blocks, score, -2.0)
    rank = jnp.zeros((tq, nbl), F32)
    for j in range(n_blocks):
        sj = score[:, j:j + 1]
        rank = rank + ((sj > score) | ((sj == score) & (lane > j))).astype(F32)
    return (rank < float(min(N_SELECT, n_blocks))).astype(F32)


def _select_blocks_t(p_c, q0, tq, n_blocks, nbl):
    m = p_c.shape[1]
    n_sel = min(N_SELECT, n_blocks)
    nbs = -(-n_blocks // SUBLANES) * SUBLANES
    imp = _group_importance(p_c, tq)
    pool_t = ((_iota((nbs, m), 1) >> int(math.log2(SEL_RATIO))) == _iota((nbs, m), 0)).astype(F32)
    imp_t = _dot_nt(pool_t, imp, HI)
    pos_r = q0 + _iota((1, tq), 1)
    blk = _iota((nbs, tq), 0)
    valid = blk * SEL_BLOCK <= pos_r

    def ranked():
        cur = pos_r >> SEL_SHIFT
        forced = (blk == 0) | (blk == cur) | (blk == cur - 1)
        score = jnp.where(valid, imp_t + FORCE_BONUS * forced.astype(F32), -1.0)
        score = jnp.where(blk < n_blocks, score, -2.0)
        rank = jnp.zeros((nbs, tq), F32)
        for j in range(n_blocks):
            sj = score[j:j + 1, :]
            rank = rank + ((sj > score) | ((sj == score) & (blk > j))).astype(F32)
        return (rank < float(n_sel)).astype(F32)

    sel_t = lax.cond(q0 + tq <= n_sel * SEL_BLOCK, lambda: valid.astype(F32), ranked)
    sel_t = jnp.concatenate([sel_t, jnp.zeros((nbl - nbs, tq), F32)], axis=0)
    return sel_t.T


def _compress_rows(x, pw):
    n = x.shape[0] // CMP_STRIDE
    x3 = x.reshape(n, CMP_STRIDE, LANES)
    lo = jnp.sum(x3 * pw[:CMP_STRIDE][None], axis=1)
    hi = jnp.sum(x3 * pw[CMP_STRIDE:][None], axis=1)
    return lo, hi


def _combine_heads(o_ref, g, gates, zb, o_c, o_s, o_w, tq):
    hd = NSA_HEAD_DIM
    for h in range(NSA_GROUP):
        hh = NSA_GROUP * g + h
        c0 = SMALL_GB + 3 * hh
        rows = slice(h * tq, (h + 1) * tq)
        o_h = (gates[:, c0:c0 + 1] * o_c[rows] + gates[:, c0 + 1:c0 + 2] * o_s[rows]
               + gates[:, c0 + 2:c0 + 3] * o_w[rows])
        o_ref[:, hh * hd:(hh + 1) * hd] = o_h * _silu(zb[:, hh * hd:(hh + 1) * hd])


def _nsa_prompt_kernel(q_ref, zb_ref, sm_ref, kcmp_ref, vcmp_ref, kslc_ref, vslc_ref, kwin_ref, vwin_ref,
                       pk_ref, pv_ref, wk_ref, wv_ref, o_ref, kc_sc, vc_sc, kcx_sc, kx_sc, kw_sc, vx_sc, *,
                       t_len, tq, tk, span, nbl, nbo):
    i = pl.program_id(1)
    m = t_len // CMP_STRIDE
    n_blocks = t_len // SEL_BLOCK
    hd = NSA_HEAD_DIM

    @pl.when(i == 0)
    def _():
        for src, pw_ref, w_ref, dst in ((kcmp_ref, pk_ref, wk_ref, kc_sc), (vcmp_ref, pv_ref, wv_ref, vc_sc)):
            lo, hi = _compress_rows(src[...], pw_ref[...])
            hi_next = pltpu.roll(hi, m - 1, 0)
            pre = jnp.where(_iota((m, LANES), 0) < m - 1, lo + hi_next, 0.0)
            dst[...] = _dot(pre, w_ref[...], HI)
        kc = kc_sc[...]
        kc_hi = kc.astype(BF16)
        kc_lo = (kc - kc_hi.astype(F32)).astype(BF16)
        for g in range(NSA_KV_HEADS):
            gs = slice(g * hd, (g + 1) * hd)
            kcx_sc[g] = jnp.concatenate([kc_hi[:, gs], kc_hi[:, gs], kc_lo[:, gs], jnp.zeros((m, hd), BF16)], axis=1)
        k_slc = kslc_ref[...].astype(BF16)
        blk_onehot = ((_iota((t_len, nbo), 0) >> SEL_SHIFT) == _iota((t_len, nbo), 1)).astype(BF16)
        for g in range(NSA_KV_HEADS):
            kx_sc[g] = jnp.concatenate([k_slc[:, g * hd:(g + 1) * hd], blk_onehot], axis=1)
        kw_sc[...] = kwin_ref[...].astype(BF16)
        ones = jnp.ones((t_len, hd), BF16)
        for idx, src in enumerate((vslc_ref, vwin_ref)):
            v = src[...].astype(BF16)
            vx_sc[idx] = jnp.concatenate(
                [piece for g in range(NSA_KV_HEADS) for piece in (v[:, g * hd:(g + 1) * hd], ones)], axis=1)

    q0 = i * tq
    pos = q0 + _iota((tq, 1), 0)
    pos4 = _tile_rows(pos)
    q = q_ref[...]
    zb = zb_ref[...]
    gates = jax.nn.sigmoid(sm_ref[...])
    c_end = _iota((1, m), 1) * CMP_STRIDE + (CMP_LEN - 1)
    n_full = q0 // tk
    w_start = pl.multiple_of(jnp.clip(q0 - WINDOW, 0, t_len - span), tq)
    w_dist = pos - (w_start + _iota((1, span), 1))
    w_bias = _tile_rows(jnp.where((w_dist >= 0) & (w_dist <= WINDOW), 0.0, NEG_INF))

    groups = range(NSA_KV_HEADS)
    rows4 = NSA_GROUP * tq
    gsl = [slice(g * hd, (g + 1) * hd) for g in groups]
    qs = [_stack_heads(q, g) for g in groups]
    qs_bf = [qs[g].astype(BF16) for g in groups]
    q_split = []
    for g in groups:
        qs_lo = (qs[g] - qs_bf[g].astype(F32)).astype(BF16)
        q_split.append(jnp.concatenate([qs_bf[g], qs_lo, qs_bf[g], jnp.zeros_like(qs_bf[g])], axis=1))
    s_c = [_dot_nt(q_split[g], kcx_sc[g]) for g in groups]
    p_c = [_masked_softmax2(s_c[g], c_end <= pos4) for g in groups]
    o_c = [_dot(p_c[g], vc_sc[:, gsl[g]]) for g in groups]
    sel = [_select_blocks_t(p_c[g], q0, tq, n_blocks, nbl) for g in groups]
    q_aug = []
    for g in groups:
        sel_bias = ((sel[g][:, :nbo] - 1.0) * -NEG_INF).astype(BF16)
        q_aug.append(jnp.concatenate([qs_bf[g], _tile_rows(sel_bias)], axis=1))

    def sel_tile(kt, carry, causal_tile):
        k0 = pl.multiple_of(kt * tk, tk)
        s = [_dot_nt(q_aug[g], kx_sc[g, pl.ds(k0, tk), :]) for g in groups]
        if causal_tile:
            causal_bias = _tile_rows(jnp.where(k0 + _iota((1, tk), 1) <= pos, 0.0, NEG_INF))
            s = [s[g] + causal_bias for g in groups]
        m_new = [jnp.maximum(carry[2 * g], jnp.max(s[g], axis=-1, keepdims=True)) for g in groups]
        e = [jnp.exp2(s[g] - m_new[g]).astype(BF16) for g in groups]
        pv = _dot(jnp.concatenate(e, axis=0), vx_sc[0, pl.ds(k0, tk), :])
        out = []
        for g in groups:
            own = pv[g * rows4:(g + 1) * rows4, g * 2 * hd:(g + 1) * 2 * hd]
            out += [m_new[g], jnp.exp2(carry[2 * g] - m_new[g]) * carry[2 * g + 1] + own]
        return tuple(out)

    init = (jnp.full((NSA_GROUP * tq, 1), NEG_INF, F32), jnp.zeros((NSA_GROUP * tq, 2 * hd), F32)) * NSA_KV_HEADS
    carry = lax.fori_loop(0, n_full, functools.partial(sel_tile, causal_tile=False), init)
    carry = sel_tile(n_full, carry, True)
    o_s = [carry[2 * g + 1][:, :hd] * _inv_denominator(carry[2 * g + 1][:, hd:hd + 1]) for g in groups]

    s_w = [_dot_nt(qs_bf[g], kw_sc[pl.ds(w_start, span), gsl[g]]) + w_bias for g in groups]
    e_w = [jnp.exp2(s_w[g] - jnp.max(s_w[g], axis=-1, keepdims=True)).astype(BF16) for g in groups]
    pv_w = _dot(jnp.concatenate(e_w, axis=0), vx_sc[1, pl.ds(w_start, span), :])
    acc_w = [pv_w[g * rows4:(g + 1) * rows4, g * 2 * hd:(g + 1) * 2 * hd] for g in groups]
    o_w = [acc_w[g][:, :hd] * _inv_denominator(acc_w[g][:, hd:hd + 1]) for g in groups]
    for g in groups:
        _combine_heads(o_ref, g, gates, zb, o_c[g], o_s[g], o_w[g], tq)


def _nsa_prompt(p3, pos_k, pos_v, w_k, w_v, *, tq, tk):
    b, t, _ = p3.shape
    span = WINDOW + tq
    assert t % tq == 0 and t % tk == 0 and t >= span and t % SEL_BLOCK == 0 and tq % LANES == 0
    assert WINDOW % tq == 0 and tk % tq == 0
    nbo = -(-(t // SEL_BLOCK) // NSA_HEAD_DIM) * NSA_HEAD_DIM
    assert (NSA_HEAD_DIM + nbo) % LANES == 0
    m = t // CMP_STRIDE
    nbl = -(-(t // SEL_BLOCK) // LANES) * LANES
    kvc = COL_KV // LANES

    def full(c):
        return pl.BlockSpec((None, t, LANES), lambda bi, i: (bi, 0, c))

    const2 = lambda bi, i: (0, 0)
    kern = functools.partial(_nsa_prompt_kernel, t_len=t, tq=tq, tk=tk, span=span, nbl=nbl, nbo=nbo)
    return pl.pallas_call(
        kern,
        grid=(b, t // tq),
        in_specs=[
            pl.BlockSpec((None, tq, NSA_DIM), lambda bi, i: (bi, i, COL_QB // NSA_DIM)),
            pl.BlockSpec((None, tq, NSA_DIM), lambda bi, i: (bi, i, COL_ZB // NSA_DIM)),
            pl.BlockSpec((None, tq, LANES), lambda bi, i: (bi, i, COL_SMALL // LANES)),
            full(kvc), full(kvc + 1), full(kvc + 2), full(kvc + 3), full(kvc + 4), full(kvc + 5),
            pl.BlockSpec((CMP_LEN, LANES), const2),
            pl.BlockSpec((CMP_LEN, LANES), const2),
            pl.BlockSpec((LANES, LANES), const2),
            pl.BlockSpec((LANES, LANES), const2),
        ],
        out_specs=pl.BlockSpec((None, tq, NSA_DIM), lambda bi, i: (bi, i, 0)),
        out_shape=jax.ShapeDtypeStruct((b, t, NSA_DIM), F32),
        scratch_shapes=[pltpu.VMEM((m, LANES), F32), pltpu.VMEM((m, LANES), F32),
                        pltpu.VMEM((NSA_KV_HEADS, m, 4 * NSA_HEAD_DIM), BF16),
                        pltpu.VMEM((NSA_KV_HEADS, t, NSA_HEAD_DIM + nbo), BF16),
                        pltpu.VMEM((t, LANES), BF16),
                        pltpu.VMEM((2, t, 2 * NSA_KV_HEADS * NSA_HEAD_DIM), BF16)],
        compiler_params=pltpu.CompilerParams(
            dimension_semantics=("parallel", "arbitrary"), vmem_limit_bytes=VMEM_LIMIT),
    )(p3, p3, p3, p3, p3, p3, p3, p3, p3, pos_k, pos_v, w_k, w_v)


def _cmp_paged_kernel(pt_ref, *refs, k_pages):
    del pt_ref
    kp = refs[:k_pages]
    kn = refs[k_pages]
    vp = refs[k_pages + 1:2 * k_pages + 1]
    vn = refs[2 * k_pages + 1]
    pk_ref, pv_ref, wk_ref, wv_ref, kc_ref, vc_ref = refs[2 * k_pages + 2:]
    n = kc_ref.shape[0]
    last_row = _iota((n, LANES), 0) == n - 1

    def row_major(r):
        return r[...].reshape(LANES, LANES).T

    for pages, nxt, pw_ref, w_ref, out in ((kp, kn, pk_ref, wk_ref, kc_ref), (vp, vn, pv_ref, wv_ref, vc_ref)):
        pw = pw_ref[...]
        parts = [_compress_rows(row_major(r), pw) for r in pages]
        lo = jnp.concatenate([p[0] for p in parts], axis=0)
        hi = jnp.concatenate([p[1] for p in parts], axis=0)
        hi_head = jnp.sum(row_major(nxt)[:CMP_STRIDE] * pw[CMP_STRIDE:], axis=0, keepdims=True)
        hi_next = jnp.where(last_row, hi_head, pltpu.roll(hi, n - 1, 0))
        out[...] = _dot(lo + hi_next, w_ref[...], HI)


def _cmp_paged(pool_k, pool_v, page_table, pos_k, pos_v, w_k, w_v, *, k_pages):
    b, n_pages = page_table.shape
    page = pool_k.shape[3]
    per_step = k_pages * page // CMP_STRIDE
    assert page == LANES and n_pages % k_pages == 0
    blk = (None, NSA_KV_HEADS, NSA_HEAD_DIM, page)

    def page_spec(i):
        return pl.BlockSpec(blk, lambda bi, j, pt: (pt[bi, j * k_pages + i], 0, 0, 0))

    next_spec = pl.BlockSpec(
        blk, lambda bi, j, pt: (pt[bi, jnp.minimum((j + 1) * k_pages, n_pages - 1)], 0, 0, 0))
    const2 = lambda bi, j, pt: (0, 0)
    in_specs = ([page_spec(i) for i in range(k_pages)] + [next_spec]
                + [page_spec(i) for i in range(k_pages)] + [next_spec]
                + [pl.BlockSpec((CMP_LEN, LANES), const2)] * 2 + [pl.BlockSpec((LANES, LANES), const2)] * 2)
    out_spec = pl.BlockSpec((None, per_step, LANES), lambda bi, j, pt: (bi, j, 0))
    m = n_pages * page // CMP_STRIDE
    return pl.pallas_call(
        functools.partial(_cmp_paged_kernel, k_pages=k_pages),
        grid_spec=pltpu.PrefetchScalarGridSpec(
            num_scalar_prefetch=1, grid=(b, n_pages // k_pages), in_specs=in_specs,
            out_specs=[out_spec, out_spec]),
        out_shape=[jax.ShapeDtypeStruct((b, m, LANES), F32)] * 2,
        compiler_params=pltpu.CompilerParams(
            dimension_semantics=("parallel", "arbitrary"), vmem_limit_bytes=VMEM_LIMIT),
    )(page_table, *([pool_k] * (k_pages + 1)), *([pool_v] * (k_pages + 1)), pos_k, pos_v, w_k, w_v)


def _new_key_scores(qs, new_k, r, gs):
    return jnp.sum(qs * new_k[r:r + 1, gs], axis=-1, keepdims=True)


def _nsa_sample_a_kernel(q_ref, kc_ref, vc_ref, bk_ref, bv_ref, nk_ref, nv_ref,
                         oc_ref, ow_ref, sel_ref, wk_out, wv_out, *, past, t_valid, tp, n_blocks, nbl):
    hd = NSA_HEAD_DIM
    m = kc_ref.shape[0]
    w_buf = bk_ref.shape[0]
    pos = past + _iota((tp, 1), 0)
    pos4 = _tile_rows(pos)
    q = q_ref[...]
    new_k = nk_ref[...]
    new_v = nv_ref[...]
    c_end = _iota((1, m), 1) * CMP_STRIDE + (CMP_LEN - 1)
    buf_pos = (past - w_buf) + _iota((1, w_buf), 1)
    buf_dist = pos4 - buf_pos
    buf_mask = (buf_dist >= 0) & (buf_dist <= WINDOW)
    for g in range(NSA_KV_HEADS):
        gs = slice(g * hd, (g + 1) * hd)
        qs = _stack_heads(q, g)
        p_c = _masked_softmax2(_dot_nt(qs, kc_ref[:, gs], HI), c_end <= pos4)
        oc_ref[g] = _dot(p_c, vc_ref[:, gs])
        sel_ref[g] = _select_blocks(p_c, pos, tp, n_blocks, nbl)
        s_buf = jnp.where(buf_mask, _dot_nt(qs, bk_ref[:, gs]), NEG_INF)
        s_new, new_mask = [], []
        for r in range(t_valid):
            dist = pos4 - (past + r)
            mk = (dist >= 0) & (dist <= WINDOW)
            new_mask.append(mk)
            s_new.append(jnp.where(mk, _new_key_scores(qs, new_k, r, gs), NEG_INF))
        m_all = jnp.max(s_buf, axis=-1, keepdims=True)
        for sn in s_new:
            m_all = jnp.maximum(m_all, sn)
        e_buf = jnp.where(buf_mask, jnp.exp2(s_buf - m_all), 0.0)
        l_all = jnp.sum(e_buf, axis=-1, keepdims=True)
        acc = _dot(e_buf, bv_ref[:, gs])
        for r in range(t_valid):
            e_r = jnp.where(new_mask[r], jnp.exp2(s_new[r] - m_all), 0.0)
            l_all = l_all + e_r
            acc = acc + e_r * new_v[r:r + 1, gs]
        ow_ref[g] = acc * _inv_denominator(l_all)
    wk_out[0:w_buf - t_valid, :] = bk_ref[t_valid:w_buf, :]
    wk_out[w_buf - t_valid:w_buf, :] = new_k[0:t_valid]
    wv_out[0:w_buf - t_valid, :] = bv_ref[t_valid:w_buf, :]
    wv_out[w_buf - t_valid:w_buf, :] = new_v[0:t_valid]


def _nsa_sample_a(p3, kc, vc, buf_k, buf_v, *, past, t_valid, n_blocks, nbl):
    b, tp, _ = p3.shape
    m = kc.shape[1]
    w_buf = buf_k.shape[1]
    assert w_buf == WINDOW and t_valid <= w_buf
    kvc = COL_KV // LANES
    rows = NSA_GROUP * tp
    kern = functools.partial(_nsa_sample_a_kernel, past=past, t_valid=t_valid, tp=tp,
                             n_blocks=n_blocks, nbl=nbl)
    per_b3 = lambda bi: (bi, 0, 0)
    per_b4 = lambda bi: (bi, 0, 0, 0)
    return pl.pallas_call(
        kern,
        grid=(b,),
        in_specs=[
            pl.BlockSpec((None, tp, NSA_DIM), lambda bi: (bi, 0, COL_QB // NSA_DIM)),
            pl.BlockSpec((None, m, LANES), per_b3),
            pl.BlockSpec((None, m, LANES), per_b3),
            pl.BlockSpec((None, w_buf, LANES), per_b3),
            pl.BlockSpec((None, w_buf, LANES), per_b3),
            pl.BlockSpec((None, tp, LANES), lambda bi: (bi, 0, kvc + 4)),
            pl.BlockSpec((None, tp, LANES), lambda bi: (bi, 0, kvc + 5)),
        ],
        out_specs=[
            pl.BlockSpec((None, NSA_KV_HEADS, rows, NSA_HEAD_DIM), per_b4),
            pl.BlockSpec((None, NSA_KV_HEADS, rows, NSA_HEAD_DIM), per_b4),
            pl.BlockSpec((None, NSA_KV_HEADS, tp, nbl), per_b4),
            pl.BlockSpec((None, w_buf, LANES), per_b3),
            pl.BlockSpec((None, w_buf, LANES), per_b3),
        ],
        out_shape=[
            jax.ShapeDtypeStruct((b, NSA_KV_HEADS, rows, NSA_HEAD_DIM), F32),
            jax.ShapeDtypeStruct((b, NSA_KV_HEADS, rows, NSA_HEAD_DIM), F32),
            jax.ShapeDtypeStruct((b, NSA_KV_HEADS, tp, nbl), F32),
            jax.ShapeDtypeStruct((b, w_buf, LANES), F32),
            jax.ShapeDtypeStruct((b, w_buf, LANES), F32),
        ],
        compiler_params=pltpu.CompilerParams(
            dimension_semantics=("parallel",), vmem_limit_bytes=VMEM_LIMIT),
    )(p3, kc, vc, buf_k, buf_v, p3, p3)


def _nsa_sample_b_kernel(pt_ref, q_ref, zb_ref, sm_ref, nk_ref, nv_ref, sel_ref, selj_ref, exp_ref,
                         oc_ref, ow_ref, *refs, k_pages, past, t_valid, tp, n_blocks, nbl):
    del pt_ref
    kp = refs[:k_pages]
    vp = refs[k_pages:2 * k_pages]
    o_ref, m_sc, l_sc, acc_sc = refs[2 * k_pages:]
    hd = NSA_HEAD_DIM
    j = pl.program_id(1)

    @pl.when(j == 0)
    def _():
        m_sc[...] = jnp.full(m_sc.shape, NEG_INF, F32)
        l_sc[...] = jnp.zeros(l_sc.shape, F32)
        acc_sc[...] = jnp.zeros(acc_sc.shape, F32)

    q = q_ref[...]
    expand = exp_ref[...]
    groups = range(NSA_KV_HEADS)
    s = []
    for g in groups:
        k_t = jnp.concatenate([r[g] for r in kp], axis=1).astype(BF16)
        sel_bias = ((selj_ref[g] - 1.0) * -NEG_INF).astype(BF16)
        s.append(_dot(_stack_heads(q, g).astype(BF16), k_t) + _tile_rows(_dot(sel_bias, expand)))
    m_run = [m_sc[g] for g in groups]
    m_new = [jnp.maximum(m_run[g], jnp.max(s[g], axis=-1, keepdims=True)) for g in groups]
    alpha = [jnp.exp2(m_run[g] - m_new[g]) for g in groups]
    e = [jnp.exp2(s[g] - m_new[g]) for g in groups]
    l_new = [alpha[g] * l_sc[g] + jnp.sum(e[g], axis=-1, keepdims=True) for g in groups]
    acc_new = []
    for g in groups:
        v_t = jnp.concatenate([r[g] for r in vp], axis=1).astype(BF16)
        acc_new.append(alpha[g] * acc_sc[g] + _dot_nt(e[g].astype(BF16), v_t))
    for g in groups:
        m_sc[g] = m_new[g]
        l_sc[g] = l_new[g]
        acc_sc[g] = acc_new[g]

    @pl.when(j == pl.num_programs(1) - 1)
    def _():
        pos4 = _tile_rows(past + _iota((tp, 1), 0))
        gates = jax.nn.sigmoid(sm_ref[...])
        zb = zb_ref[...]
        new_k = nk_ref[...]
        new_v = nv_ref[...]
        for g in range(NSA_KV_HEADS):
            gs = slice(g * hd, (g + 1) * hd)
            qs = _stack_heads(q, g)
            picked = _tile_rows(sel_ref[g][:, n_blocks - 1:n_blocks] > 0.5)
            m_run, l_run, acc = m_sc[g], l_sc[g], acc_sc[g]
            s_new, new_mask = [], []
            for r in range(t_valid):
                mk = picked & (past + r <= pos4)
                new_mask.append(mk)
                s_new.append(jnp.where(mk, _new_key_scores(qs, new_k, r, gs), NEG_INF))
            m_all = m_run
            for sn in s_new:
                m_all = jnp.maximum(m_all, sn)
            alpha = jnp.exp2(m_run - m_all)
            l_all = alpha * l_run
            acc = alpha * acc
            for r in range(t_valid):
                e_r = jnp.where(new_mask[r], jnp.exp2(s_new[r] - m_all), 0.0)
                l_all = l_all + e_r
                acc = acc + e_r * new_v[r:r + 1, gs]
            o_s = acc * _inv_denominator(l_all)
            _combine_heads(o_ref, g, gates, zb, oc_ref[g], o_s, ow_ref[g], tp)


def _nsa_sample_b(p3, sel, o_c, o_w, pool_k, pool_v, page_table, *, k_pages, past, t_valid, n_blocks, nbl):
    b, tp, _ = p3.shape
    n_pages = page_table.shape[1]
    page = pool_k.shape[3]
    n_steps = n_pages // k_pages
    n_keys = k_pages * page
    per_step = n_keys // SEL_BLOCK
    assert n_pages % k_pages == 0 and page % SEL_BLOCK == 0 and past == n_pages * page
    assert n_blocks == past // SEL_BLOCK + 1 and t_valid <= SEL_BLOCK and per_step <= LANES
    kvc = COL_KV // LANES
    rows = NSA_GROUP * tp
    sel_steps = sel[..., :n_steps * per_step].reshape(b, NSA_KV_HEADS, tp, n_steps, per_step)
    sel_steps = jnp.pad(sel_steps.transpose(0, 3, 1, 2, 4), ((0, 0),) * 4 + ((0, LANES - per_step),))
    expand = (jnp.arange(LANES)[:, None] == (jnp.arange(n_keys) // SEL_BLOCK)[None, :]).astype(BF16)
    blk = (None, NSA_KV_HEADS, NSA_HEAD_DIM, page)

    def page_spec(i):
        return pl.BlockSpec(blk, lambda bi, j, pt: (pt[bi, j * k_pages + i], 0, 0, 0))

    per_b4 = lambda bi, j, pt: (bi, 0, 0, 0)
    in_specs = [
        pl.BlockSpec((None, tp, NSA_DIM), lambda bi, j, pt: (bi, 0, COL_QB // NSA_DIM)),
        pl.BlockSpec((None, tp, NSA_DIM), lambda bi, j, pt: (bi, 0, COL_ZB // NSA_DIM)),
        pl.BlockSpec((None, tp, LANES), lambda bi, j, pt: (bi, 0, COL_SMALL // LANES)),
        pl.BlockSpec((None, tp, LANES), lambda bi, j, pt: (bi, 0, kvc + 2)),
        pl.BlockSpec((None, tp, LANES), lambda bi, j, pt: (bi, 0, kvc + 3)),
        pl.BlockSpec((None, NSA_KV_HEADS, tp, nbl), per_b4),
        pl.BlockSpec((None, None, NSA_KV_HEADS, tp, LANES), lambda bi, j, pt: (bi, j, 0, 0, 0)),
        pl.BlockSpec((LANES, n_keys), lambda bi, j, pt: (0, 0)),
        pl.BlockSpec((None, NSA_KV_HEADS, rows, NSA_HEAD_DIM), per_b4),
        pl.BlockSpec((None, NSA_KV_HEADS, rows, NSA_HEAD_DIM), per_b4),
    ] + [page_spec(i) for i in range(k_pages)] * 2
    kern = functools.partial(_nsa_sample_b_kernel, k_pages=k_pages, past=past, t_valid=t_valid, tp=tp,
                             n_blocks=n_blocks, nbl=nbl)
    return pl.pallas_call(
        kern,
        grid_spec=pltpu.PrefetchScalarGridSpec(
            num_scalar_prefetch=1, grid=(b, n_steps), in_specs=in_specs,
            out_specs=pl.BlockSpec((None, tp, NSA_DIM), lambda bi, j, pt: (bi, 0, 0)),
            scratch_shapes=[pltpu.VMEM((NSA_KV_HEADS, rows, 1), F32), pltpu.VMEM((NSA_KV_HEADS, rows, 1), F32),
                            pltpu.VMEM((NSA_KV_HEADS, rows, NSA_HEAD_DIM), F32)]),
        out_shape=jax.ShapeDtypeStruct((b, tp, NSA_DIM), F32),
        compiler_params=pltpu.CompilerParams(
            dimension_semantics=("parallel", "arbitrary"), vmem_limit_bytes=VMEM_LIMIT),
    )(page_table, p3, p3, p3, p3, p3, sel, sel_steps, expand, o_c, o_w,
      *([pool_k] * k_pages), *([pool_v] * k_pages))


def _merge_kernel(x_ref, oa_ref, ob_ref, ga_ref, gb_ref, wa_ref, wb_ref, wo_ref, fg_ref, y_ref):
    mixed = (jax.nn.sigmoid(ga_ref[...]) * _dot(oa_ref[...].astype(BF16), wa_ref[...])
             + jax.nn.sigmoid(gb_ref[...]) * _dot(ob_ref[...].astype(BF16), wb_ref[...]))
    x_new = x_ref[...] + _dot(mixed.astype(BF16), wo_ref[...])
    y = x_new * lax.rsqrt(jnp.mean(x_new * x_new, axis=-1, keepdims=True) + RMS_EPS)
    y_ref[...] = y * fg_ref[...]


def _merge(x2d, o_a, o_b, p2d, w_a, w_b, w_o, final_gain, tm):
    rows, d = x2d.shape
    const2 = lambda i: (0, 0)
    return pl.pallas_call(
        _merge_kernel,
        grid=(rows // tm,),
        in_specs=[
            pl.BlockSpec((tm, d), lambda i: (i, 0)),
            pl.BlockSpec((tm, GDN_DIM), lambda i: (i, 0)),
            pl.BlockSpec((tm, NSA_DIM), lambda i: (i, 0)),
            pl.BlockSpec((tm, d), lambda i: (i, COL_GATE_A // D_MODEL)),
            pl.BlockSpec((tm, d), lambda i: (i, COL_GATE_B // D_MODEL)),
            pl.BlockSpec((GDN_DIM, d), const2),
            pl.BlockSpec((NSA_DIM, d), const2),
            pl.BlockSpec((d, d), const2),
            pl.BlockSpec((1, d), const2),
        ],
        out_specs=pl.BlockSpec((tm, d), lambda i: (i, 0)),
        out_shape=jax.ShapeDtypeStruct((rows, d), F32),
        compiler_params=pltpu.CompilerParams(
            dimension_semantics=("parallel",), vmem_limit_bytes=VMEM_LIMIT),
    )(x2d, o_a, o_b, p2d, p2d, w_a, w_b, w_o, final_gain)


def _pack_w_in(w_in):
    bounds = [0]
    for width in (GDN_QKV_DIM, GDN_HEADS, GDN_HEADS, GDN_DIM, NSA_DIM, 6 * NSA_KV_DIM, 3 * NSA_HEADS,
                  NSA_DIM, 2 * D_MODEL):
        bounds.append(bounds[-1] + width)
    qkv, a, bt, z_a, q_b, kv_b, g_b, z_b, merge = (w_in[:, bounds[i]:bounds[i + 1]] for i in range(9))
    small = jnp.concatenate([a, bt, g_b], axis=1)
    small = jnp.pad(small, ((0, 0), (0, N_PACKED - COL_SMALL - small.shape[1])))
    packed = jnp.concatenate([qkv, z_a, merge, q_b, z_b, kv_b, small], axis=1)
    assert packed.shape[1] == N_PACKED
    return packed.astype(BF16)


def _lane_vec(v):
    return jnp.pad(v.astype(F32), (0, LANES - v.shape[0])).reshape(1, LANES)


def _block_diag2(w):
    z = jnp.zeros_like(w)
    return jnp.concatenate([jnp.concatenate([w, z], axis=1), jnp.concatenate([z, w], axis=1)], axis=0)


def _tile_for(n, candidates):
    for c in candidates:
        if n % c == 0:
            return c
    raise ValueError(f"no tile for {n}")


def kernel(x_prompt, x_sample, cache_k_cmp, cache_v_cmp, cache_k_slc, cache_v_slc, state_win_k, state_win_v,
           state_conv, state_gdn, page_table, norm_gain, w_in, conv_w, a_log, dt_bias, gdn_gain, cmp_pos_k,
           cmp_w_k, cmp_pos_v, cmp_w_v, w_branch_a, w_branch_b, w_out, final_gain):
    assert w_in.shape[0] == 1, "one layer"
    bp, tp_len, d = x_prompt.shape
    bs, ts, _ = x_sample.shape
    n_pool, page = cache_k_cmp.shape[1:3]
    n_pages = page_table.shape[1]
    past = n_pages * page
    assert past % CMP_STRIDE == 0 and ts < CMP_STRIDE and ts >= CONV_WIDTH - 1 and ts <= SUBLANES
    t_pad = SUBLANES

    w_packed = _pack_w_in(w_in[0])
    gain_in = norm_gain[0].reshape(1, d)
    fgain = final_gain.reshape(1, d)
    w_a = w_branch_a[0].astype(BF16)
    w_b = w_branch_b[0].astype(BF16)
    w_o = w_out[0].astype(BF16)
    cw = conv_w[0]
    alog = _lane_vec(a_log[0])
    dtb = _lane_vec(dt_bias[0])
    ggain = gdn_gain[0].reshape(1, GDN_HEAD_DIM)
    pos_k = jnp.tile(cmp_pos_k[0], (1, NSA_KV_HEADS))
    pos_v = jnp.tile(cmp_pos_v[0], (1, NSA_KV_HEADS))
    wk2 = _block_diag2(cmp_w_k[0])
    wv2 = _block_diag2(cmp_w_v[0])
    tn = _tile_for(N_PACKED, (2048, 1024, 512, 256))
    kvc = COL_KV

    rows_p = bp * tp_len
    x2p = x_prompt.reshape(rows_p, d)
    p2 = _proj(x2p, gain_in, w_packed, _tile_for(rows_p, (1024, 512, 256, 128)), tn)
    p3 = p2.reshape(bp, tp_len, N_PACKED)
    tb = _tile_for(tp_len, (256, 128, 64))
    o_a, gdn_p = _gdn(p3, cw, jnp.zeros((bp, SUBLANES, GDN_QKV_DIM), F32), alog, dtb, ggain,
                      jnp.zeros((bp, GDN_HEADS, GDN_HEAD_DIM, GDN_HEAD_DIM), F32),
                      tb=tb, chunk=GDN_CHUNK, t_valid=tp_len)
    o_b = _nsa_prompt(p3, pos_k, pos_v, wk2, wv2, tq=256, tk=_tile_for(tp_len, (512, 256, 128)))
    y_p = _merge(x2p, o_a.reshape(rows_p, GDN_DIM), o_b.reshape(rows_p, NSA_DIM), p2, w_a, w_b, w_o, fgain,
                 _tile_for(rows_p, (512, 256, 128)))
    y_prompt = y_p.reshape(bp, tp_len, d)

    def kv_out(p, i, t_keep_from, t_to):
        return p[:, t_keep_from:t_to, kvc + i * LANES:kvc + (i + 1) * LANES].reshape(
            1, p.shape[0], t_to - t_keep_from, NSA_KV_HEADS, NSA_HEAD_DIM)

    keep_p = min(WINDOW, tp_len)
    prompt_state = ([kv_out(p3, i, 0, tp_len) for i in range(4)]
                    + [kv_out(p3, i, tp_len - keep_p, tp_len) for i in (4, 5)]
                    + [p3[:, tp_len - (CONV_WIDTH - 1):, :GDN_QKV_DIM][None], gdn_p[None]])

    xs_pad = jnp.pad(x_sample, ((0, 0), (0, t_pad - ts), (0, 0)))
    rows_s = bs * t_pad
    x2s = xs_pad.reshape(rows_s, d)
    ps2 = _proj(x2s, gain_in, w_packed, _tile_for(rows_s, (512, 256, 128, 8)), tn)
    ps3 = ps2.reshape(bs, t_pad, N_PACKED)
    prefix_s = jnp.pad(state_conv[0], ((0, 0), (SUBLANES - (CONV_WIDTH - 1), 0), (0, 0)))
    o_a_s, gdn_s = _gdn(ps3, cw, prefix_s, alog, dtb, ggain, state_gdn[0], tb=t_pad, chunk=t_pad, t_valid=ts)

    pools = [c[0].transpose(0, 2, 3, 1) for c in (cache_k_cmp, cache_v_cmp, cache_k_slc, cache_v_slc)]
    kc_s, vc_s = _cmp_paged(pools[0], pools[1], page_table, pos_k, pos_v, wk2, wv2,
                            k_pages=_tile_for(n_pages, (64, 32, 16, 8, 4, 2, 1)))
    n_blocks = -(-(past + ts) // SEL_BLOCK)
    nbl = -(-n_blocks // LANES) * LANES
    w_buf = state_win_k.shape[2]
    o_c_s, o_w_s, sel_s, win_k_s, win_v_s = _nsa_sample_a(
        ps3, kc_s, vc_s, state_win_k[0].reshape(bs, w_buf, NSA_KV_DIM), state_win_v[0].reshape(bs, w_buf, NSA_KV_DIM),
        past=past, t_valid=ts, n_blocks=n_blocks, nbl=nbl)
    o_b_s = _nsa_sample_b(ps3, sel_s, o_c_s, o_w_s, pools[2], pools[3], page_table,
                          k_pages=_tile_for(n_pages, (64, 32, 16, 8, 4, 2, 1)),
                          past=past, t_valid=ts, n_blocks=n_blocks, nbl=nbl)
    y_s = _merge(x2s, o_a_s.reshape(rows_s, GDN_DIM), o_b_s.reshape(rows_s, NSA_DIM), ps2, w_a, w_b, w_o, fgain,
                 _tile_for(rows_s, (512, 256, 128, 8)))
    y_sample = y_s.reshape(bs, t_pad, d)[:, :ts]

    keep_s = min(WINDOW, w_buf + ts)
    assert keep_s == w_buf
    sample_state = ([kv_out(ps3, i, 0, ts) for i in range(4)]
                    + [win_k_s.reshape(1, bs, keep_s, NSA_KV_HEADS, NSA_HEAD_DIM),
                       win_v_s.reshape(1, bs, keep_s, NSA_KV_HEADS, NSA_HEAD_DIM),
                       ps3[:, ts - (CONV_WIDTH - 1):ts, :GDN_QKV_DIM][None], gdn_s[None]])
    return (y_prompt, y_sample, *prompt_state, *sample_state)
```
